```python
import jax
import jax.numpy as jnp
from jax import lax
import numpy as np

D_MODEL = 1024
BATCH = 8
SEQ = 2048
DEPTH = 2
DEC_BATCH = 16
DEC_SEQ = 32
PAST_LEN = 2048

CHUNK = 64
Q_BLOCK = 128
EPS = 1e-6

A_HEADS = 6
A_DK = 64
A_DV = 64
A_CONV = 4
A_QK = A_HEADS * A_DK
A_V = A_HEADS * A_DV
A_CONV_CH = 2 * A_QK + A_V
A_WIDTH = A_CONV_CH + A_V + 2 * A_HEADS

B_HEADS = 6
B_N = 64
B_C = B_HEADS * B_N
B_W_LORA = 64
B_A_LORA = 64
B_G_LORA = 128
B_WIDTH = 3 * B_C + B_W_LORA + B_A_LORA + B_G_LORA
B_GN_EPS = 64e-5

C_HEADS = 4
C_NOPE = 64
C_ROPE = 32
C_VDIM = 64
C_QK = C_NOPE + C_ROPE
C_Q_LORA = 256
C_KV_LORA = 128
C_WIDTH = C_Q_LORA + C_KV_LORA + C_ROPE
ROPE_THETA = 10000.0

P_TOTAL = A_WIDTH + B_WIDTH + C_WIDTH
D_MIX = A_V + B_C + C_HEADS * C_VDIM

N_GROUPS = 4
EXPERTS_PER_GROUP = 8
N_EXPERTS = N_GROUPS * EXPERTS_PER_GROUP
TOP_K = 2
D_EXPERT = 256

kernel_name = 'hybrid_streaming_encoder_step'

F32 = jnp.float32


def rms_norm(x, g, eps=EPS):
    xf = x.astype(F32)
    y = xf * lax.rsqrt(jnp.mean(xf * xf, axis=-1, keepdims=True) + eps)
    return (y * g.astype(F32)).astype(x.dtype)


def l2_normalize(x, eps=1e-6):
    xf = x.astype(F32)
    return xf * lax.rsqrt(jnp.sum(xf * xf, axis=-1, keepdims=True) + eps)


def causal_dwconv(x, prev, w):
    width = w.shape[0]
    L = x.shape[1]
    xp = jnp.concatenate([prev.astype(x.dtype), x], axis=1)
    y = xp[:, 0:L] * w[0]
    for j in range(1, width):
        y = y + xp[:, j:j + L] * w[j]
    return y, xp[:, L:]


def rope_tables(pos):
    inv = ROPE_THETA ** (-(jnp.arange(0, C_ROPE, 2, dtype=F32) / C_ROPE))
    ang = pos.astype(F32)[:, None] * inv[None, :]
    return jnp.cos(ang), jnp.sin(ang)


def apply_rope(x, cos, sin):
    half = x.shape[-1] // 2
    xf = x.astype(F32)
    x1, x2 = xf[..., :half], xf[..., half:]
    return jnp.concatenate([x1 * cos - x2 * sin, x2 * cos + x1 * sin], axis=-1).astype(x.dtype)


def gated_delta_chunked(q, k, v, g, beta, S0):
    Bsz, L, H, DK = q.shape
    DV = v.shape[-1]
    C = min(CHUNK, L)
    N = L // C

    def to_chunks(t):
        t = t.reshape((Bsz, N, C, H) + t.shape[3:])
        return jnp.moveaxis(t, (1, 3), (0, 2))

    qc, kc, vc = to_chunks(q), to_chunks(k), to_chunks(v)
    gc = jnp.cumsum(to_chunks(g), axis=-1)
    bc = to_chunks(beta)
    tril = jnp.tril(jnp.ones((C, C), bool))
    strict = jnp.tril(jnp.ones((C, C), bool), -1)
    decay = jnp.exp(jnp.where(tril, gc[..., :, None] - gc[..., None, :], -jnp.inf))
    kb = kc * bc[..., None]
    A = jnp.where(strict, jnp.einsum('nbhid,nbhjd->nbhij', kb, kc) * decay, 0.0)
    rhs = jnp.concatenate([vc * bc[..., None], kb * jnp.exp(gc)[..., None]], axis=-1)
    sol = lax.linalg.triangular_solve(A + jnp.eye(C, dtype=F32), rhs, left_side=True,
                                      lower=True, unit_diagonal=True)
    val, kcd = sol[..., :DV], sol[..., DV:]
    attn = jnp.where(tril, jnp.einsum('nbhid,nbhjd->nbhij', qc, kc) * decay, 0.0)
    qg = qc * jnp.exp(gc)[..., None]
    kg = kc * jnp.exp(gc[..., -1:] - gc)[..., None]
    glast = jnp.exp(gc[..., -1])

    def step(S, xs):
        val_i, kcd_i, attn_i, qg_i, kg_i, gl_i = xs
        u = val_i - jnp.einsum('bhck,bhkv->bhcv', kcd_i, S)
        o = jnp.einsum('bhck,bhkv->bhcv', qg_i, S) + jnp.einsum('bhij,bhjv->bhiv', attn_i, u)
        S = S * gl_i[..., None, None] + jnp.einsum('bhck,bhcv->bhkv', kg_i, u)
        return S, o

    S, o = lax.scan(step, S0, (val, kcd, attn, qg, kg, glast))
    o = jnp.moveaxis(o, (0, 2), (1, 3)).reshape(Bsz, L, H, DV)
    return o, S


def wkv7_scan(r, w, k, v, kk, a, S0):
    def step(S, xs):
        r_t, w_t, k_t, v_t, kk_t, a_t = xs
        sa = jnp.einsum('bhvk,bhk->bhv', S, -kk_t)
        S = (S * w_t[:, :, None, :] + sa[..., None] * (kk_t * a_t)[:, :, None, :]
             + v_t[..., None] * k_t[:, :, None, :])
        return S, jnp.einsum('bhvk,bhk->bhv', S, r_t)

    xs = tuple(jnp.moveaxis(t, 1, 0) for t in (r, w, k, v, kk, a))
    S, o = lax.scan(step, S0, xs)
    return jnp.moveaxis(o, 0, 1), S


def chunk_causal_attention(q, k, v, q_pos, k_pos):
    Bsz, Lq, H, d = q.shape
    qb = Q_BLOCK if Lq % Q_BLOCK == 0 else Lq
    nb = Lq // qb
    q_blocks = jnp.moveaxis(q.reshape(Bsz, nb, qb, H, d), 1, 0)
    pos_blocks = q_pos.reshape(nb, qb)
    k_chunk = k_pos // CHUNK
    scale = d ** -0.5

    def attend(args):
        qi, pi = args
        s = jnp.einsum('bqhd,bkhd->bhqk', qi, k).astype(F32) * scale
        mask = k_chunk[None, :] <= (pi // CHUNK)[:, None]
        pr = jax.nn.softmax(jnp.where(mask, s, -jnp.inf), axis=-1)
        return jnp.einsum('bhqk,bkhd->bqhd', pr.astype(v.dtype), v)

    o = lax.map(attend, (q_blocks, pos_blocks))
    return jnp.moveaxis(o, 0, 1).reshape(Bsz, Lq, H, v.shape[-1])


def mixer_a(p, conv_prev, S0, conv_w, A_log, dt_bias, norm_w):
    Bsz, L, _ = p.shape
    qkv, z, a, b = jnp.split(p, [A_CONV_CH, A_CONV_CH + A_V, A_CONV_CH + A_V + A_HEADS], axis=-1)
    qkv, conv_new = causal_dwconv(qkv, conv_prev, conv_w)
    qkv = jax.nn.silu(qkv)
    q, k, v = jnp.split(qkv, [A_QK, 2 * A_QK], axis=-1)
    q = l2_normalize(q.reshape(Bsz, L, A_HEADS, A_DK)) * (A_DK ** -0.5)
    k = l2_normalize(k.reshape(Bsz, L, A_HEADS, A_DK))
    v = v.reshape(Bsz, L, A_HEADS, A_DV).astype(F32)
    beta = jax.nn.sigmoid(b.astype(F32))
    g = -jnp.exp(A_log.astype(F32)) * jax.nn.softplus(a.astype(F32) + dt_bias.astype(F32))
    o, S = gated_delta_chunked(q, k, v, g, beta, S0.astype(F32))
    o = rms_norm(o, norm_w) * jax.nn.silu(z.reshape(Bsz, L, A_HEADS, A_DV).astype(F32))
    return o.reshape(Bsz, L, A_V).astype(p.dtype), conv_new, S.astype(S0.dtype)


def mixer_b(p, shift_prev, S0, mu, w0, w2, a0, a2, g2, k_k, k_a, r_k, ln_w, ln_b):
    Bsz, L, _ = p.shape
    pp = jnp.concatenate([shift_prev.astype(p.dtype), p], axis=1)
    xs = (p + (pp[:, :-1] - p) * mu).astype(F32)
    shift_new = pp[:, -1:]
    r, k, v, xw, xa, xg = jnp.split(
        xs, [B_C, 2 * B_C, 3 * B_C, 3 * B_C + B_W_LORA, 3 * B_C + B_W_LORA + B_A_LORA], axis=-1)
    w = -jax.nn.softplus(-(w0.astype(F32) + jnp.tanh(xw) @ w2.astype(F32))) - 0.5
    decay = jnp.exp(-jnp.exp(w))
    a = jax.nn.sigmoid(a0.astype(F32) + xa @ a2.astype(F32))
    gate = jax.nn.sigmoid(xg) @ g2.astype(F32)
    hs = (Bsz, L, B_HEADS, B_N)
    kk = l2_normalize((k * k_k.astype(F32)).reshape(hs))
    k = k * (1.0 + (a - 1.0) * k_a.astype(F32))
    r4, k4, v4 = r.reshape(hs), k.reshape(hs), v.reshape(hs)
    o, S = wkv7_scan(r4, decay.reshape(hs), k4, v4, kk, a.reshape(hs), S0.astype(F32))
    mean = jnp.mean(o, axis=-1, keepdims=True)
    var = jnp.mean(jnp.square(o - mean), axis=-1, keepdims=True)
    o = ((o - mean) * lax.rsqrt(var + B_GN_EPS)).reshape(Bsz, L, B_C) * ln_w.astype(F32) + ln_b.astype(F32)
    bonus = jnp.sum(r4 * k4 * r_k.astype(F32), axis=-1, keepdims=True) * v4
    o = (o + bonus.reshape(Bsz, L, B_C)) * gate
    return o.astype(p.dtype), shift_new, S.astype(S0.dtype)


def qk_gain(g):
    return jnp.concatenate([g[:C_NOPE], g[C_NOPE:], g[C_NOPE:]])


def mixer_c(p, ckv_past, krope_past, q_norm, kv_norm, w_uq, w_ukv, q_gain, k_gain):
    Bsz, L, _ = p.shape
    past = ckv_past.shape[1]
    cq, ckv_raw, kr_raw = jnp.split(p, [C_Q_LORA, C_Q_LORA + C_KV_LORA], axis=-1)
    q_pos = past + jnp.arange(L)
    cos, sin = rope_tables(q_pos)
    q = (rms_norm(cq, q_norm) @ w_uq).reshape(Bsz, L, C_HEADS, C_QK)
    q = jnp.concatenate([q[..., :C_NOPE], apply_rope(q[..., C_NOPE:], cos[:, None], sin[:, None])], axis=-1)
    ckv = rms_norm(ckv_raw, kv_norm)
    k_rope = apply_rope(kr_raw, cos, sin)
    ckv_all = jnp.concatenate([ckv_past.astype(ckv.dtype), ckv], axis=1)
    kr_all = jnp.concatenate([krope_past.astype(k_rope.dtype), k_rope], axis=1)
    Lk = ckv_all.shape[1]
    kv = (ckv_all @ w_ukv).reshape(Bsz, Lk, C_HEADS, C_NOPE + C_VDIM)
    k = jnp.concatenate([kv[..., :C_NOPE],
                         jnp.broadcast_to(kr_all[:, :, None, :], (Bsz, Lk, C_HEADS, C_ROPE)).astype(kv.dtype)],
                        axis=-1)
    v = kv[..., C_NOPE:]
    q = rms_norm(q, qk_gain(q_gain))
    k = rms_norm(k, qk_gain(k_gain))
    o = chunk_causal_attention(q, k, v, q_pos, jnp.arange(Lk))
    return o.reshape(Bsz, L, C_HEADS * C_VDIM).astype(p.dtype), ckv, k_rope


def hier_moe(x, router_group, router_group_bias, router_expert, router_expert_bias, e_gate, e_up, e_down):
    Bsz, L, D = x.shape
    t = x.reshape(-1, D)
    T = t.shape[0]
    p_group = jax.nn.softmax((t @ router_group).astype(F32) + router_group_bias.astype(F32), axis=-1)
    pg, g_idx = lax.top_k(p_group, 1)
    el = ((t @ router_expert).astype(F32) + router_expert_bias.astype(F32)).reshape(T, N_GROUPS, EXPERTS_PER_GROUP)
    idx = jnp.broadcast_to(g_idx[:, :, None], (T, 1, EXPERTS_PER_GROUP))
    el = jnp.take_along_axis(el, idx, axis=1)[:, 0]
    pv, e_idx = lax.top_k(jax.nn.softmax(el, axis=-1), TOP_K)
    wts = pg * pv / jnp.sum(pv, axis=-1, keepdims=True)
    eid = g_idx * EXPERTS_PER_GROUP + e_idx
    gates = jnp.sum(jax.nn.one_hot(eid, N_EXPERTS, dtype=F32) * wts[..., None], axis=1)
    h = jax.nn.silu(jnp.einsum('td,edf->tef', t, e_gate)) * jnp.einsum('td,edf->tef', t, e_up)
    y = jnp.einsum('tef,efd->td', h * gates[:, :, None].astype(h.dtype), e_down)
    return y.reshape(Bsz, L, D).astype(x.dtype)


def layer(x, conv_prev, delta_prev, shift_prev, wkv_prev, ckv_past, krope_past,
          norm_mix, w_in, a_conv_w, a_A_log, a_dt_bias, a_norm_w,
          b_mu, b_w0, b_w2, b_a0, b_a2, b_g2, b_k_k, b_k_a, b_r_k, b_ln_w, b_ln_b,
          c_q_norm, c_kv_norm, c_w_uq, c_w_ukv, c_q_gain, c_k_gain,
          w_out, norm_ffn, router_group, router_group_bias, router_expert, router_expert_bias,
          e_gate, e_up, e_down):
    p = rms_norm(x, norm_mix) @ w_in
    pa, pb, pc = jnp.split(p, [A_WIDTH, A_WIDTH + B_WIDTH], axis=-1)
    oa, conv_new, delta_new = mixer_a(pa, conv_prev, delta_prev, a_conv_w, a_A_log, a_dt_bias, a_norm_w)
    ob, shift_new, wkv_new = mixer_b(pb, shift_prev, wkv_prev, b_mu, b_w0, b_w2, b_a0, b_a2, b_g2,
                                     b_k_k, b_k_a, b_r_k, b_ln_w, b_ln_b)
    oc, ckv_new, krope_new = mixer_c(pc, ckv_past, krope_past, c_q_norm, c_kv_norm, c_w_uq, c_w_ukv,
                                     c_q_gain, c_k_gain)
    x = x + jnp.concatenate([oa, ob, oc], axis=-1) @ w_out
    x = x + hier_moe(rms_norm(x, norm_ffn), router_group, router_group_bias, router_expert,
                     router_expert_bias, e_gate, e_up, e_down)
    return x, conv_new, delta_new, shift_new, wkv_new, ckv_new, krope_new


def run_trunk(x, states, weights):
    new = [[] for _ in range(6)]
    for l in range(DEPTH):
        out = layer(x, *[s[l] for s in states], *[w[l] for w in weights])
        x = out[0]
        for i in range(6):
            new[i].append(out[i + 1])
    return x, [jnp.stack(n) for n in new]


def setup_inputs(seed: int = 0) -> dict:
    key = jax.random.key(seed)
    ks = iter(jax.random.split(key, 64))

    def nrm(shape, scale):
        return jax.random.normal(next(ks), shape, F32) * scale

    def unif(shape, lo, hi):
        return jax.random.uniform(next(ks), shape, F32, lo, hi)

    return {
        'x_prompt': nrm((BATCH, SEQ, D_MODEL), 1.0),
        'x_sample': nrm((DEC_BATCH, DEC_SEQ, D_MODEL), 1.0),
        'cache_c_kv': nrm((DEPTH, DEC_BATCH, PAST_LEN, C_KV_LORA), 1.0),
        'cache_k_rope': nrm((DEPTH, DEC_BATCH, PAST_LEN, C_ROPE), 1.0),
        'state_conv_a': nrm((DEPTH, DEC_BATCH, A_CONV - 1, A_CONV_CH), 1.0),
        'state_delta_a': nrm((DEPTH, DEC_BATCH, A_HEADS, A_DK, A_DV), 0.1),
        'state_shift_b': nrm((DEPTH, DEC_BATCH, 1, B_WIDTH), 1.0),
        'state_wkv_b': nrm((DEPTH, DEC_BATCH, B_HEADS, B_N, B_N), 0.1),
        'norm_mix': 1.0 + nrm((DEPTH, D_MODEL), 0.02),
        'w_in': nrm((DEPTH, D_MODEL, P_TOTAL), D_MODEL ** -0.5),
        'a_conv_w': nrm((DEPTH, A_CONV, A_CONV_CH), A_CONV ** -0.5),
        'a_A_log': jnp.log(unif((DEPTH, A_HEADS), 1.0, 16.0)),
        'a_dt_bias': nrm((DEPTH, A_HEADS), 0.1),
        'a_norm_w': 1.0 + nrm((DEPTH, A_DV), 0.02),
        'b_mu': unif((DEPTH, B_WIDTH), 0.0, 1.0),
        'b_w0': unif((DEPTH, B_C), -6.0, -1.0),
        'b_w2': nrm((DEPTH, B_W_LORA, B_C), 0.1),
        'b_a0': nrm((DEPTH, B_C), 0.1),
        'b_a2': nrm((DEPTH, B_A_LORA, B_C), 0.1),
        'b_g2': nrm((DEPTH, B_G_LORA, B_C), B_G_LORA ** -0.5),
        'b_k_k': 0.85 + nrm((DEPTH, B_C), 0.02),
        'b_k_a': 1.0 + nrm((DEPTH, B_C), 0.02),
        'b_r_k': nrm((DEPTH, B_HEADS, B_N), 0.1),
        'b_ln_w': 1.0 + nrm((DEPTH, B_C), 0.02),
        'b_ln_b': nrm((DEPTH, B_C), 0.02),
        'c_q_norm': 1.0 + nrm((DEPTH, C_Q_LORA), 0.02),
        'c_kv_norm': 1.0 + nrm((DEPTH, C_KV_LORA), 0.02),
        'c_w_uq': nrm((DEPTH, C_Q_LORA, C_HEADS * C_QK), C_Q_LORA ** -0.5),
        'c_w_ukv': nrm((DEPTH, C_KV_LORA, C_HEADS * (C_NOPE + C_VDIM)), C_KV_LORA ** -0.5),
        'c_q_gain': 1.0 + nrm((DEPTH, C_NOPE + C_ROPE // 2), 0.02),
        'c_k_gain': 1.0 + nrm((DEPTH, C_NOPE + C_ROPE // 2), 0.02),
        'w_out': nrm((DEPTH, D_MIX, D_MODEL), D_MIX ** -0.5),
        'norm_ffn': 1.0 + nrm((DEPTH, D_MODEL), 0.02),
        'router_group': nrm((DEPTH, D_MODEL, N_GROUPS), D_MODEL ** -0.5),
        'router_group_bias': nrm((DEPTH, N_GROUPS), 0.01),
        'router_expert': nrm((DEPTH, D_MODEL, N_EXPERTS), D_MODEL ** -0.5),
        'router_expert_bias': nrm((DEPTH, N_EXPERTS), 0.01),
        'e_gate': nrm((DEPTH, N_EXPERTS, D_MODEL, D_EXPERT), D_MODEL ** -0.5),
        'e_up': nrm((DEPTH, N_EXPERTS, D_MODEL, D_EXPERT), D_MODEL ** -0.5),
        'e_down': nrm((DEPTH, N_EXPERTS, D_EXPERT, D_MODEL), D_EXPERT ** -0.5),
    }


def reference(x_prompt, x_sample, cache_c_kv, cache_k_rope, state_conv_a, state_delta_a,
              state_shift_b, state_wkv_b,
              norm_mix, w_in, a_conv_w, a_A_log, a_dt_bias, a_norm_w,
              b_mu, b_w0, b_w2, b_a0, b_a2, b_g2, b_k_k, b_k_a, b_r_k, b_ln_w, b_ln_b,
              c_q_norm, c_kv_norm, c_w_uq, c_w_ukv, c_q_gain, c_k_gain,
              w_out, norm_ffn, router_group, router_group_bias, router_expert, router_expert_bias,
              e_gate, e_up, e_down):
    weights = (norm_mix, w_in, a_conv_w, a_A_log, a_dt_bias, a_norm_w,
               b_mu, b_w0, b_w2, b_a0, b_a2, b_g2, b_k_k, b_k_a, b_r_k, b_ln_w, b_ln_b,
               c_q_norm, c_kv_norm, c_w_uq, c_w_ukv, c_q_gain, c_k_gain,
               w_out, norm_ffn, router_group, router_group_bias, router_expert, router_expert_bias,
               e_gate, e_up, e_down)
    dt = x_prompt.dtype
    bp = x_prompt.shape[0]
    fresh = (jnp.zeros((DEPTH, bp, A_CONV - 1, A_CONV_CH), dt),
             jnp.zeros((DEPTH, bp, A_HEADS, A_DK, A_DV), dt),
             jnp.zeros((DEPTH, bp, 1, B_WIDTH), dt),
             jnp.zeros((DEPTH, bp, B_HEADS, B_N, B_N), dt),
             jnp.zeros((DEPTH, bp, 0, C_KV_LORA), dt),
             jnp.zeros((DEPTH, bp, 0, C_ROPE), dt))
    y_prompt, new_p = run_trunk(x_prompt, fresh, weights)
    carried = (state_conv_a, state_delta_a, state_shift_b, state_wkv_b, cache_c_kv, cache_k_rope)
    y_sample, new_s = run_trunk(x_sample, carried, weights)
    p_conv, p_delta, p_shift, p_wkv, p_ckv, p_krope = new_p
    s_conv, s_delta, s_shift, s_wkv, s_ckv, s_krope = new_s
    return (y_prompt, y_sample, p_ckv, p_krope, p_conv, p_delta, p_shift, p_wkv,
            s_ckv, s_krope, s_conv, s_delta, s_shift, s_wkv)
```

```python
import functools

import jax
import jax.numpy as jnp
from jax import lax
from jax.experimental import pallas as pl
from jax.experimental.pallas import tpu as pltpu

F32 = jnp.float32
BF16 = jnp.bfloat16
I32 = jnp.int32

D_MODEL = 1024
DEPTH = 2
CHUNK = 64
EPS = 1e-6
LANES = 128

A_HEADS, A_DK, A_DV, A_CONV = 6, 64, 64, 4
A_QK = A_HEADS * A_DK
A_V = A_HEADS * A_DV
A_CONV_CH = 2 * A_QK + A_V
A_WIDTH = A_CONV_CH + A_V + 2 * A_HEADS
A_SEG = A_CONV_CH + A_V + LANES

B_HEADS, B_N = 6, 64
B_C = B_HEADS * B_N
B_W_LORA, B_A_LORA, B_G_LORA = 64, 64, 128
B_WIDTH = 3 * B_C + B_W_LORA + B_A_LORA + B_G_LORA
B_GN_EPS = 64e-5

C_HEADS, C_NOPE, C_ROPE, C_VDIM = 4, 64, 32, 64
C_QK = C_NOPE + C_ROPE
C_Q_LORA, C_KV_LORA = 256, 128
C_WIDTH = C_Q_LORA + C_KV_LORA + C_ROPE
C_SEG = C_Q_LORA + C_KV_LORA + 2 * LANES
C_HPAD = LANES
ROPE_THETA = 10000.0

P_TOTAL = A_WIDTH + B_WIDTH + C_WIDTH
D_MIX = A_V + B_C + C_HEADS * C_VDIM

N_GROUPS, EXPERTS_PER_GROUP = 4, 8
N_EXPERTS = N_GROUPS * EXPERTS_PER_GROUP
TOP_K = 2
D_EXPERT = 256
MOE_TILE = 256
GATHER_ROWS = 256

VMEM_LIMIT = 48 * 1024 * 1024

_NN = (((1,), (0,)), ((), ()))
_NT = (((1,), (1,)), ((), ()))
_TN = (((0,), (0,)), ((), ()))


def _pick(n, prefs):
    for p in prefs:
        if n % p == 0:
            return p
    raise ValueError(f"no tile for {n} in {prefs}")


def _cparams(sem):
    return pltpu.CompilerParams(dimension_semantics=sem, vmem_limit_bytes=VMEM_LIMIT)


def _bdot(a, b, dims=_NN):
    return lax.dot_general(a.astype(BF16), b.astype(BF16), dims, preferred_element_type=F32)


def _split2(a):
    hi = a.astype(BF16)
    lo = (a - hi.astype(F32)).astype(BF16)
    return hi, lo


def _dot3(a, b, dims=_NN):
    ah, al = _split2(a)
    bh, bl = _split2(b)
    f = lambda x, y: lax.dot_general(x, y, dims, preferred_element_type=F32)
    return f(ah, bh) + (f(ah, bl) + f(al, bh))


def _cumsum_rows(x, ltri):
    h = x.astype(BF16)
    r = x - h.astype(F32)
    m = r.astype(BF16)
    l = (r - m.astype(F32)).astype(BF16)
    d = lambda y: jnp.dot(ltri, y, preferred_element_type=F32)
    return d(h) + (d(m) + d(l))


def _tri_masks(c):
    row = lax.broadcasted_iota(I32, (c, c), 0)
    col = lax.broadcasted_iota(I32, (c, c), 1)
    return row >= col, row > col, (row == col).astype(F32)


def _neumann_inv(n, eye):
    c = n.shape[0]
    t = eye + n
    p = n
    k = 2
    while k < c:
        p = _dot3(p, p)
        t = t + _dot3(t, p)
        k *= 2
    return t


def _softplus(x):
    return jnp.maximum(x, 0.0) + jnp.log1p(jnp.exp(-jnp.abs(x)))


def _silu(x):
    return x * jax.nn.sigmoid(x)


def _rms(x, eps=EPS):
    return x * lax.rsqrt(jnp.mean(x * x, axis=-1, keepdims=True) + eps)


def _in_proj_kernel(x_ref, g_ref, w_ref, pa_ref, pb_ref, pc_ref):
    xn = (_rms(x_ref[...]) * g_ref[...]).astype(BF16)
    pa_ref[...] = jnp.dot(xn, w_ref[:, 0:A_SEG], preferred_element_type=F32)
    pb_ref[...] = jnp.dot(xn, w_ref[:, A_SEG:A_SEG + B_WIDTH], preferred_element_type=F32)
    pc_ref[...] = jnp.dot(xn, w_ref[:, A_SEG + B_WIDTH:A_SEG + B_WIDTH + C_SEG], preferred_element_type=F32)


def _in_proj(x, g, w):
    t = x.shape[0]
    tm = _pick(t, (512, 256, 128, 64, 32, 16, 8))
    wtot = A_SEG + B_WIDTH + C_SEG
    return pl.pallas_call(
        _in_proj_kernel,
        grid=(t // tm,),
        in_specs=[
            pl.BlockSpec((tm, D_MODEL), lambda i: (i, 0)),
            pl.BlockSpec((1, D_MODEL), lambda i: (0, 0)),
            pl.BlockSpec((D_MODEL, wtot), lambda i: (0, 0)),
        ],
        out_specs=[
            pl.BlockSpec((tm, A_SEG), lambda i: (i, 0)),
            pl.BlockSpec((tm, B_WIDTH), lambda i: (i, 0)),
            pl.BlockSpec((tm, C_SEG), lambda i: (i, 0)),
        ],
        out_shape=[
            jax.ShapeDtypeStruct((t, A_SEG), F32),
            jax.ShapeDtypeStruct((t, B_WIDTH), F32),
            jax.ShapeDtypeStruct((t, C_SEG), F32),
        ],
        compiler_params=_cparams(("arbitrary",)),
        name="in_proj",
    )(x, g.reshape(1, D_MODEL), w)


def _mixer_a_kernel(pa_ref, cprev_ref, s0_ref, cw_ref, alog_ref, dtb_ref, nw_ref,
                    oa_ref, cnew_ref, snew_ref, tail_ref, s_ref, *, c, n_chunks):
    n = pl.program_id(1)

    @pl.when(n == 0)
    def _():
        tail_ref[...] = jnp.zeros_like(tail_ref)
        tail_ref[8 - (A_CONV - 1):8, :] = cprev_ref[...]
        s_ref[...] = s0_ref[...]

    x = pa_ref[...]
    qkv = x[:, :A_CONV_CH]
    z = x[:, A_CONV_CH:A_CONV_CH + A_V]
    ab = x[:, A_CONV_CH + A_V:A_SEG]

    xp = jnp.concatenate([tail_ref[...], qkv], axis=0)
    cw = cw_ref[...]
    y = qkv * cw[3:4]
    for j in range(A_CONV - 1):
        y = y + xp[8 - (A_CONV - 1) + j:8 - (A_CONV - 1) + j + c] * cw[j:j + 1]
    tail_ref[...] = qkv[c - 8:c]

    @pl.when(n == n_chunks - 1)
    def _():
        cnew_ref[...] = qkv[c - (A_CONV - 1):c]

    act = _silu(y)
    g_all = -jnp.exp(alog_ref[...]) * _softplus(ab + dtb_ref[...])
    beta_all = jax.nn.sigmoid(ab)
    tril, strict, eye = _tri_masks(c)
    ltri = tril.astype(BF16)
    gc_all = _cumsum_rows(g_all, ltri)
    gc_t = gc_all.T
    nw = nw_ref[...]

    outs = []
    for h in range(A_HEADS):
        q = act[:, h * A_DK:(h + 1) * A_DK]
        k = act[:, A_QK + h * A_DK:A_QK + (h + 1) * A_DK]
        v = act[:, 2 * A_QK + h * A_DV:2 * A_QK + (h + 1) * A_DV]
        q = q * lax.rsqrt(jnp.sum(q * q, axis=-1, keepdims=True) + 1e-6) * (A_DK ** -0.5)
        k = k * lax.rsqrt(jnp.sum(k * k, axis=-1, keepdims=True) + 1e-6)
        beta = beta_all[:, A_HEADS + h:A_HEADS + h + 1]
        gcc = gc_all[:, h:h + 1]
        gcr = gc_t[h:h + 1, :]
        dec = jnp.exp(jnp.where(tril, gcc - gcr, -1e30))
        kb = k * beta
        a_mat = jnp.where(strict, _bdot(kb, k, _NT) * dec, 0.0)
        t_inv = _neumann_inv(-a_mat, eye)
        egc = jnp.exp(gcc)
        val = _dot3(t_inv, v * beta)
        kcd = _dot3(t_inv, kb * egc)
        attn = jnp.where(tril, _bdot(q, k, _NT) * dec, 0.0)
        gl = gc_all[c - 1:c, h:h + 1]
        s = s_ref[h]
        u = val - _bdot(kcd, s)
        o = _bdot(q * egc, s) + _bdot(attn, u)
        s_ref[h] = s * jnp.exp(gl) + _bdot(k * jnp.exp(gl - gcc), u, _TN)
        o = _rms(o) * nw
        outs.append(o * _silu(z[:, h * A_DV:(h + 1) * A_DV]))
    oa_ref[...] = jnp.concatenate(outs, axis=1).astype(oa_ref.dtype)

    @pl.when(n == n_chunks - 1)
    def _():
        snew_ref[...] = s_ref[...]


def _row(v, width=LANES):
    v = v.reshape(1, -1).astype(F32)
    return jnp.pad(v, ((0, 0), (0, width - v.shape[1])))


def _mixer_a(pa, row0, bsz, seq, conv_prev, s0, conv_w, a_log, dt_bias, norm_w):
    c = min(CHUNK, seq)
    n_chunks = seq // c
    blk0 = row0 // c
    kern = functools.partial(_mixer_a_kernel, c=c, n_chunks=n_chunks)
    full = lambda shape: pl.BlockSpec(shape, lambda b, n: (0,) * len(shape))
    return pl.pallas_call(
        kern,
        grid=(bsz, n_chunks),
        in_specs=[
            pl.BlockSpec((c, A_SEG), lambda b, n: (blk0 + b * n_chunks + n, 0)),
            pl.BlockSpec((None, A_CONV - 1, A_CONV_CH), lambda b, n: (b, 0, 0)),
            pl.BlockSpec((None, A_HEADS, A_DK, A_DV), lambda b, n: (b, 0, 0, 0)),
            full((A_CONV, A_CONV_CH)),
            full((1, LANES)),
            full((1, LANES)),
            full((1, A_DV)),
        ],
        out_specs=[
            pl.BlockSpec((c, A_V), lambda b, n: (b * n_chunks + n, 0)),
            pl.BlockSpec((None, A_CONV - 1, A_CONV_CH), lambda b, n: (b, 0, 0)),
            pl.BlockSpec((None, A_HEADS, A_DK, A_DV), lambda b, n: (b, 0, 0, 0)),
        ],
        out_shape=[
            jax.ShapeDtypeStruct((bsz * seq, A_V), BF16),
            jax.ShapeDtypeStruct((bsz, A_CONV - 1, A_CONV_CH), F32),
            jax.ShapeDtypeStruct((bsz, A_HEADS, A_DK, A_DV), F32),
        ],
        scratch_shapes=[pltpu.VMEM((8, A_CONV_CH), F32), pltpu.VMEM((A_HEADS, A_DK, A_DV), F32)],
        compiler_params=_cparams(("arbitrary", "arbitrary")),
        name="mixer_a",
    )(pa, conv_prev, s0, conv_w, _row(a_log), _row(dt_bias), norm_w.reshape(1, A_DV))


def _mixer_b_kernel(pb_ref, sprev_ref, s0_ref, mu_ref, w0_ref, w2_ref, a0_ref, a2_ref, g2_ref,
                    kk_ref, ka_ref, rk_ref, lnw_ref, lnb_ref,
                    ob_ref, shnew_ref, snew_ref, last_ref, s_ref, *, c, n_chunks):
    n = pl.program_id(1)

    @pl.when(n == 0)
    def _():
        last_ref[...] = sprev_ref[...]
        s_ref[...] = s0_ref[...]

    p = pb_ref[...]
    rowi = lax.broadcasted_iota(I32, (c, 1), 0)
    prev = jnp.where(rowi == 0, last_ref[...], pltpu.roll(p, 1, 0))
    last_ref[...] = p[c - 1:c]

    @pl.when(n == n_chunks - 1)
    def _():
        shnew_ref[...] = p[c - 1:c]

    xs = p + (prev - p) * mu_ref[...]
    r_all = xs[:, 0:B_C]
    k_all = xs[:, B_C:2 * B_C]
    v_all = xs[:, 2 * B_C:3 * B_C]
    xwa = xs[:, 3 * B_C:3 * B_C + B_W_LORA + B_A_LORA]
    xg = xs[:, 3 * B_C + B_W_LORA + B_A_LORA:B_WIDTH]

    w_log = -_softplus(-(w0_ref[...] + _bdot(jnp.tanh(xwa), w2_ref[...]))) - 0.5
    lw = -jnp.exp(w_log)
    rate = jax.nn.sigmoid(a0_ref[...] + _bdot(xwa, a2_ref[...]))
    gate = _bdot(jax.nn.sigmoid(xg), g2_ref[...])
    kkr = k_all * kk_ref[...]
    k2_all = k_all * (1.0 + (rate - 1.0) * ka_ref[...])

    tril, strict, eye = _tri_masks(c)
    cl = _cumsum_rows(lw, tril.astype(BF16))
    e_cl = jnp.exp(cl)
    e_clp = jnp.exp(cl - lw)
    e_neg = jnp.exp(-cl)
    e_last = e_cl[c - 1:c]
    rk = rk_ref[...]
    lnw = lnw_ref[...]
    lnb = lnb_ref[...]

    outs = []
    for h in range(B_HEADS):
        sl = slice(h * B_N, (h + 1) * B_N)
        kk = kkr[:, sl]
        kk = kk * lax.rsqrt(jnp.sum(kk * kk, axis=-1, keepdims=True) + 1e-6)
        r = r_all[:, sl]
        k2 = k2_all[:, sl]
        v = v_all[:, sl]
        at = -kk * e_clp[:, sl]
        bt = kk * rate[:, sl] * e_neg[:, sl]
        kt = k2 * e_neg[:, sl]
        rt = r * e_cl[:, sl]
        l_ab = jnp.where(strict, _dot3(at, bt, _NT), 0.0)
        m_ak = jnp.where(strict, _dot3(at, kt, _NT), 0.0)
        l_rb = jnp.where(tril, _dot3(rt, bt, _NT), 0.0)
        l_rk = jnp.where(tril, _dot3(rt, kt, _NT), 0.0)
        t_inv = _neumann_inv(l_ab, eye)
        s = s_ref[h]
        u = _dot3(t_inv, _dot3(at, s, _NT) + _dot3(m_ak, v))
        o = _dot3(rt, s, _NT) + _dot3(l_rb, u) + _dot3(l_rk, v)
        s_ref[h] = (s + _dot3(u, bt, _TN) + _dot3(v, kt, _TN)) * e_last[:, sl]
        mean = jnp.mean(o, axis=-1, keepdims=True)
        var = jnp.mean(jnp.square(o - mean), axis=-1, keepdims=True)
        o = (o - mean) * lax.rsqrt(var + B_GN_EPS) * lnw[:, sl] + lnb[:, sl]
        bonus = jnp.sum(r * k2 * rk[:, sl], axis=-1, keepdims=True) * v
        outs.append((o + bonus) * gate[:, sl])
    ob_ref[...] = jnp.concatenate(outs, axis=1).astype(ob_ref.dtype)

    @pl.when(n == n_chunks - 1)
    def _():
        snew_ref[...] = s_ref[...]


def _mixer_b(pb, row0, bsz, seq, shift_prev, s0, mu, w0, w2, a0, a2, g2, k_k, k_a, r_k, ln_w, ln_b):
    c = min(CHUNK, seq)
    n_chunks = seq // c
    blk0 = row0 // c
    kern = functools.partial(_mixer_b_kernel, c=c, n_chunks=n_chunks)
    full = lambda shape: pl.BlockSpec(shape, lambda b, n: (0,) * len(shape))
    lora = B_W_LORA + B_A_LORA
    w2p = jnp.zeros((lora, B_C), F32).at[:B_W_LORA].set(w2).astype(BF16)
    a2p = jnp.zeros((lora, B_C), F32).at[B_W_LORA:].set(a2).astype(BF16)
    vec = lambda v: v.reshape(1, -1).astype(F32)
    return pl.pallas_call(
        kern,
        grid=(bsz, n_chunks),
        in_specs=[
            pl.BlockSpec((c, B_WIDTH), lambda b, n: (blk0 + b * n_chunks + n, 0)),
            pl.BlockSpec((None, 1, B_WIDTH), lambda b, n: (b, 0, 0)),
            pl.BlockSpec((None, B_HEADS, B_N, B_N), lambda b, n: (b, 0, 0, 0)),
            full((1, B_WIDTH)),
            full((1, B_C)), full((lora, B_C)),
            full((1, B_C)), full((lora, B_C)),
            full((B_G_LORA, B_C)),
            full((1, B_C)), full((1, B_C)), full((1, B_C)), full((1, B_C)), full((1, B_C)),
        ],
        out_specs=[
            pl.BlockSpec((c, B_C), lambda b, n: (b * n_chunks + n, 0)),
            pl.BlockSpec((None, 1, B_WIDTH), lambda b, n: (b, 0, 0)),
            pl.BlockSpec((None, B_HEADS, B_N, B_N), lambda b, n: (b, 0, 0, 0)),
        ],
        out_shape=[
            jax.ShapeDtypeStruct((bsz * seq, B_C), BF16),
            jax.ShapeDtypeStruct((bsz, 1, B_WIDTH), F32),
            jax.ShapeDtypeStruct((bsz, B_HEADS, B_N, B_N), F32),
        ],
        scratch_shapes=[pltpu.VMEM((1, B_WIDTH), F32), pltpu.VMEM((B_HEADS, B_N, B_N), F32)],
        compiler_params=_cparams(("arbitrary", "arbitrary")),
        name="mixer_b",
    )(pb, shift_prev, s0, vec(mu), vec(w0), w2p, vec(a0), a2p, g2.astype(BF16),
      vec(k_k), vec(k_a), vec(r_k), vec(ln_w), vec(ln_b))


def _kv_expand(ckvn, krope, wuk_ref, wuv_ref, gk_ref, k_ref, v_ref):
    ck = ckvn.astype(BF16)
    k0 = jnp.dot(ck, wuk_ref[...], preferred_element_type=F32)
    gk = gk_ref[...]
    for h in range(C_HEADS):
        kh = k0[:, h * C_HPAD:(h + 1) * C_HPAD] + krope
        ss = jnp.sum(kh * kh, axis=-1, keepdims=True) * (1.0 / C_QK)
        k_ref[:, h * C_HPAD:(h + 1) * C_HPAD] = (kh * lax.rsqrt(ss + EPS) * gk).astype(k_ref.dtype)
    v_ref[...] = jnp.dot(ck, wuv_ref[...], preferred_element_type=F32).astype(v_ref.dtype)


def _c_prep_kernel(pc_ref, cos_ref, sin_ref, qn_ref, kvn_ref, wq_ref, wqr_ref, wuk_ref, wuv_ref, gq_ref, gk_ref,
                   q_ref, k_ref, v_ref, ckv_ref, kr_ref):
    pc = pc_ref[...]
    cq = pc[:, 0:C_Q_LORA]
    ckv_raw = pc[:, C_Q_LORA:C_Q_LORA + C_KV_LORA]
    krp = pc[:, C_Q_LORA + C_KV_LORA:C_Q_LORA + C_KV_LORA + LANES]
    krr = pc[:, C_Q_LORA + C_KV_LORA + LANES:C_SEG]
    cos = cos_ref[...]
    sin = sin_ref[...]
    lane = lax.broadcasted_iota(I32, cos.shape, 1)
    cosq = jnp.where(lane < C_NOPE, 1.0, cos)

    cqn = (_rms(cq) * qn_ref[...]).astype(BF16)
    q0 = jnp.dot(cqn, wq_ref[...], preferred_element_type=F32)
    q1 = jnp.dot(cqn, wqr_ref[...], preferred_element_type=F32)
    gq = gq_ref[...] * (C_QK ** -0.5)
    for h in range(C_HEADS):
        sl = slice(h * C_HPAD, (h + 1) * C_HPAD)
        qh = q0[:, sl] * cosq + q1[:, sl] * sin
        ss = jnp.sum(qh * qh, axis=-1, keepdims=True) * (1.0 / C_QK)
        q_ref[:, sl] = (qh * lax.rsqrt(ss + EPS) * gq).astype(q_ref.dtype)

    ckvn = _rms(ckv_raw) * kvn_ref[...]
    ckv_ref[...] = ckvn
    krope = krp * cos + krr * sin
    kr_ref[...] = krope[:, C_NOPE:C_NOPE + C_ROPE]
    _kv_expand(ckvn, krope, wuk_ref, wuv_ref, gk_ref, k_ref, v_ref)


def _kv_expand_kernel(ckv_ref, kr_ref, wuk_ref, wuv_ref, gk_ref, k_ref, v_ref):
    _kv_expand(ckv_ref[...], kr_ref[...], wuk_ref, wuv_ref, gk_ref, k_ref, v_ref)


def _c_weights(q_norm, kv_norm, w_uq, w_ukv, q_gain, k_gain):
    half = C_ROPE // 2
    wq = jnp.zeros((C_Q_LORA, C_HEADS * C_HPAD), F32)
    wqr = jnp.zeros((C_Q_LORA, C_HEADS * C_HPAD), F32)
    wuk = jnp.zeros((C_KV_LORA, C_HEADS * C_HPAD), F32)
    wuv = jnp.zeros((C_KV_LORA, C_HEADS * C_VDIM), F32)
    for h in range(C_HEADS):
        wh = w_uq[:, h * C_QK:(h + 1) * C_QK]
        wq = wq.at[:, h * C_HPAD:h * C_HPAD + C_QK].set(wh)
        rot = jnp.concatenate([-wh[:, C_NOPE + half:], wh[:, C_NOPE:C_NOPE + half]], axis=1)
        wqr = wqr.at[:, h * C_HPAD + C_NOPE:h * C_HPAD + C_QK].set(rot)
        kvh = w_ukv[:, h * (C_NOPE + C_VDIM):(h + 1) * (C_NOPE + C_VDIM)]
        wuk = wuk.at[:, h * C_HPAD:h * C_HPAD + C_NOPE].set(kvh[:, :C_NOPE])
        wuv = wuv.at[:, h * C_VDIM:(h + 1) * C_VDIM].set(kvh[:, C_NOPE:])
    gain = lambda g: _row(jnp.concatenate([g[:C_NOPE], g[C_NOPE:], g[C_NOPE:]]))
    return (q_norm.reshape(1, -1), kv_norm.reshape(1, -1), wq.astype(BF16), wqr.astype(BF16),
            wuk.astype(BF16), wuv.astype(BF16), gain(q_gain), gain(k_gain))


def _rope_tables(pos):
    half = C_ROPE // 2
    inv = ROPE_THETA ** (-(jnp.arange(0, C_ROPE, 2, dtype=F32) / C_ROPE))
    ang = pos.astype(F32)[:, None] * inv[None, :]
    cos, sin = jnp.cos(ang), jnp.sin(ang)
    pad = lambda t: jnp.pad(jnp.concatenate([t, t], axis=1), ((0, 0), (C_NOPE, LANES - C_NOPE - 2 * half)))
    return pad(cos), pad(sin)


def _c_prep(pc, cos, sin, cw):
    t = pc.shape[0]
    tm = _pick(t, (512, 256, 128, 64, 32, 16, 8))
    qn, kvn, wq, wqr, wuk, wuv, gq, gk = cw
    full = lambda a: pl.BlockSpec(a.shape, lambda i: (0,) * a.ndim)
    rows = lambda w: pl.BlockSpec((tm, w), lambda i: (i, 0))
    return pl.pallas_call(
        _c_prep_kernel,
        grid=(t // tm,),
        in_specs=[rows(C_SEG), rows(LANES), rows(LANES)] + [full(a) for a in (qn, kvn, wq, wqr, wuk, wuv, gq, gk)],
        out_specs=[rows(C_HEADS * C_HPAD), rows(C_HEADS * C_HPAD), rows(C_HEADS * C_VDIM), rows(C_KV_LORA),
                   rows(C_ROPE)],
        out_shape=[
            jax.ShapeDtypeStruct((t, C_HEADS * C_HPAD), BF16),
            jax.ShapeDtypeStruct((t, C_HEADS * C_HPAD), BF16),
            jax.ShapeDtypeStruct((t, C_HEADS * C_VDIM), BF16),
            jax.ShapeDtypeStruct((t, C_KV_LORA), F32),
            jax.ShapeDtypeStruct((t, C_ROPE), F32),
        ],
        compiler_params=_cparams(("arbitrary",)),
        name="c_prep",
    )(pc, cos, sin, qn, kvn, wq, wqr, wuk, wuv, gq, gk)


def _kv_expand_call(ckv, krp, cw):
    t = ckv.shape[0]
    tm = _pick(t, (1024, 512, 256, 128, 64, 32, 16, 8))
    _, _, _, _, wuk, wuv, _, gk = cw
    full = lambda a: pl.BlockSpec(a.shape, lambda i: (0,) * a.ndim)
    rows = lambda w: pl.BlockSpec((tm, w), lambda i: (i, 0))
    return pl.pallas_call(
        _kv_expand_kernel,
        grid=(t // tm,),
        in_specs=[rows(C_KV_LORA), rows(LANES), full(wuk), full(wuv), full(gk)],
        out_specs=[rows(C_HEADS * C_HPAD), rows(C_HEADS * C_VDIM)],
        out_shape=[
            jax.ShapeDtypeStruct((t, C_HEADS * C_HPAD), BF16),
            jax.ShapeDtypeStruct((t, C_HEADS * C_VDIM), BF16),
        ],
        compiler_params=_cparams(("arbitrary",)),
        name="kv_expand",
    )(ckv, krp, wuk, wuv, gk)


def _attn_kernel(*refs, tq, past, has_cache):
    if has_cache:
        q_ref, kn_ref, vn_ref, kc_ref, vc_ref, o_ref = refs
    else:
        q_ref, kn_ref, vn_ref, o_ref = refs
    qi = pl.program_id(1)
    q = q_ref[...]
    kn = kn_ref[...]
    vn = vn_ref[...]
    seq = kn.shape[0]
    shift = CHUNK.bit_length() - 1
    q_chunk = jnp.right_shift(past + qi * tq + lax.broadcasted_iota(I32, (tq, 1), 0), shift)
    vis_n = jnp.right_shift(past + lax.broadcasted_iota(I32, (1, seq), 1), shift) <= q_chunk
    if has_cache:
        kc = kc_ref[...]
        vc = vc_ref[...]
        vis_c = jnp.right_shift(lax.broadcasted_iota(I32, (1, past), 1), shift) <= q_chunk
    outs = []
    for h in range(C_HEADS):
        sl = slice(h * C_HPAD, (h + 1) * C_HPAD)
        vsl = slice(h * C_VDIM, (h + 1) * C_VDIM)
        qh = q[:, sl]
        s_n = jnp.where(vis_n, lax.dot_general(qh, kn[:, sl], _NT, preferred_element_type=F32), -1e30)
        m = jnp.max(s_n, axis=-1, keepdims=True)
        if has_cache:
            s_c = jnp.where(vis_c, lax.dot_general(qh, kc[:, sl], _NT, preferred_element_type=F32), -1e30)
            m = jnp.maximum(m, jnp.max(s_c, axis=-1, keepdims=True))
        p_n = jnp.exp(s_n - m)
        den = jnp.sum(p_n, axis=-1, keepdims=True)
        acc = jnp.dot(p_n.astype(BF16), vn[:, vsl], preferred_element_type=F32)
        if has_cache:
            p_c = jnp.exp(s_c - m)
            den = den + jnp.sum(p_c, axis=-1, keepdims=True)
            acc = acc + jnp.dot(p_c.astype(BF16), vc[:, vsl], preferred_element_type=F32)
        outs.append(acc / den)
    o_ref[...] = jnp.concatenate(outs, axis=1).astype(o_ref.dtype)


def _attention(q, k, v, row0, bsz, seq, k_cache=None, v_cache=None):
    has_cache = k_cache is not None
    past = k_cache.shape[0] // bsz if has_cache else 0
    tq = _pick(seq, (256, 128, 64, 32, 16, 8))
    nq = seq // tq
    kern = functools.partial(_attn_kernel, tq=tq, past=past, has_cache=has_cache)
    qw, vw = C_HEADS * C_HPAD, C_HEADS * C_VDIM
    in_specs = [
        pl.BlockSpec((tq, qw), lambda b, i: (row0 // tq + b * nq + i, 0)),
        pl.BlockSpec((seq, qw), lambda b, i: (row0 // seq + b, 0)),
        pl.BlockSpec((seq, vw), lambda b, i: (row0 // seq + b, 0)),
    ]
    args = [q, k, v]
    if has_cache:
        in_specs += [pl.BlockSpec((past, qw), lambda b, i: (b, 0)), pl.BlockSpec((past, vw), lambda b, i: (b, 0))]
        args += [k_cache, v_cache]
    return pl.pallas_call(
        kern,
        grid=(bsz, nq),
        in_specs=in_specs,
        out_specs=pl.BlockSpec((tq, vw), lambda b, i: (b * nq + i, 0)),
        out_shape=jax.ShapeDtypeStruct((bsz * seq, vw), BF16),
        compiler_params=_cparams(("arbitrary", "arbitrary")),
        name="attention",
    )(*args)


def _out_proj_kernel(x_ref, oap_ref, obp_ref, ocp_ref, oas_ref, obs_ref, ocs_ref, w_ref, g_ref, wr_ref, br_ref,
                     y_ref, xn_ref, ri_ref, rw_ref, *, n_prompt_tiles):
    is_p = pl.program_id(0) < n_prompt_tiles
    oa = jnp.where(is_p, oap_ref[...], oas_ref[...])
    ob = jnp.where(is_p, obp_ref[...], obs_ref[...])
    oc = jnp.where(is_p, ocp_ref[...], ocs_ref[...])
    y = x_ref[...]
    y = y + jnp.dot(oa, w_ref[0:A_V], preferred_element_type=F32)
    y = y + jnp.dot(ob, w_ref[A_V:A_V + B_C], preferred_element_type=F32)
    y = y + jnp.dot(oc, w_ref[A_V + B_C:D_MIX], preferred_element_type=F32)
    y_ref[...] = y
    xn = _rms(y) * g_ref[...]
    xn_ref[...] = xn

    logits = _dot3(xn, wr_ref[...]) + br_ref[...]
    lane = lax.broadcasted_iota(I32, logits.shape, 1)
    lanef = lane.astype(F32)
    neg = -1e30
    lg = jnp.where(lane < N_GROUPS, logits, neg)
    mg = jnp.max(lg, axis=-1, keepdims=True)
    pg = 1.0 / jnp.sum(jnp.exp(lg - mg), axis=-1, keepdims=True)
    gidx = jnp.min(jnp.where(lg == mg, lanef, float(LANES)), axis=-1, keepdims=True)
    lo = N_GROUPS + EXPERTS_PER_GROUP * gidx
    in_grp = (lanef >= lo) & (lanef < lo + EXPERTS_PER_GROUP)
    el = jnp.where(in_grp, logits, neg)
    m1 = jnp.max(el, axis=-1, keepdims=True)
    i1 = jnp.min(jnp.where(el == m1, lanef, float(LANES)), axis=-1, keepdims=True)
    el2 = jnp.where(lanef == i1, neg, el)
    m2 = jnp.max(el2, axis=-1, keepdims=True)
    i2 = jnp.min(jnp.where(el2 == m2, lanef, float(LANES)), axis=-1, keepdims=True)
    den = jnp.sum(jnp.exp(el - m1), axis=-1, keepdims=True)
    p1 = 1.0 / den
    p2 = jnp.exp(m2 - m1) / den
    w1 = pg * p1 / (p1 + p2)
    w2 = pg * p2 / (p1 + p2)
    e1 = (i1 - N_GROUPS).astype(I32)
    e2 = (i2 - N_GROUPS).astype(I32)
    ri_ref[...] = jnp.where(lane == 0, e1, jnp.where(lane == 1, e2, 0))
    rw_ref[...] = jnp.where(lane == 0, w1, jnp.where(lane == 1, w2, 0.0))


def _out_proj(x, mix_p, mix_s, w_out, g, w_router, b_router):
    t = x.shape[0]
    tp, ts = mix_p[0].shape[0], mix_s[0].shape[0]
    tm = _pick(ts, (512, 256, 128, 64, 32, 16, 8))
    assert tp % tm == 0 and tp + ts == t
    npt = tp // tm
    kern = functools.partial(_out_proj_kernel, n_prompt_tiles=npt)
    pspec = lambda w: pl.BlockSpec((tm, w), lambda i: (jnp.minimum(i, npt - 1), 0))
    sspec = lambda w: pl.BlockSpec((tm, w), lambda i: (jnp.maximum(i - npt, 0), 0))
    full = lambda a: pl.BlockSpec(a.shape, lambda i: (0,) * a.ndim)
    rows = lambda w: pl.BlockSpec((tm, w), lambda i: (i, 0))
    cw = C_HEADS * C_VDIM
    g2 = g.reshape(1, D_MODEL)
    return pl.pallas_call(
        kern,
        grid=(t // tm,),
        in_specs=[rows(D_MODEL), pspec(A_V), pspec(B_C), pspec(cw), sspec(A_V), sspec(B_C), sspec(cw),
                  full(w_out), full(g2), full(w_router), full(b_router)],
        out_specs=[rows(D_MODEL), rows(D_MODEL), rows(LANES), rows(LANES)],
        out_shape=[
            jax.ShapeDtypeStruct((t, D_MODEL), F32),
            jax.ShapeDtypeStruct((t, D_MODEL), F32),
            jax.ShapeDtypeStruct((t, LANES), I32),
            jax.ShapeDtypeStruct((t, LANES), F32),
        ],
        compiler_params=_cparams(("arbitrary",)),
        name="out_proj_router",
    )(x, *mix_p, *mix_s, w_out, g2, w_router, b_router)


def _gather_kernel(idx_ref, src_ref, dst_ref, sem, *, rows):
    base = pl.program_id(0) * rows

    def issue(r, carry):
        pltpu.make_async_copy(src_ref.at[pl.ds(idx_ref[0, r], 1)], dst_ref.at[pl.ds(base + r, 1)], sem).start()
        return carry

    lax.fori_loop(0, rows, issue, 0)

    def drain(r, carry):
        pltpu.make_async_copy(src_ref.at[pl.ds(0, 1)], dst_ref.at[pl.ds(base, 1)], sem).wait()
        return carry

    lax.fori_loop(0, rows, drain, 0)


def _gather_rows(src, idx):
    m = idx.shape[0]
    rows = _pick(m, (GATHER_ROWS, 128, 64, 32, 16, 8))
    kern = functools.partial(_gather_kernel, rows=rows)
    return pl.pallas_call(
        kern,
        grid=(m // rows,),
        in_specs=[
            pl.BlockSpec((None, 1, rows), lambda i: (i, 0, 0), memory_space=pltpu.SMEM),
            pl.BlockSpec(memory_space=pl.ANY),
        ],
        out_specs=pl.BlockSpec(memory_space=pl.ANY),
        out_shape=jax.ShapeDtypeStruct((m,) + src.shape[1:], src.dtype),
        scratch_shapes=[pltpu.SemaphoreType.DMA(())],
        compiler_params=_cparams(("arbitrary",)),
        name="gather_rows",
    )(idx.reshape(m // rows, 1, rows), src)


def _expert_kernel(te_ref, nu_ref, x_ref, wg_ref, wu_ref, wd_ref, o_ref):
    i = pl.program_id(0)

    @pl.when(i < nu_ref[0])
    def _():
        x = x_ref[...].astype(BF16)
        hg = jnp.dot(x, wg_ref[...].astype(BF16), preferred_element_type=F32)
        hu = jnp.dot(x, wu_ref[...].astype(BF16), preferred_element_type=F32)
        hidden = (_silu(hg) * hu).astype(BF16)
        o_ref[...] = jnp.dot(hidden, wd_ref[...].astype(BF16), preferred_element_type=F32)

    @pl.when(i >= nu_ref[0])
    def _():
        o_ref[...] = jnp.zeros_like(o_ref)


def _expert_ffn(xg, tile_expert, n_used, e_gate, e_up, e_down):
    np_rows = xg.shape[0]
    nt = np_rows // MOE_TILE
    grid_spec = pltpu.PrefetchScalarGridSpec(
        num_scalar_prefetch=2,
        grid=(nt,),
        in_specs=[
            pl.BlockSpec((MOE_TILE, D_MODEL), lambda i, te, nu: (i, 0)),
            pl.BlockSpec((None, D_MODEL, D_EXPERT), lambda i, te, nu: (te[i], 0, 0)),
            pl.BlockSpec((None, D_MODEL, D_EXPERT), lambda i, te, nu: (te[i], 0, 0)),
            pl.BlockSpec((None, D_EXPERT, D_MODEL), lambda i, te, nu: (te[i], 0, 0)),
        ],
        out_specs=pl.BlockSpec((MOE_TILE, D_MODEL), lambda i, te, nu: (i, 0)),
    )
    return pl.pallas_call(
        _expert_kernel,
        grid_spec=grid_spec,
        out_shape=jax.ShapeDtypeStruct((np_rows, D_MODEL), F32),
        compiler_params=_cparams(("arbitrary",)),
        name="expert_ffn",
    )(tile_expert, n_used, xg, e_gate, e_up, e_down)


def _combine_kernel(x_ref, og_ref, rw_ref, y_ref):
    rw = rw_ref[...]
    og = og_ref[...]
    y_ref[...] = x_ref[...] + rw[:, 0:1] * og[:, 0:D_MODEL] + rw[:, 1:2] * og[:, D_MODEL:2 * D_MODEL]


def _combine(x, og, rw):
    t = x.shape[0]
    tm = _pick(t, (512, 256, 128, 64, 32, 16, 8))
    rows = lambda w: pl.BlockSpec((tm, w), lambda i: (i, 0))
    return pl.pallas_call(
        _combine_kernel,
        grid=(t // tm,),
        in_specs=[rows(D_MODEL), rows(TOP_K * D_MODEL), rows(LANES)],
        out_specs=rows(D_MODEL),
        out_shape=jax.ShapeDtypeStruct((t, D_MODEL), F32),
        compiler_params=_cparams(("arbitrary",)),
        name="moe_combine",
    )(x, og, rw)


def _route_tables(eid):
    flat = eid.reshape(-1)
    n_assign = flat.shape[0]
    onehot = (flat[:, None] == jnp.arange(N_EXPERTS, dtype=I32)[None, :]).astype(I32)
    csum = jnp.cumsum(onehot, axis=0)
    rank = jnp.take_along_axis(csum, flat[:, None], axis=1)[:, 0] - 1
    counts = csum[-1]
    padded = ((counts + MOE_TILE - 1) // MOE_TILE) * MOE_TILE
    ends = jnp.cumsum(padded)
    pos = (ends - padded)[flat] + rank
    np_rows = n_assign + N_EXPERTS * MOE_TILE
    np_rows = ((np_rows + GATHER_ROWS - 1) // GATHER_ROWS) * GATHER_ROWS
    token_of_pos = jnp.zeros((np_rows,), I32).at[pos].set(jnp.arange(n_assign, dtype=I32) // TOP_K)
    tile_start = jnp.arange(np_rows // MOE_TILE, dtype=I32) * MOE_TILE
    tile_expert = jnp.minimum(jnp.searchsorted(ends, tile_start, side="right"), N_EXPERTS - 1).astype(I32)
    n_used = (ends[-1] // MOE_TILE).astype(I32).reshape(1)
    return pos.astype(I32), token_of_pos, tile_expert, n_used


def _moe(x, xn, ri, rw, e_gate, e_up, e_down):
    t = x.shape[0]
    pos, token_of_pos, tile_expert, n_used = _route_tables(ri[:, :TOP_K])
    tiles = lambda a: a.reshape(a.shape[0], D_MODEL // LANES, LANES)
    xg = _gather_rows(tiles(xn), token_of_pos).reshape(-1, D_MODEL)
    out_sorted = _expert_ffn(xg, tile_expert, n_used, e_gate, e_up, e_down)
    og = _gather_rows(tiles(out_sorted), pos).reshape(t, TOP_K * D_MODEL)
    return _combine(x, og, rw)


def _in_weights(w_in):
    half = C_ROPE // 2
    zeros = lambda n: jnp.zeros((D_MODEL, n), F32)
    c0 = A_WIDTH + B_WIDTH
    kr = w_in[:, c0 + C_Q_LORA + C_KV_LORA:c0 + C_WIDTH]
    kr_rot = jnp.concatenate([-kr[:, half:], kr[:, :half]], axis=1)
    tail = LANES - C_NOPE - C_ROPE
    cols = [
        w_in[:, :A_CONV_CH + A_V], w_in[:, A_CONV_CH + A_V:A_WIDTH], zeros(LANES - 2 * A_HEADS),
        w_in[:, A_WIDTH:c0],
        w_in[:, c0:c0 + C_Q_LORA + C_KV_LORA],
        zeros(C_NOPE), kr, zeros(tail),
        zeros(C_NOPE), kr_rot, zeros(tail),
    ]
    return jnp.concatenate(cols, axis=1).astype(BF16)


def _router_weights(rg, rgb, re, reb):
    w = jnp.zeros((D_MODEL, LANES), F32).at[:, :N_GROUPS].set(rg).at[:, N_GROUPS:N_GROUPS + N_EXPERTS].set(re)
    b = jnp.zeros((1, LANES), F32).at[0, :N_GROUPS].set(rgb).at[0, N_GROUPS:N_GROUPS + N_EXPERTS].set(reb)
    return w, b


def _layer(x, geom, st_p, st_s, cos, sin, wts):
    (norm_mix, w_in, a_conv_w, a_A_log, a_dt_bias, a_norm_w,
     b_mu, b_w0, b_w2, b_a0, b_a2, b_g2, b_k_k, b_k_a, b_r_k, b_ln_w, b_ln_b,
     c_q_norm, c_kv_norm, c_w_uq, c_w_ukv, c_q_gain, c_k_gain,
     w_out, norm_ffn, router_group, router_group_bias, router_expert, router_expert_bias,
     e_gate, e_up, e_down) = wts
    bp, lp, bs, ls, past = geom
    tp = bp * lp

    pa, pb, pc = _in_proj(x, norm_mix, _in_weights(w_in))
    cw = _c_weights(c_q_norm, c_kv_norm, c_w_uq, c_w_ukv, c_q_gain, c_k_gain)
    q, k, v, ckv, krope = _c_prep(pc, cos, sin, cw)

    a_args = (a_conv_w, a_A_log, a_dt_bias, a_norm_w)
    b_args = (b_mu, b_w0, b_w2, b_a0, b_a2, b_g2, b_k_k, b_k_a, b_r_k.reshape(-1), b_ln_w, b_ln_b)
    conv_p, delta_p, shift_p, wkv_p = st_p
    conv_s, delta_s, shift_s, wkv_s, ckv_past, krope_past = st_s

    oa_p, conv_np, delta_np = _mixer_a(pa, 0, bp, lp, conv_p, delta_p, *a_args)
    oa_s, conv_ns, delta_ns = _mixer_a(pa, tp, bs, ls, conv_s, delta_s, *a_args)
    ob_p, shift_np, wkv_np = _mixer_b(pb, 0, bp, lp, shift_p, wkv_p, *b_args)
    ob_s, shift_ns, wkv_ns = _mixer_b(pb, tp, bs, ls, shift_s, wkv_s, *b_args)

    oc_p = _attention(q, k, v, 0, bp, lp)
    krp_past = jnp.pad(krope_past.reshape(bs * past, C_ROPE), ((0, 0), (C_NOPE, LANES - C_NOPE - C_ROPE)))
    k_cache, v_cache = _kv_expand_call(ckv_past.reshape(bs * past, C_KV_LORA), krp_past, cw)
    oc_s = _attention(q, k, v, tp, bs, ls, k_cache, v_cache)

    w_router, b_router = _router_weights(router_group, router_group_bias, router_expert, router_expert_bias)
    y, xn, ri, rw = _out_proj(x, (oa_p, ob_p, oc_p), (oa_s, ob_s, oc_s), w_out.astype(BF16), norm_ffn,
                              w_router, b_router)
    x_new = _moe(y, xn, ri, rw, e_gate, e_up, e_down)

    new_p = (conv_np, delta_np, shift_np, wkv_np, ckv[:tp].reshape(bp, lp, C_KV_LORA),
             krope[:tp].reshape(bp, lp, C_ROPE))
    new_s = (conv_ns, delta_ns, shift_ns, wkv_ns, ckv[tp:].reshape(bs, ls, C_KV_LORA),
             krope[tp:].reshape(bs, ls, C_ROPE))
    return x_new, new_p, new_s


def _forward(x_prompt, x_sample, cache_c_kv, cache_k_rope, state_conv_a, state_delta_a, state_shift_b,
             state_wkv_b, weights):
    bp, lp, _ = x_prompt.shape
    bs, ls, _ = x_sample.shape
    depth = cache_c_kv.shape[0]
    past = cache_c_kv.shape[2]
    geom = (bp, lp, bs, ls, past)
    x = jnp.concatenate([x_prompt.reshape(bp * lp, D_MODEL), x_sample.reshape(bs * ls, D_MODEL)], axis=0)
    pos = jnp.concatenate([jnp.tile(jnp.arange(lp), bp), jnp.tile(past + jnp.arange(ls), bs)])
    cos, sin = _rope_tables(pos)
    zeros = lambda *s: jnp.zeros(s, F32)
    st_p = (zeros(bp, A_CONV - 1, A_CONV_CH), zeros(bp, A_HEADS, A_DK, A_DV), zeros(bp, 1, B_WIDTH),
            zeros(bp, B_HEADS, B_N, B_N))
    news_p, news_s = [], []
    for l in range(depth):
        st_s = (state_conv_a[l], state_delta_a[l], state_shift_b[l], state_wkv_b[l], cache_c_kv[l], cache_k_rope[l])
        x, new_p, new_s = _layer(x, geom, st_p, st_s, cos, sin, [w[l] for w in weights])
        news_p.append(new_p)
        news_s.append(new_s)
    stack = lambda news, i: jnp.stack([n[i] for n in news])
    y_prompt = x[:bp * lp].reshape(bp, lp, D_MODEL)
    y_sample = x[bp * lp:].reshape(bs, ls, D_MODEL)
    p_conv, p_delta, p_shift, p_wkv, p_ckv, p_krope = (stack(news_p, i) for i in range(6))
    s_conv, s_delta, s_shift, s_wkv, s_ckv, s_krope = (stack(news_s, i) for i in range(6))
    return (y_prompt, y_sample, p_ckv, p_krope, p_conv, p_delta, p_shift, p_wkv,
            s_ckv, s_krope, s_conv, s_delta, s_shift, s_wkv)


def kernel(x_prompt, x_sample, cache_c_kv, cache_k_rope, state_conv_a, state_delta_a, state_shift_b, state_wkv_b,
           norm_mix, w_in, a_conv_w, a_A_log, a_dt_bias, a_norm_w,
           b_mu, b_w0, b_w2, b_a0, b_a2, b_g2, b_k_k, b_k_a, b_r_k, b_ln_w, b_ln_b,
           c_q_norm, c_kv_norm, c_w_uq, c_w_ukv, c_q_gain, c_k_gain,
           w_out, norm_ffn, router_group, router_group_bias, router_expert, router_expert_bias,
           e_gate, e_up, e_down):
    weights = (norm_mix, w_in, a_conv_w, a_A_log, a_dt_bias, a_norm_w,
               b_mu, b_w0, b_w2, b_a0, b_a2, b_g2, b_k_k, b_k_a, b_r_k, b_ln_w, b_ln_b,
               c_q_norm, c_kv_norm, c_w_uq, c_w_ukv, c_q_gain, c_k_gain,
               w_out, norm_ffn, router_group, router_group_bias, router_expert, router_expert_bias,
               e_gate, e_up, e_down)
    return _forward(x_prompt, x_sample, cache_c_kv, cache_k_rope, state_conv_a, state_delta_a, state_shift_b,
                    state_wkv_b, weights)
```

```python
import functools

import jax
import jax.numpy as jnp
from jax import lax
from jax.experimental import pallas as pl
from jax.experimental.pallas import tpu as pltpu

F32 = jnp.float32
BF16 = jnp.bfloat16
I32 = jnp.int32

D_MODEL = 1024
DEPTH = 2
CHUNK = 64
EPS = 1e-6
LANES = 128
SUBLANES = 8
ROW_TILES = D_MODEL // LANES

A_HEADS, A_DK, A_DV, A_CONV = 6, 64, 64, 4
A_QK = A_HEADS * A_DK
A_V = A_HEADS * A_DV
A_CONV_CH = 2 * A_QK + A_V
A_WIDTH = A_CONV_CH + A_V + 2 * A_HEADS
A_SEG = A_CONV_CH + A_V + LANES

B_HEADS, B_N = 6, 64
B_C = B_HEADS * B_N
B_W_LORA, B_A_LORA, B_G_LORA = 64, 64, 128
B_WIDTH = 3 * B_C + B_W_LORA + B_A_LORA + B_G_LORA
B_GN_EPS = 64e-5

C_HEADS, C_NOPE, C_ROPE, C_VDIM = 4, 64, 32, 64
C_QK = C_NOPE + C_ROPE
C_Q_LORA, C_KV_LORA = 256, 128
C_WIDTH = C_Q_LORA + C_KV_LORA + C_ROPE
C_SEG = C_Q_LORA + C_KV_LORA + 2 * LANES
C_HPAD = LANES
ROPE_THETA = 10000.0

P_TOTAL = A_WIDTH + B_WIDTH + C_WIDTH
D_MIX = A_V + B_C + C_HEADS * C_VDIM

N_GROUPS, EXPERTS_PER_GROUP = 4, 8
N_EXPERTS = N_GROUPS * EXPERTS_PER_GROUP
TOP_K = 2
D_EXPERT = 256
MOE_TILE = 256
MIXER_CHUNKS_PER_STEP = 4

VMEM_LIMIT = 48 * 1024 * 1024

_NN = (((1,), (0,)), ((), ()))
_NT = (((1,), (1,)), ((), ()))
_TN = (((0,), (0,)), ((), ()))


def _pick(n, prefs):
    for p in prefs:
        if n % p == 0:
            return p
    raise ValueError(f"no tile for {n} in {prefs}")


def _cparams(sem):
    return pltpu.CompilerParams(dimension_semantics=sem, vmem_limit_bytes=VMEM_LIMIT)


def _bdot(a, b, dims=_NN):
    return lax.dot_general(a.astype(BF16), b.astype(BF16), dims, preferred_element_type=F32)


def _split2(a):
    hi = a.astype(BF16)
    lo = (a - hi.astype(F32)).astype(BF16)
    return hi, lo


def _dot3(a, b, dims=_NN):
    ah, al = _split2(a)
    bh, bl = _split2(b)
    f = lambda x, y: lax.dot_general(x, y, dims, preferred_element_type=F32)
    return f(ah, bh) + (f(ah, bl) + f(al, bh))


def _mdot(precise):
    return _dot3 if precise else _bdot


def _cumsum_rows(x, ltri):
    h = x.astype(BF16)
    r = x - h.astype(F32)
    m = r.astype(BF16)
    l = (r - m.astype(F32)).astype(BF16)
    d = lambda y: jnp.dot(ltri, y, preferred_element_type=F32)
    return d(h) + (d(m) + d(l))


def _tri_masks(c):
    row = lax.broadcasted_iota(I32, (c, c), 0)
    col = lax.broadcasted_iota(I32, (c, c), 1)
    return row >= col, row > col, (row == col).astype(F32)


def _chunk_tril(rows, c):
    shift = c.bit_length() - 1
    row = lax.broadcasted_iota(I32, (rows, rows), 0)
    col = lax.broadcasted_iota(I32, (rows, rows), 1)
    same = jnp.right_shift(row, shift) == jnp.right_shift(col, shift)
    return (same & (row >= col)).astype(BF16)


def _neumann_inv_many(ns, eye):
    c = eye.shape[0]
    ts = [eye + n for n in ns]
    ps = list(ns)
    k = 2
    while k < c:
        ps = [_dot3(p, p) for p in ps]
        ts = [t + _dot3(t, p) for t, p in zip(ts, ps)]
        k *= 2
    return ts


def _softplus(x):
    return jnp.maximum(x, 0.0) + jnp.log1p(jnp.exp(-jnp.abs(x)))


def _silu(x):
    return x * jax.nn.sigmoid(x)


def _rms(x, eps=EPS):
    return x * lax.rsqrt(jnp.mean(x * x, axis=-1, keepdims=True) + eps)


def _l2n(x):
    return x * lax.rsqrt(jnp.sum(x * x, axis=-1, keepdims=True) + 1e-6)


def _rows_to_tiles(ref, val):
    n = val.shape[0]
    for j in range(ROW_TILES):
        ref[pl.ds(j, n, stride=SUBLANES), :] = val[:, j * LANES:(j + 1) * LANES]


def _tiles_to_rows(ref, start, n):
    return jnp.concatenate([ref[pl.ds(start + j, n, stride=SUBLANES), :] for j in range(ROW_TILES)], axis=1)


def _in_proj_kernel(x_ref, g_ref, *refs, precise):
    if precise:
        wh_ref, wl_ref, pa_ref, pb_ref, pc_ref = refs
    else:
        wh_ref, pa_ref, pb_ref, pc_ref = refs
    xn = _rms(x_ref[...]) * g_ref[...]
    xh = xn.astype(BF16)
    if precise:
        xl = (xn - xh.astype(F32)).astype(BF16)
    lo = 0
    for out_ref, width in ((pa_ref, A_SEG), (pb_ref, B_WIDTH), (pc_ref, C_SEG)):
        acc = jnp.dot(xh, wh_ref[:, lo:lo + width], preferred_element_type=F32)
        if precise:
            acc = acc + (jnp.dot(xh, wl_ref[:, lo:lo + width], preferred_element_type=F32)
                         + jnp.dot(xl, wh_ref[:, lo:lo + width], preferred_element_type=F32))
        out_ref[...] = acc
        lo += width


def _split_kernel(w_ref, hi_ref, lo_ref):
    hi, lo = _split2(w_ref[...])
    hi_ref[...] = hi
    lo_ref[...] = lo


def _split_hi_lo(w):
    r, c = w.shape
    tc = _pick(c, (512, 256, 128))
    spec = pl.BlockSpec((r, tc), lambda j: (0, j))
    return pl.pallas_call(
        _split_kernel,
        grid=(c // tc,),
        in_specs=[spec],
        out_specs=[spec, spec],
        out_shape=[jax.ShapeDtypeStruct((r, c), BF16), jax.ShapeDtypeStruct((r, c), BF16)],
        compiler_params=_cparams(("arbitrary",)),
        name="split_hi_lo",
    )(w)


def _in_proj(x, g, w, precise):
    t = x.shape[0]
    tm = _pick(t, (256, 128, 64, 32, 16, 8) if precise else (512, 256, 128, 64, 32, 16, 8))
    wtot = A_SEG + B_WIDTH + C_SEG
    wspec = pl.BlockSpec((D_MODEL, wtot), lambda i: (0, 0))
    w_args = _split_hi_lo(w) if precise else [w.astype(BF16)]
    return pl.pallas_call(
        functools.partial(_in_proj_kernel, precise=precise),
        grid=(t // tm,),
        in_specs=[
            pl.BlockSpec((tm, D_MODEL), lambda i: (i, 0)),
            pl.BlockSpec((1, D_MODEL), lambda i: (0, 0)),
        ] + [wspec] * len(w_args),
        out_specs=[
            pl.BlockSpec((tm, A_SEG), lambda i: (i, 0)),
            pl.BlockSpec((tm, B_WIDTH), lambda i: (i, 0)),
            pl.BlockSpec((tm, C_SEG), lambda i: (i, 0)),
        ],
        out_shape=[
            jax.ShapeDtypeStruct((t, A_SEG), F32),
            jax.ShapeDtypeStruct((t, B_WIDTH), F32),
            jax.ShapeDtypeStruct((t, C_SEG), F32),
        ],
        compiler_params=_cparams(("arbitrary",)),
        name="in_proj",
    )(x, g.reshape(1, D_MODEL), *w_args)


def _mixer_a_kernel(pa_ref, cprev_ref, s0_ref, cw_ref, alog_ref, dtb_ref, nw_ref,
                    oa_ref, cnew_ref, snew_ref, tail_ref, s_ref, *, c, nch, n_steps, precise):
    dot = _mdot(precise)
    step = pl.program_id(1)
    rows = c * nch

    @pl.when(step == 0)
    def _():
        tail_ref[...] = jnp.zeros_like(tail_ref)
        tail_ref[SUBLANES - (A_CONV - 1):SUBLANES, :] = cprev_ref[...]
        s_ref[...] = s0_ref[...]

    x = pa_ref[...]
    qkv = x[:, :A_CONV_CH]
    z = x[:, A_CONV_CH:A_CONV_CH + A_V]
    ab = x[:, A_CONV_CH + A_V:A_SEG]

    xp = jnp.concatenate([tail_ref[...], qkv], axis=0)
    cw = cw_ref[...]
    y = qkv * cw[A_CONV - 1:A_CONV]
    for j in range(A_CONV - 1):
        o = SUBLANES - (A_CONV - 1) + j
        y = y + xp[o:o + rows] * cw[j:j + 1]
    tail_ref[...] = qkv[rows - SUBLANES:rows]

    @pl.when(step == n_steps - 1)
    def _():
        cnew_ref[...] = qkv[rows - (A_CONV - 1):rows]

    act = _silu(y)
    g_all = -jnp.exp(alog_ref[...]) * _softplus(ab + dtb_ref[...])
    beta_all = jax.nn.sigmoid(ab)
    gc_all = _cumsum_rows(g_all, _chunk_tril(rows, c))
    gc_t = gc_all.T
    tril, strict, eye = _tri_masks(c)
    nw = nw_ref[...]

    qn = [_l2n(act[:, h * A_DK:(h + 1) * A_DK]) * (A_DK ** -0.5) for h in range(A_HEADS)]
    kn = [_l2n(act[:, A_QK + h * A_DK:A_QK + (h + 1) * A_DK]) for h in range(A_HEADS)]
    vv = [act[:, 2 * A_QK + h * A_DV:2 * A_QK + (h + 1) * A_DV] for h in range(A_HEADS)]

    items = [(ci, h) for ci in range(nch) for h in range(A_HEADS)]
    q_i, k_i, kb_i, vb_i, dec_i, gcc_i, gl_i = [], [], [], [], [], [], []
    for ci, h in items:
        r0 = ci * c
        beta = beta_all[r0:r0 + c, A_HEADS + h:A_HEADS + h + 1]
        gcc = gc_all[r0:r0 + c, h:h + 1]
        gcr = gc_t[h:h + 1, r0:r0 + c]
        k = kn[h][r0:r0 + c]
        q_i.append(qn[h][r0:r0 + c])
        k_i.append(k)
        kb_i.append(k * beta)
        vb_i.append(vv[h][r0:r0 + c] * beta)
        dec_i.append(jnp.exp(jnp.where(tril, gcc - gcr, -1e30)))
        gcc_i.append(gcc)
        gl_i.append(gc_all[r0 + c - 1:r0 + c, h:h + 1])
    a_i = [jnp.where(strict, dot(kb, k, _NT) * dec, 0.0) for kb, k, dec in zip(kb_i, k_i, dec_i)]
    t_i = _neumann_inv_many([-a for a in a_i], eye)
    egc_i = [jnp.exp(g) for g in gcc_i]
    val_i = [_dot3(t, vb) for t, vb in zip(t_i, vb_i)]
    kcd_i = [_dot3(t, kb * e) for t, kb, e in zip(t_i, kb_i, egc_i)]
    attn_i = [jnp.where(tril, dot(q, k, _NT) * dec, 0.0) for q, k, dec in zip(q_i, k_i, dec_i)]
    qg_i = [q * e for q, e in zip(q_i, egc_i)]
    kg_i = [k * jnp.exp(gl - g) for k, gl, g in zip(k_i, gl_i, gcc_i)]

    s = [s_ref[h] for h in range(A_HEADS)]
    out_rows = []
    for ci in range(nch):
        ids = [ci * A_HEADS + h for h in range(A_HEADS)]
        u = [val_i[i] - dot(kcd_i[i], s[h]) for h, i in enumerate(ids)]
        o = [dot(qg_i[i], s[h]) + dot(attn_i[i], u[h]) for h, i in enumerate(ids)]
        s = [s[h] * jnp.exp(gl_i[i]) + dot(kg_i[i], u[h], _TN) for h, i in enumerate(ids)]
        r0 = ci * c
        outs = [_rms(o[h]) * nw * _silu(z[r0:r0 + c, h * A_DV:(h + 1) * A_DV]) for h in range(A_HEADS)]
        out_rows.append(jnp.concatenate(outs, axis=1))
    for h in range(A_HEADS):
        s_ref[h] = s[h]
    oa_ref[...] = jnp.concatenate(out_rows, axis=0).astype(oa_ref.dtype)

    @pl.when(step == n_steps - 1)
    def _():
        snew_ref[...] = s_ref[...]


def _row(v, width=LANES):
    v = v.reshape(1, -1).astype(F32)
    return jnp.pad(v, ((0, 0), (0, width - v.shape[1])))


def _mixer_geometry(row0, seq):
    c = min(CHUNK, seq)
    nch = _pick(seq // c, (MIXER_CHUNKS_PER_STEP, 2, 1))
    rows = c * nch
    assert row0 % rows == 0
    return c, nch, rows, seq // rows, row0 // rows


def _mixer_a(pa, row0, bsz, seq, precise, conv_prev, s0, conv_w, a_log, dt_bias, norm_w):
    c, nch, rows, n_steps, blk0 = _mixer_geometry(row0, seq)
    kern = functools.partial(_mixer_a_kernel, c=c, nch=nch, n_steps=n_steps, precise=precise)
    full = lambda shape: pl.BlockSpec(shape, lambda b, n: (0,) * len(shape))
    return pl.pallas_call(
        kern,
        grid=(bsz, n_steps),
        in_specs=[
            pl.BlockSpec((rows, A_SEG), lambda b, n: (blk0 + b * n_steps + n, 0)),
            pl.BlockSpec((None, A_CONV - 1, A_CONV_CH), lambda b, n: (b, 0, 0)),
            pl.BlockSpec((None, A_HEADS, A_DK, A_DV), lambda b, n: (b, 0, 0, 0)),
            full((A_CONV, A_CONV_CH)),
            full((1, LANES)),
            full((1, LANES)),
            full((1, A_DV)),
        ],
        out_specs=[
            pl.BlockSpec((rows, A_V), lambda b, n: (b * n_steps + n, 0)),
            pl.BlockSpec((None, A_CONV - 1, A_CONV_CH), lambda b, n: (b, 0, 0)),
            pl.BlockSpec((None, A_HEADS, A_DK, A_DV), lambda b, n: (b, 0, 0, 0)),
        ],
        out_shape=[
            jax.ShapeDtypeStruct((bsz * seq, A_V), F32 if precise else BF16),
            jax.ShapeDtypeStruct((bsz, A_CONV - 1, A_CONV_CH), F32),
            jax.ShapeDtypeStruct((bsz, A_HEADS, A_DK, A_DV), F32),
        ],
        scratch_shapes=[pltpu.VMEM((SUBLANES, A_CONV_CH), F32), pltpu.VMEM((A_HEADS, A_DK, A_DV), F32)],
        compiler_params=_cparams(("arbitrary", "arbitrary")),
        name="mixer_a",
    )(pa, conv_prev, s0, conv_w, _row(a_log), _row(dt_bias), norm_w.reshape(1, A_DV))


def _mixer_b_kernel(pb_ref, sprev_ref, s0_ref, mu_ref, w0_ref, w2_ref, a0_ref, a2_ref, g2_ref,
                    kk_ref, ka_ref, rk_ref, lnw_ref, lnb_ref,
                    ob_ref, shnew_ref, snew_ref, last_ref, s_ref, *, c, nch, n_steps, precise):
    dot = _mdot(precise)
    step = pl.program_id(1)
    rows = c * nch

    @pl.when(step == 0)
    def _():
        last_ref[...] = sprev_ref[...]
        s_ref[...] = s0_ref[...]

    p = pb_ref[...]
    rowi = lax.broadcasted_iota(I32, (rows, 1), 0)
    prev = jnp.where(rowi == 0, last_ref[...], pltpu.roll(p, 1, 0))
    last_ref[...] = p[rows - 1:rows]

    @pl.when(step == n_steps - 1)
    def _():
        shnew_ref[...] = p[rows - 1:rows]

    xs = p + (prev - p) * mu_ref[...]
    r_all = xs[:, 0:B_C]
    k_all = xs[:, B_C:2 * B_C]
    v_all = xs[:, 2 * B_C:3 * B_C]
    xwa = xs[:, 3 * B_C:3 * B_C + B_W_LORA + B_A_LORA]
    xg = xs[:, 3 * B_C + B_W_LORA + B_A_LORA:B_WIDTH]

    w_log = -_softplus(-(w0_ref[...] + dot(jnp.tanh(xwa), w2_ref[...]))) - 0.5
    lw = -jnp.exp(w_log)
    rate = jax.nn.sigmoid(a0_ref[...] + dot(xwa, a2_ref[...]))
    gate = dot(jax.nn.sigmoid(xg), g2_ref[...])
    kkr = k_all * kk_ref[...]
    k2_all = k_all * (1.0 + (rate - 1.0) * ka_ref[...])

    cl = _cumsum_rows(lw, _chunk_tril(rows, c))
    e_cl = jnp.exp(cl)
    e_neg = jnp.exp(-cl)
    at_all = jnp.exp(cl - lw)
    bt_all = rate * e_neg
    kt_all = k2_all * e_neg
    rt_all = r_all * e_cl
    tril, strict, eye = _tri_masks(c)
    rk = rk_ref[...]
    lnw = lnw_ref[...]
    lnb = lnb_ref[...]
    kkn = [_l2n(kkr[:, h * B_N:(h + 1) * B_N]) for h in range(B_HEADS)]

    items = [(ci, h) for ci in range(nch) for h in range(B_HEADS)]
    at_i, bt_i, kt_i, rt_i, v_i = [], [], [], [], []
    for ci, h in items:
        rs = slice(ci * c, (ci + 1) * c)
        ls = slice(h * B_N, (h + 1) * B_N)
        kk = kkn[h][rs]
        at_i.append(-kk * at_all[rs, ls])
        bt_i.append(kk * bt_all[rs, ls])
        kt_i.append(kt_all[rs, ls])
        rt_i.append(rt_all[rs, ls])
        v_i.append(v_all[rs, ls])
    l_ab = [jnp.where(strict, _dot3(a, b, _NT), 0.0) for a, b in zip(at_i, bt_i)]
    t_i = _neumann_inv_many(l_ab, eye)
    m_ak = [jnp.where(strict, dot(a, k, _NT), 0.0) for a, k in zip(at_i, kt_i)]
    l_rb = [jnp.where(tril, dot(r, b, _NT), 0.0) for r, b in zip(rt_i, bt_i)]
    l_rk = [jnp.where(tril, dot(r, k, _NT), 0.0) for r, k in zip(rt_i, kt_i)]
    mv_i = [dot(m, v) for m, v in zip(m_ak, v_i)]
    ov_i = [dot(l, v) for l, v in zip(l_rk, v_i)]

    s = [s_ref[h] for h in range(B_HEADS)]
    out_rows = []
    for ci in range(nch):
        ids = [ci * B_HEADS + h for h in range(B_HEADS)]
        rs = slice(ci * c, (ci + 1) * c)
        u = [_dot3(t_i[i], dot(at_i[i], s[h], _NT) + mv_i[i]) for h, i in enumerate(ids)]
        o = [dot(rt_i[i], s[h], _NT) + dot(l_rb[i], u[h]) + ov_i[i] for h, i in enumerate(ids)]
        e_last = e_cl[(ci + 1) * c - 1:(ci + 1) * c]
        s = [(s[h] + dot(u[h], bt_i[i], _TN) + dot(v_i[i], kt_i[i], _TN)) * e_last[:, h * B_N:(h + 1) * B_N]
             for h, i in enumerate(ids)]
        outs = []
        for h in range(B_HEADS):
            ls = slice(h * B_N, (h + 1) * B_N)
            mean = jnp.mean(o[h], axis=-1, keepdims=True)
            var = jnp.mean(jnp.square(o[h] - mean), axis=-1, keepdims=True)
            gn = (o[h] - mean) * lax.rsqrt(var + B_GN_EPS) * lnw[:, ls] + lnb[:, ls]
            bonus = jnp.sum(r_all[rs, ls] * k2_all[rs, ls] * rk[:, ls], axis=-1, keepdims=True) * v_all[rs, ls]
            outs.append((gn + bonus) * gate[rs, ls])
        out_rows.append(jnp.concatenate(outs, axis=1))
    for h in range(B_HEADS):
        s_ref[h] = s[h]
    ob_ref[...] = jnp.concatenate(out_rows, axis=0).astype(ob_ref.dtype)

    @pl.when(step == n_steps - 1)
    def _():
        snew_ref[...] = s_ref[...]


def _mixer_b(pb, row0, bsz, seq, precise, shift_prev, s0, mu, w0, w2, a0, a2, g2, k_k, k_a, r_k, ln_w, ln_b):
    c, nch, rows, n_steps, blk0 = _mixer_geometry(row0, seq)
    kern = functools.partial(_mixer_b_kernel, c=c, nch=nch, n_steps=n_steps, precise=precise)
    full = lambda shape: pl.BlockSpec(shape, lambda b, n: (0,) * len(shape))
    lora = B_W_LORA + B_A_LORA
    w2p = jnp.zeros((lora, B_C), F32).at[:B_W_LORA].set(w2)
    a2p = jnp.zeros((lora, B_C), F32).at[B_W_LORA:].set(a2)
    vec = lambda v: v.reshape(1, -1).astype(F32)
    return pl.pallas_call(
        kern,
        grid=(bsz, n_steps),
        in_specs=[
            pl.BlockSpec((rows, B_WIDTH), lambda b, n: (blk0 + b * n_steps + n, 0)),
            pl.BlockSpec((None, 1, B_WIDTH), lambda b, n: (b, 0, 0)),
            pl.BlockSpec((None, B_HEADS, B_N, B_N), lambda b, n: (b, 0, 0, 0)),
            full((1, B_WIDTH)),
            full((1, B_C)), full((lora, B_C)),
            full((1, B_C)), full((lora, B_C)),
            full((B_G_LORA, B_C)),
            full((1, B_C)), full((1, B_C)), full((1, B_C)), full((1, B_C)), full((1, B_C)),
        ],
        out_specs=[
            pl.BlockSpec((rows, B_C), lambda b, n: (b * n_steps + n, 0)),
            pl.BlockSpec((None, 1, B_WIDTH), lambda b, n: (b, 0, 0)),
            pl.BlockSpec((None, B_HEADS, B_N, B_N), lambda b, n: (b, 0, 0, 0)),
        ],
        out_shape=[
            jax.ShapeDtypeStruct((bsz * seq, B_C), F32 if precise else BF16),
            jax.ShapeDtypeStruct((bsz, 1, B_WIDTH), F32),
            jax.ShapeDtypeStruct((bsz, B_HEADS, B_N, B_N), F32),
        ],
        scratch_shapes=[pltpu.VMEM((1, B_WIDTH), F32), pltpu.VMEM((B_HEADS, B_N, B_N), F32)],
        compiler_params=_cparams(("arbitrary", "arbitrary")),
        name="mixer_b",
    )(pb, shift_prev, s0, vec(mu), vec(w0), w2p, vec(a0), a2p, g2,
      vec(k_k), vec(k_a), vec(r_k), vec(ln_w), vec(ln_b))


def _kv_expand(ckvn, krope, wuk_ref, wuv_ref, gk_ref, k_ref, v_ref, dot):
    k0 = dot(ckvn, wuk_ref[...])
    gk = gk_ref[...]
    for h in range(C_HEADS):
        kh = k0[:, h * C_HPAD:(h + 1) * C_HPAD] + krope
        ss = jnp.sum(kh * kh, axis=-1, keepdims=True) * (1.0 / C_QK)
        k_ref[:, h * C_HPAD:(h + 1) * C_HPAD] = (kh * lax.rsqrt(ss + EPS) * gk).astype(k_ref.dtype)
    v_ref[...] = dot(ckvn, wuv_ref[...]).astype(v_ref.dtype)


def _c_prep_kernel(pc_ref, cos_ref, sin_ref, qn_ref, kvn_ref, wq_ref, wqr_ref, wuk_ref, wuv_ref, gq_ref, gk_ref,
                   q_ref, k_ref, v_ref, ckv_ref, kr_ref, *, precise):
    dot = _mdot(precise)
    pc = pc_ref[...]
    cq = pc[:, 0:C_Q_LORA]
    ckv_raw = pc[:, C_Q_LORA:C_Q_LORA + C_KV_LORA]
    krp = pc[:, C_Q_LORA + C_KV_LORA:C_Q_LORA + C_KV_LORA + LANES]
    krr = pc[:, C_Q_LORA + C_KV_LORA + LANES:C_SEG]
    cos = cos_ref[...]
    sin = sin_ref[...]
    lane = lax.broadcasted_iota(I32, cos.shape, 1)
    cosq = jnp.where(lane < C_NOPE, 1.0, cos)

    cqn = _rms(cq) * qn_ref[...]
    q0 = dot(cqn, wq_ref[...])
    q1 = dot(cqn, wqr_ref[...])
    gq = gq_ref[...] * (C_QK ** -0.5)
    for h in range(C_HEADS):
        sl = slice(h * C_HPAD, (h + 1) * C_HPAD)
        qh = q0[:, sl] * cosq + q1[:, sl] * sin
        ss = jnp.sum(qh * qh, axis=-1, keepdims=True) * (1.0 / C_QK)
        q_ref[:, sl] = (qh * lax.rsqrt(ss + EPS) * gq).astype(q_ref.dtype)

    ckvn = _rms(ckv_raw) * kvn_ref[...]
    ckv_ref[...] = ckvn
    krope = krp * cos + krr * sin
    kr_ref[...] = krope[:, C_NOPE:C_NOPE + C_ROPE]
    _kv_expand(ckvn, krope, wuk_ref, wuv_ref, gk_ref, k_ref, v_ref, dot)


def _kv_expand_kernel(ckv_ref, kr_ref, wuk_ref, wuv_ref, gk_ref, k_ref, v_ref, *, precise):
    _kv_expand(ckv_ref[...], kr_ref[...], wuk_ref, wuv_ref, gk_ref, k_ref, v_ref, _mdot(precise))


def _c_weights(q_norm, kv_norm, w_uq, w_ukv, q_gain, k_gain):
    half = C_ROPE // 2
    wq = jnp.zeros((C_Q_LORA, C_HEADS * C_HPAD), F32)
    wqr = jnp.zeros((C_Q_LORA, C_HEADS * C_HPAD), F32)
    wuk = jnp.zeros((C_KV_LORA, C_HEADS * C_HPAD), F32)
    wuv = jnp.zeros((C_KV_LORA, C_HEADS * C_VDIM), F32)
    for h in range(C_HEADS):
        wh = w_uq[:, h * C_QK:(h + 1) * C_QK]
        wq = wq.at[:, h * C_HPAD:h * C_HPAD + C_QK].set(wh)
        rot = jnp.concatenate([-wh[:, C_NOPE + half:], wh[:, C_NOPE:C_NOPE + half]], axis=1)
        wqr = wqr.at[:, h * C_HPAD + C_NOPE:h * C_HPAD + C_QK].set(rot)
        kvh = w_ukv[:, h * (C_NOPE + C_VDIM):(h + 1) * (C_NOPE + C_VDIM)]
        wuk = wuk.at[:, h * C_HPAD:h * C_HPAD + C_NOPE].set(kvh[:, :C_NOPE])
        wuv = wuv.at[:, h * C_VDIM:(h + 1) * C_VDIM].set(kvh[:, C_NOPE:])
    gain = lambda g: _row(jnp.concatenate([g[:C_NOPE], g[C_NOPE:], g[C_NOPE:]]))
    return (q_norm.reshape(1, -1), kv_norm.reshape(1, -1), wq, wqr, wuk, wuv, gain(q_gain), gain(k_gain))


def _rope_tables(pos):
    half = C_ROPE // 2
    inv = ROPE_THETA ** (-(jnp.arange(0, C_ROPE, 2, dtype=F32) / C_ROPE))
    ang = pos.astype(F32)[:, None] * inv[None, :]
    cos, sin = jnp.cos(ang), jnp.sin(ang)
    pad = lambda t: jnp.pad(jnp.concatenate([t, t], axis=1), ((0, 0), (C_NOPE, LANES - C_NOPE - 2 * half)))
    return pad(cos), pad(sin)


def _c_prep(pc, cos, sin, cw, precise):
    t = pc.shape[0]
    tm = _pick(t, (512, 256, 128, 64, 32, 16, 8))
    qn, kvn, wq, wqr, wuk, wuv, gq, gk = cw
    full = lambda a: pl.BlockSpec(a.shape, lambda i: (0,) * a.ndim)
    rows = lambda w: pl.BlockSpec((tm, w), lambda i: (i, 0))
    act = F32 if precise else BF16
    return pl.pallas_call(
        functools.partial(_c_prep_kernel, precise=precise),
        grid=(t // tm,),
        in_specs=[rows(C_SEG), rows(LANES), rows(LANES)] + [full(a) for a in (qn, kvn, wq, wqr, wuk, wuv, gq, gk)],
        out_specs=[rows(C_HEADS * C_HPAD), rows(C_HEADS * C_HPAD), rows(C_HEADS * C_VDIM), rows(C_KV_LORA),
                   rows(C_ROPE)],
        out_shape=[
            jax.ShapeDtypeStruct((t, C_HEADS * C_HPAD), act),
            jax.ShapeDtypeStruct((t, C_HEADS * C_HPAD), act),
            jax.ShapeDtypeStruct((t, C_HEADS * C_VDIM), act),
            jax.ShapeDtypeStruct((t, C_KV_LORA), F32),
            jax.ShapeDtypeStruct((t, C_ROPE), F32),
        ],
        compiler_params=_cparams(("arbitrary",)),
        name="c_prep",
    )(pc, cos, sin, qn, kvn, wq, wqr, wuk, wuv, gq, gk)


def _kv_expand_call(ckv, krp, cw, precise):
    t = ckv.shape[0]
    tm = _pick(t, (1024, 512, 256, 128, 64, 32, 16, 8))
    _, _, _, _, wuk, wuv, _, gk = cw
    full = lambda a: pl.BlockSpec(a.shape, lambda i: (0,) * a.ndim)
    rows = lambda w: pl.BlockSpec((tm, w), lambda i: (i, 0))
    act = F32 if precise else BF16
    return pl.pallas_call(
        functools.partial(_kv_expand_kernel, precise=precise),
        grid=(t // tm,),
        in_specs=[rows(C_KV_LORA), rows(LANES), full(wuk), full(wuv), full(gk)],
        out_specs=[rows(C_HEADS * C_HPAD), rows(C_HEADS * C_VDIM)],
        out_shape=[
            jax.ShapeDtypeStruct((t, C_HEADS * C_HPAD), act),
            jax.ShapeDtypeStruct((t, C_HEADS * C_VDIM), act),
        ],
        compiler_params=_cparams(("arbitrary",)),
        name="kv_expand",
    )(ckv, krp, wuk, wuv, gk)


def _attn_kernel(*refs, tq, tk, past, has_cache, precise):
    if has_cache:
        q_ref, kn_ref, vn_ref, kc_ref, vc_ref, o_ref = refs
    else:
        q_ref, kn_ref, vn_ref, o_ref = refs
    dot = _mdot(precise)
    qi = pl.program_id(1)
    q = q_ref[...]
    shift = CHUNK.bit_length() - 1
    q_chunk = jnp.right_shift(past + qi * tq + lax.broadcasted_iota(I32, (tq, 1), 0), shift)
    qh = [q[:, h * C_HPAD:(h + 1) * C_HPAD] for h in range(C_HEADS)]

    def block(carry, kb, vb, k_pos0, width):
        vis = jnp.right_shift(k_pos0 + lax.broadcasted_iota(I32, (1, width), 1), shift) <= q_chunk
        new = []
        for h in range(C_HEADS):
            m, den, acc = carry[h]
            s = jnp.where(vis, dot(qh[h], kb[:, h * C_HPAD:(h + 1) * C_HPAD], _NT), -1e30)
            m_new = jnp.maximum(m, jnp.max(s, axis=-1, keepdims=True))
            scale = jnp.exp(m - m_new)
            p = jnp.exp(s - m_new)
            den = den * scale + jnp.sum(p, axis=-1, keepdims=True)
            acc = acc * scale + dot(p, vb[:, h * C_VDIM:(h + 1) * C_VDIM])
            new.append((m_new, den, acc))
        return tuple(new)

    carry = tuple((jnp.full((tq, 1), -1e30, F32), jnp.zeros((tq, 1), F32), jnp.zeros((tq, C_VDIM), F32))
                  for _ in range(C_HEADS))
    if has_cache:
        tc = _pick(past, (512, 256, 128, 64, 32, 16, 8))

        def cache_step(j, c):
            r0 = pl.multiple_of(j * tc, tc)
            return block(c, kc_ref[pl.ds(r0, tc), :], vc_ref[pl.ds(r0, tc), :], r0, tc)

        carry = lax.fori_loop(0, past // tc, cache_step, carry)

    def new_step(j, c):
        r0 = pl.multiple_of(j * tk, tk)
        return block(c, kn_ref[pl.ds(r0, tk), :], vn_ref[pl.ds(r0, tk), :], past + r0, tk)

    carry = lax.fori_loop(0, (qi + 1) * (tq // tk), new_step, carry)
    o_ref[...] = jnp.concatenate([acc / den for _, den, acc in carry], axis=1).astype(o_ref.dtype)


def _attention(q, k, v, row0, bsz, seq, precise, k_cache=None, v_cache=None):
    has_cache = k_cache is not None
    past = k_cache.shape[0] // bsz if has_cache else 0
    tq = _pick(seq, (256, 128, 64, 32, 16, 8))
    assert tq % CHUNK == 0 or tq == seq
    nq = seq // tq
    kern = functools.partial(_attn_kernel, tq=tq, tk=tq, past=past, has_cache=has_cache, precise=precise)
    qw, vw = C_HEADS * C_HPAD, C_HEADS * C_VDIM
    in_specs = [
        pl.BlockSpec((tq, qw), lambda b, i: (row0 // tq + b * nq + i, 0)),
        pl.BlockSpec((seq, qw), lambda b, i: (row0 // seq + b, 0)),
        pl.BlockSpec((seq, vw), lambda b, i: (row0 // seq + b, 0)),
    ]
    args = [q, k, v]
    if has_cache:
        in_specs += [pl.BlockSpec((past, qw), lambda b, i: (b, 0)), pl.BlockSpec((past, vw), lambda b, i: (b, 0))]
        args += [k_cache, v_cache]
    return pl.pallas_call(
        kern,
        grid=(bsz, nq),
        in_specs=in_specs,
        out_specs=pl.BlockSpec((tq, vw), lambda b, i: (b * nq + i, 0)),
        out_shape=jax.ShapeDtypeStruct((bsz * seq, vw), F32 if precise else BF16),
        compiler_params=_cparams(("arbitrary", "arbitrary")),
        name="attention",
    )(*args)


def _out_proj_kernel(x_ref, oap_ref, obp_ref, ocp_ref, oas_ref, obs_ref, ocs_ref, w_ref, g_ref, wr_ref, br_ref,
                     y_ref, xt_ref, ri_ref, rw_ref, *, n_prompt_tiles, precise):
    dot = _mdot(precise)
    is_p = pl.program_id(0) < n_prompt_tiles
    oa = jnp.where(is_p, oap_ref[...], oas_ref[...])
    ob = jnp.where(is_p, obp_ref[...], obs_ref[...])
    oc = jnp.where(is_p, ocp_ref[...], ocs_ref[...])
    y = x_ref[...]
    y = y + dot(oa, w_ref[0:A_V])
    y = y + dot(ob, w_ref[A_V:A_V + B_C])
    y = y + dot(oc, w_ref[A_V + B_C:D_MIX])
    y_ref[...] = y
    xn = _rms(y) * g_ref[...]
    _rows_to_tiles(xt_ref, xn)

    logits = _dot3(xn, wr_ref[...]) + br_ref[...]
    lane = lax.broadcasted_iota(I32, logits.shape, 1)
    lanef = lane.astype(F32)
    neg = -1e30
    lg = jnp.where(lane < N_GROUPS, logits, neg)
    mg = jnp.max(lg, axis=-1, keepdims=True)
    pg = 1.0 / jnp.sum(jnp.exp(lg - mg), axis=-1, keepdims=True)
    gidx = jnp.min(jnp.where(lg == mg, lanef, float(LANES)), axis=-1, keepdims=True)
    lo = N_GROUPS + EXPERTS_PER_GROUP * gidx
    in_grp = (lanef >= lo) & (lanef < lo + EXPERTS_PER_GROUP)
    el = jnp.where(in_grp, logits, neg)
    m1 = jnp.max(el, axis=-1, keepdims=True)
    i1 = jnp.min(jnp.where(el == m1, lanef, float(LANES)), axis=-1, keepdims=True)
    el2 = jnp.where(lanef == i1, neg, el)
    m2 = jnp.max(el2, axis=-1, keepdims=True)
    i2 = jnp.min(jnp.where(el2 == m2, lanef, float(LANES)), axis=-1, keepdims=True)
    den = jnp.sum(jnp.exp(el - m1), axis=-1, keepdims=True)
    p1 = 1.0 / den
    p2 = jnp.exp(m2 - m1) / den
    w1 = pg * p1 / (p1 + p2)
    w2 = pg * p2 / (p1 + p2)
    e1 = (i1 - N_GROUPS).astype(I32)
    e2 = (i2 - N_GROUPS).astype(I32)
    ri_ref[...] = jnp.where(lane == 0, e1, jnp.where(lane == 1, e2, 0))
    rw_ref[...] = jnp.where(lane == 0, w1, jnp.where(lane == 1, w2, 0.0))


def _out_proj(x, mix_p, mix_s, w_out, g, w_router, b_router, precise):
    t = x.shape[0]
    tp, ts = mix_p[0].shape[0], mix_s[0].shape[0]
    tm = _pick(ts, (512, 256, 128, 64, 32, 16, 8))
    assert tp % tm == 0 and tp + ts == t
    npt = tp // tm
    kern = functools.partial(_out_proj_kernel, n_prompt_tiles=npt, precise=precise)
    pspec = lambda w: pl.BlockSpec((tm, w), lambda i: (jnp.minimum(i, npt - 1), 0))
    sspec = lambda w: pl.BlockSpec((tm, w), lambda i: (jnp.maximum(i - npt, 0), 0))
    full = lambda a: pl.BlockSpec(a.shape, lambda i: (0,) * a.ndim)
    rows = lambda w: pl.BlockSpec((tm, w), lambda i: (i, 0))
    cw = C_HEADS * C_VDIM
    g2 = g.reshape(1, D_MODEL)
    return pl.pallas_call(
        kern,
        grid=(t // tm,),
        in_specs=[rows(D_MODEL), pspec(A_V), pspec(B_C), pspec(cw), sspec(A_V), sspec(B_C), sspec(cw),
                  full(w_out), full(g2), full(w_router), full(b_router)],
        out_specs=[rows(D_MODEL), pl.BlockSpec((tm * SUBLANES, LANES), lambda i: (i, 0)), rows(LANES), rows(LANES)],
        out_shape=[
            jax.ShapeDtypeStruct((t, D_MODEL), F32),
            jax.ShapeDtypeStruct((t * SUBLANES, LANES), F32),
            jax.ShapeDtypeStruct((t, LANES), I32),
            jax.ShapeDtypeStruct((t, LANES), F32),
        ],
        compiler_params=_cparams(("arbitrary",)),
        name="out_proj_router",
    )(x, *mix_p, *mix_s, w_out, g2, w_router, b_router)


def _rank_kernel(ri_ref, rank_ref, cnt_ref, base_ref):
    @pl.when(pl.program_id(0) == 0)
    def _():
        base_ref[...] = jnp.zeros_like(base_ref)

    ri = ri_ref[...]
    tm = ri.shape[0]
    lane = lax.broadcasted_iota(I32, ri.shape, 1)
    oh0 = lane == ri[:, 0:1]
    oh1 = lane == ri[:, 1:2]
    cnt = oh0.astype(F32) + oh1.astype(F32)
    row = lax.broadcasted_iota(I32, (tm, tm), 0)
    col = lax.broadcasted_iota(I32, (tm, tm), 1)
    before = jnp.dot((row > col).astype(BF16), cnt.astype(BF16), preferred_element_type=F32) + base_ref[...]
    r0 = jnp.sum(jnp.where(oh0, before, 0.0), axis=-1, keepdims=True)
    r1 = jnp.sum(jnp.where(oh1, before, 0.0), axis=-1, keepdims=True)
    rank_ref[...] = jnp.where(lane == 0, r0, jnp.where(lane == 1, r1, 0.0)).astype(I32)
    total = base_ref[...] + jnp.sum(cnt, axis=0, keepdims=True)
    base_ref[...] = total
    cnt_ref[...] = total.astype(I32)


def _rank(ri):
    t = ri.shape[0]
    tm = _pick(t, (256, 128, 64, 32, 16, 8))
    return pl.pallas_call(
        _rank_kernel,
        grid=(t // tm,),
        in_specs=[pl.BlockSpec((tm, LANES), lambda i: (i, 0))],
        out_specs=[pl.BlockSpec((tm, LANES), lambda i: (i, 0)), pl.BlockSpec((1, LANES), lambda i: (0, 0))],
        out_shape=[jax.ShapeDtypeStruct((t, LANES), I32), jax.ShapeDtypeStruct((1, LANES), I32)],
        scratch_shapes=[pltpu.VMEM((1, LANES), F32)],
        compiler_params=_cparams(("arbitrary",)),
        name="moe_rank",
    )(ri)


def _gather_tiles(idx_ref, n, src_hbm, buf, slot, sem):
    def body(r, carry):
        src = pl.multiple_of(idx_ref[0, r] * SUBLANES, SUBLANES)
        dst = pl.multiple_of((slot * n + r) * SUBLANES, SUBLANES)
        pltpu.make_async_copy(src_hbm.at[pl.ds(src, SUBLANES)], buf.at[pl.ds(dst, SUBLANES)], sem.at[slot]).start()
        return carry

    lax.fori_loop(0, n, body, 0, unroll=8)


def _gather_wait(n, src_hbm, buf, slot, sem):
    base = pl.multiple_of(slot * n * SUBLANES, SUBLANES)
    pltpu.make_async_copy(src_hbm.at[pl.ds(0, n * SUBLANES)], buf.at[pl.ds(base, n * SUBLANES)], sem.at[slot]).wait()
    return base


def _expert_kernel(te_ref, nu_ref, idc_ref, idn_ref, xt_hbm, wg_ref, wu_ref, wd_ref, o_ref, xbuf, sem):
    i = pl.program_id(0)
    n_used = nu_ref[0]
    slot = lax.rem(i, 2)

    @pl.when((i == 0) & (n_used > 0))
    def _():
        _gather_tiles(idc_ref, MOE_TILE, xt_hbm, xbuf, 0, sem)

    @pl.when(i + 1 < n_used)
    def _():
        _gather_tiles(idn_ref, MOE_TILE, xt_hbm, xbuf, 1 - slot, sem)

    @pl.when(i < n_used)
    def _():
        base = _gather_wait(MOE_TILE, xt_hbm, xbuf, slot, sem)
        x = _tiles_to_rows(xbuf, base, MOE_TILE).astype(BF16)
        hg = jnp.dot(x, wg_ref[...].astype(BF16), preferred_element_type=F32)
        hu = jnp.dot(x, wu_ref[...].astype(BF16), preferred_element_type=F32)
        hidden = (_silu(hg) * hu).astype(BF16)
        _rows_to_tiles(o_ref, jnp.dot(hidden, wd_ref[...].astype(BF16), preferred_element_type=F32))

    @pl.when(i >= n_used)
    def _():
        o_ref[...] = jnp.zeros_like(o_ref)


def _expert_ffn(xt, token_of_pos, tile_expert, n_used, e_gate, e_up, e_down):
    nt = token_of_pos.shape[0] // MOE_TILE
    idx = token_of_pos.reshape(nt, 1, MOE_TILE)
    tile_rows = MOE_TILE * SUBLANES
    smem = lambda imap: pl.BlockSpec((None, 1, MOE_TILE), imap, memory_space=pltpu.SMEM)
    grid_spec = pltpu.PrefetchScalarGridSpec(
        num_scalar_prefetch=2,
        grid=(nt,),
        in_specs=[
            smem(lambda i, te, nu: (i, 0, 0)),
            smem(lambda i, te, nu: (jnp.minimum(i + 1, nt - 1), 0, 0)),
            pl.BlockSpec(memory_space=pl.ANY),
            pl.BlockSpec((None, D_MODEL, D_EXPERT), lambda i, te, nu: (te[i], 0, 0)),
            pl.BlockSpec((None, D_MODEL, D_EXPERT), lambda i, te, nu: (te[i], 0, 0)),
            pl.BlockSpec((None, D_EXPERT, D_MODEL), lambda i, te, nu: (te[i], 0, 0)),
        ],
        out_specs=pl.BlockSpec((tile_rows, LANES), lambda i, te, nu: (i, 0)),
        scratch_shapes=[pltpu.VMEM((2 * tile_rows, LANES), F32), pltpu.SemaphoreType.DMA((2,))],
    )
    return pl.pallas_call(
        _expert_kernel,
        grid_spec=grid_spec,
        out_shape=jax.ShapeDtypeStruct((nt * tile_rows, LANES), F32),
        compiler_params=_cparams(("arbitrary",)),
        name="expert_ffn",
    )(tile_expert, n_used, idx, idx, xt, e_gate, e_up, e_down)


def _combine_kernel(pc_ref, pn_ref, x_ref, rw_ref, ot_hbm, y_ref, obuf, sem, *, tm):
    i = pl.program_id(0)
    nt = pl.num_programs(0)
    slot = lax.rem(i, 2)
    n = TOP_K * tm

    @pl.when(i == 0)
    def _():
        _gather_tiles(pc_ref, n, ot_hbm, obuf, 0, sem)

    @pl.when(i + 1 < nt)
    def _():
        _gather_tiles(pn_ref, n, ot_hbm, obuf, 1 - slot, sem)

    base = _gather_wait(n, ot_hbm, obuf, slot, sem)
    rw = rw_ref[...]
    o0 = _tiles_to_rows(obuf, base, tm)
    o1 = _tiles_to_rows(obuf, base + tm * SUBLANES, tm)
    y_ref[...] = x_ref[...] + rw[:, 0:1] * o0 + rw[:, 1:2] * o1


def _combine(x, out_tiles, pos, rw):
    t = x.shape[0]
    tm = _pick(t, (256, 128, 64, 32, 16, 8))
    nt = t // tm
    n = TOP_K * tm
    idx = pos.reshape(nt, tm, TOP_K).transpose(0, 2, 1).reshape(nt, 1, n)
    kern = functools.partial(_combine_kernel, tm=tm)
    rows = lambda w: pl.BlockSpec((tm, w), lambda i: (i, 0))
    smem = lambda imap: pl.BlockSpec((None, 1, n), imap, memory_space=pltpu.SMEM)
    return pl.pallas_call(
        kern,
        grid=(nt,),
        in_specs=[smem(lambda i: (i, 0, 0)), smem(lambda i: (jnp.minimum(i + 1, nt - 1), 0, 0)),
                  rows(D_MODEL), rows(LANES), pl.BlockSpec(memory_space=pl.ANY)],
        out_specs=rows(D_MODEL),
        out_shape=jax.ShapeDtypeStruct((t, D_MODEL), F32),
        scratch_shapes=[pltpu.VMEM((2 * n * SUBLANES, LANES), F32), pltpu.SemaphoreType.DMA((2,))],
        compiler_params=_cparams(("arbitrary",)),
        name="moe_combine",
    )(idx, idx, x, rw, out_tiles)


def _route_tables(eid, rank, counts):
    t = eid.shape[0]
    padded = ((counts + MOE_TILE - 1) // MOE_TILE) * MOE_TILE
    ends = jnp.cumsum(padded)
    starts = ends - padded
    experts = jnp.arange(N_EXPERTS, dtype=I32)
    pos = jnp.sum(jnp.where(eid[:, :, None] == experts, starts, 0), axis=-1) + rank
    np_rows = -(-(TOP_K * t + N_EXPERTS * MOE_TILE) // MOE_TILE) * MOE_TILE
    token_of_pos = jnp.zeros((np_rows,), I32).at[pos.reshape(-1)].set(jnp.arange(TOP_K * t, dtype=I32) // TOP_K)
    tile_start = jnp.arange(np_rows // MOE_TILE, dtype=I32) * MOE_TILE
    tile_expert = jnp.minimum(jnp.sum(tile_start[:, None] >= ends[None, :], axis=1), N_EXPERTS - 1).astype(I32)
    n_used = (ends[-1] // MOE_TILE).astype(I32).reshape(1)
    return pos.astype(I32), token_of_pos, tile_expert, n_used


def _moe(y, xt, ri, rw, e_gate, e_up, e_down):
    rank, counts = _rank(ri)
    pos, token_of_pos, tile_expert, n_used = _route_tables(ri[:, :TOP_K], rank[:, :TOP_K], counts[0, :N_EXPERTS])
    out_tiles = _expert_ffn(xt, token_of_pos, tile_expert, n_used, e_gate, e_up, e_down)
    return _combine(y, out_tiles, pos, rw)


def _in_weights(w_in):
    half = C_ROPE // 2
    zeros = lambda n: jnp.zeros((D_MODEL, n), F32)
    c0 = A_WIDTH + B_WIDTH
    kr = w_in[:, c0 + C_Q_LORA + C_KV_LORA:c0 + C_WIDTH]
    kr_rot = jnp.concatenate([-kr[:, half:], kr[:, :half]], axis=1)
    tail = LANES - C_NOPE - C_ROPE
    cols = [
        w_in[:, :A_CONV_CH + A_V], w_in[:, A_CONV_CH + A_V:A_WIDTH], zeros(LANES - 2 * A_HEADS),
        w_in[:, A_WIDTH:c0],
        w_in[:, c0:c0 + C_Q_LORA + C_KV_LORA],
        zeros(C_NOPE), kr, zeros(tail),
        zeros(C_NOPE), kr_rot, zeros(tail),
    ]
    return jnp.concatenate(cols, axis=1)


def _router_weights(rg, rgb, re, reb):
    w = jnp.zeros((D_MODEL, LANES), F32).at[:, :N_GROUPS].set(rg).at[:, N_GROUPS:N_GROUPS + N_EXPERTS].set(re)
    b = jnp.zeros((1, LANES), F32).at[0, :N_GROUPS].set(rgb).at[0, N_GROUPS:N_GROUPS + N_EXPERTS].set(reb)
    return w, b


def _layer(x, geom, st_p, st_s, cos, sin, wts, precise):
    (norm_mix, w_in, a_conv_w, a_A_log, a_dt_bias, a_norm_w,
     b_mu, b_w0, b_w2, b_a0, b_a2, b_g2, b_k_k, b_k_a, b_r_k, b_ln_w, b_ln_b,
     c_q_norm, c_kv_norm, c_w_uq, c_w_ukv, c_q_gain, c_k_gain,
     w_out, norm_ffn, router_group, router_group_bias, router_expert, router_expert_bias,
     e_gate, e_up, e_down) = wts
    bp, lp, bs, ls, past = geom
    tp = bp * lp

    pa, pb, pc = _in_proj(x, norm_mix, _in_weights(w_in), precise)
    cw = _c_weights(c_q_norm, c_kv_norm, c_w_uq, c_w_ukv, c_q_gain, c_k_gain)
    q, k, v, ckv, krope = _c_prep(pc, cos, sin, cw, precise)

    a_args = (a_conv_w, a_A_log, a_dt_bias, a_norm_w)
    b_args = (b_mu, b_w0, b_w2, b_a0, b_a2, b_g2, b_k_k, b_k_a, b_r_k.reshape(-1), b_ln_w, b_ln_b)
    conv_p, delta_p, shift_p, wkv_p = st_p
    conv_s, delta_s, shift_s, wkv_s, ckv_past, krope_past = st_s

    oa_p, conv_np, delta_np = _mixer_a(pa, 0, bp, lp, precise, conv_p, delta_p, *a_args)
    oa_s, conv_ns, delta_ns = _mixer_a(pa, tp, bs, ls, precise, conv_s, delta_s, *a_args)
    ob_p, shift_np, wkv_np = _mixer_b(pb, 0, bp, lp, precise, shift_p, wkv_p, *b_args)
    ob_s, shift_ns, wkv_ns = _mixer_b(pb, tp, bs, ls, precise, shift_s, wkv_s, *b_args)

    oc_p = _attention(q, k, v, 0, bp, lp, precise)
    krp_past = jnp.pad(krope_past.reshape(bs * past, C_ROPE), ((0, 0), (C_NOPE, LANES - C_NOPE - C_ROPE)))
    k_cache, v_cache = _kv_expand_call(ckv_past.reshape(bs * past, C_KV_LORA), krp_past, cw, precise)
    oc_s = _attention(q, k, v, tp, bs, ls, precise, k_cache, v_cache)

    w_router, b_router = _router_weights(router_group, router_group_bias, router_expert, router_expert_bias)
    y, xt, ri, rw = _out_proj(x, (oa_p, ob_p, oc_p), (oa_s, ob_s, oc_s), w_out, norm_ffn,
                              w_router, b_router, precise)
    x_new = _moe(y, xt, ri, rw, e_gate, e_up, e_down)

    new_p = (conv_np, delta_np, shift_np, wkv_np, ckv[:tp].reshape(bp, lp, C_KV_LORA),
             krope[:tp].reshape(bp, lp, C_ROPE))
    new_s = (conv_ns, delta_ns, shift_ns, wkv_ns, ckv[tp:].reshape(bs, ls, C_KV_LORA),
             krope[tp:].reshape(bs, ls, C_ROPE))
    return x_new, new_p, new_s


def _forward(x_prompt, x_sample, cache_c_kv, cache_k_rope, state_conv_a, state_delta_a, state_shift_b,
             state_wkv_b, weights):
    bp, lp, _ = x_prompt.shape
    bs, ls, _ = x_sample.shape
    depth = cache_c_kv.shape[0]
    past = cache_c_kv.shape[2]
    geom = (bp, lp, bs, ls, past)
    x = jnp.concatenate([x_prompt.reshape(bp * lp, D_MODEL), x_sample.reshape(bs * ls, D_MODEL)], axis=0)
    pos = jnp.concatenate([jnp.tile(jnp.arange(lp), bp), jnp.tile(past + jnp.arange(ls), bs)])
    cos, sin = _rope_tables(pos)
    zeros = lambda *s: jnp.zeros(s, F32)
    st_p = (zeros(bp, A_CONV - 1, A_CONV_CH), zeros(bp, A_HEADS, A_DK, A_DV), zeros(bp, 1, B_WIDTH),
            zeros(bp, B_HEADS, B_N, B_N))
    news_p, news_s = [], []
    for l in range(depth):
        st_s = (state_conv_a[l], state_delta_a[l], state_shift_b[l], state_wkv_b[l], cache_c_kv[l], cache_k_rope[l])
        x, new_p, new_s = _layer(x, geom, st_p, st_s, cos, sin, [w[l] for w in weights], precise=l < depth - 1)
        news_p.append(new_p)
        news_s.append(new_s)
    stack = lambda news, i: jnp.stack([n[i] for n in news])
    y_prompt = x[:bp * lp].reshape(bp, lp, D_MODEL)
    y_sample = x[bp * lp:].reshape(bs, ls, D_MODEL)
    p_conv, p_delta, p_shift, p_wkv, p_ckv, p_krope = (stack(news_p, i) for i in range(6))
    s_conv, s_delta, s_shift, s_wkv, s_ckv, s_krope = (stack(news_s, i) for i in range(6))
    return (y_prompt, y_sample, p_ckv, p_krope, p_conv, p_delta, p_shift, p_wkv,
            s_ckv, s_krope, s_conv, s_delta, s_shift, s_wkv)


def kernel(x_prompt, x_sample, cache_c_kv, cache_k_rope, state_conv_a, state_delta_a, state_shift_b, state_wkv_b,
           norm_mix, w_in, a_conv_w, a_A_log, a_dt_bias, a_norm_w,
           b_mu, b_w0, b_w2, b_a0, b_a2, b_g2, b_k_k, b_k_a, b_r_k, b_ln_w, b_ln_b,
           c_q_norm, c_kv_norm, c_w_uq, c_w_ukv, c_q_gain, c_k_gain,
           w_out, norm_ffn, router_group, router_group_bias, router_expert, router_expert_bias,
           e_gate, e_up, e_down):
    weights = (norm_mix, w_in, a_conv_w, a_A_log, a_dt_bias, a_norm_w,
               b_mu, b_w0, b_w2, b_a0, b_a2, b_g2, b_k_k, b_k_a, b_r_k, b_ln_w, b_ln_b,
               c_q_norm, c_kv_norm, c_w_uq, c_w_ukv, c_q_gain, c_k_gain,
               w_out, norm_ffn, router_group, router_group_bias, router_expert, router_expert_bias,
               e_gate, e_up, e_down)
    return _forward(x_prompt, x_sample, cache_c_kv, cache_k_rope, state_conv_a, state_delta_a, state_shift_b,
                    state_wkv_b, weights)
```

```python
import functools

import jax
import jax.numpy as jnp
from jax import lax
from jax.experimental import pallas as pl
from jax.experimental.pallas import tpu as pltpu

F32 = jnp.float32
BF16 = jnp.bfloat16
I32 = jnp.int32

D_MODEL = 1024
DEPTH = 2
CHUNK = 64
EPS = 1e-6
LANES = 128
SUBLANES = 8
ROW_TILES = D_MODEL // LANES

A_HEADS, A_DK, A_DV, A_CONV = 6, 64, 64, 4
A_QK = A_HEADS * A_DK
A_V = A_HEADS * A_DV
A_CONV_CH = 2 * A_QK + A_V
A_WIDTH = A_CONV_CH + A_V + 2 * A_HEADS
A_SEG = A_CONV_CH + A_V + LANES

B_HEADS, B_N = 6, 64
B_C = B_HEADS * B_N
B_W_LORA, B_A_LORA, B_G_LORA = 64, 64, 128
B_WIDTH = 3 * B_C + B_W_LORA + B_A_LORA + B_G_LORA
B_GN_EPS = 64e-5

C_HEADS, C_NOPE, C_ROPE, C_VDIM = 4, 64, 32, 64
C_QK = C_NOPE + C_ROPE
C_Q_LORA, C_KV_LORA = 256, 128
C_WIDTH = C_Q_LORA + C_KV_LORA + C_ROPE
C_SEG = C_Q_LORA + C_KV_LORA + 2 * LANES
C_HPAD = LANES
ROPE_THETA = 10000.0

P_TOTAL = A_WIDTH + B_WIDTH + C_WIDTH
D_MIX = A_V + B_C + C_HEADS * C_VDIM

N_GROUPS, EXPERTS_PER_GROUP = 4, 8
N_EXPERTS = N_GROUPS * EXPERTS_PER_GROUP
TOP_K = 2
D_EXPERT = 256
MOE_TILE = 256
MIXER_CHUNKS_PER_STEP = 4

VMEM_LIMIT = 48 * 1024 * 1024

_NN = (((1,), (0,)), ((), ()))
_NT = (((1,), (1,)), ((), ()))
_TN = (((0,), (0,)), ((), ()))


def _pick(n, prefs):
    for p in prefs:
        if n % p == 0:
            return p
    raise ValueError(f"no tile for {n} in {prefs}")


def _cparams(sem):
    return pltpu.CompilerParams(dimension_semantics=sem, vmem_limit_bytes=VMEM_LIMIT)


def _bdot(a, b, dims=_NN):
    return lax.dot_general(a.astype(BF16), b.astype(BF16), dims, preferred_element_type=F32)


def _split2(a):
    hi = a.astype(BF16)
    lo = (a - hi.astype(F32)).astype(BF16)
    return hi, lo


def _dot3(a, b, dims=_NN):
    ah, al = _split2(a)
    bh, bl = _split2(b)
    f = lambda x, y: lax.dot_general(x, y, dims, preferred_element_type=F32)
    return f(ah, bh) + (f(ah, bl) + f(al, bh))


def _mdot(precise):
    return _dot3 if precise else _bdot


def _cumsum_rows(x, ltri):
    h = x.astype(BF16)
    r = x - h.astype(F32)
    m = r.astype(BF16)
    l = (r - m.astype(F32)).astype(BF16)
    d = lambda y: jnp.dot(ltri, y, preferred_element_type=F32)
    return d(h) + (d(m) + d(l))


def _tri_masks(c):
    row = lax.broadcasted_iota(I32, (c, c), 0)
    col = lax.broadcasted_iota(I32, (c, c), 1)
    return row >= col, row > col, (row == col).astype(F32)


def _chunk_tril(rows, c):
    shift = c.bit_length() - 1
    row = lax.broadcasted_iota(I32, (rows, rows), 0)
    col = lax.broadcasted_iota(I32, (rows, rows), 1)
    same = jnp.right_shift(row, shift) == jnp.right_shift(col, shift)
    return (same & (row >= col)).astype(BF16)


def _neumann_inv_many(ns, eye):
    c = eye.shape[0]
    keep_t = lax.broadcasted_iota(I32, (c, 2 * c), 1) >= c
    pts = [jnp.concatenate([n, eye], axis=1) for n in ns]
    m = 1
    while m < c:
        rs = [_dot3(pt[:, :c], pt) for pt in pts]
        pts = [r + jnp.where(keep_t, pt, 0.0) for r, pt in zip(rs, pts)]
        m *= 2
    return [pt[:, c:] for pt in pts]


def _softplus(x):
    return jnp.maximum(x, 0.0) + jnp.log1p(jnp.exp(-jnp.abs(x)))


def _silu(x):
    return x * jax.nn.sigmoid(x)


def _rms(x, eps=EPS):
    return x * lax.rsqrt(jnp.mean(x * x, axis=-1, keepdims=True) + eps)


def _l2n(x):
    return x * lax.rsqrt(jnp.sum(x * x, axis=-1, keepdims=True) + 1e-6)


def _rows_to_tiles(ref, val):
    n = val.shape[0]
    for j in range(ROW_TILES):
        ref[pl.ds(j, n, stride=SUBLANES), :] = val[:, j * LANES:(j + 1) * LANES]


def _tiles_to_rows(ref, start, n):
    return jnp.concatenate([ref[pl.ds(start + j, n, stride=SUBLANES), :] for j in range(ROW_TILES)], axis=1)


def _in_proj_kernel(*refs, precise, n_first_tiles):
    if n_first_tiles is None:
        x = refs[0][...]
        g_ref, *refs = refs[1:]
    else:
        x = jnp.where(pl.program_id(0) < n_first_tiles, refs[0][...], refs[1][...])
        g_ref, *refs = refs[2:]
    if precise:
        wh_ref, wl_ref, pa_ref, pb_ref, pc_ref = refs
    else:
        wh_ref, pa_ref, pb_ref, pc_ref = refs
    xn = _rms(x) * g_ref[...]
    xh = xn.astype(BF16)
    if precise:
        xl = (xn - xh.astype(F32)).astype(BF16)
    lo = 0
    for out_ref, width in ((pa_ref, A_SEG), (pb_ref, B_WIDTH), (pc_ref, C_SEG)):
        acc = jnp.dot(xh, wh_ref[:, lo:lo + width], preferred_element_type=F32)
        if precise:
            acc = acc + (jnp.dot(xh, wl_ref[:, lo:lo + width], preferred_element_type=F32)
                         + jnp.dot(xl, wh_ref[:, lo:lo + width], preferred_element_type=F32))
        out_ref[...] = acc
        lo += width


def _split_kernel(w_ref, hi_ref, lo_ref):
    hi, lo = _split2(w_ref[...])
    hi_ref[...] = hi
    lo_ref[...] = lo


def _split_hi_lo(w):
    r, c = w.shape
    tc = _pick(c, (512, 256, 128))
    spec = pl.BlockSpec((r, tc), lambda j: (0, j))
    return pl.pallas_call(
        _split_kernel,
        grid=(c // tc,),
        in_specs=[spec],
        out_specs=[spec, spec],
        out_shape=[jax.ShapeDtypeStruct((r, c), BF16), jax.ShapeDtypeStruct((r, c), BF16)],
        compiler_params=_cparams(("arbitrary",)),
        name="split_hi_lo",
    )(w)


def _in_proj(xs, g, w, precise):
    prefs = (256, 128, 64, 32, 16, 8) if precise else (512, 256, 128, 64, 32, 16, 8)
    t = sum(x.shape[0] for x in xs)
    if len(xs) == 1:
        tm = _pick(t, prefs)
        nft = None
        x_specs = [pl.BlockSpec((tm, D_MODEL), lambda i: (i, 0))]
    else:
        tm = _pick(xs[1].shape[0], prefs)
        assert xs[0].shape[0] % tm == 0
        nft = xs[0].shape[0] // tm
        x_specs = [pl.BlockSpec((tm, D_MODEL), lambda i: (jnp.minimum(i, nft - 1), 0)),
                   pl.BlockSpec((tm, D_MODEL), lambda i: (jnp.maximum(i - nft, 0), 0))]
    wtot = A_SEG + B_WIDTH + C_SEG
    wspec = pl.BlockSpec((D_MODEL, wtot), lambda i: (0, 0))
    w_args = _split_hi_lo(w) if precise else [w.astype(BF16)]
    return pl.pallas_call(
        functools.partial(_in_proj_kernel, precise=precise, n_first_tiles=nft),
        grid=(t // tm,),
        in_specs=x_specs + [pl.BlockSpec((1, D_MODEL), lambda i: (0, 0))] + [wspec] * len(w_args),
        out_specs=[
            pl.BlockSpec((tm, A_SEG), lambda i: (i, 0)),
            pl.BlockSpec((tm, B_WIDTH), lambda i: (i, 0)),
            pl.BlockSpec((tm, C_SEG), lambda i: (i, 0)),
        ],
        out_shape=[
            jax.ShapeDtypeStruct((t, A_SEG), F32),
            jax.ShapeDtypeStruct((t, B_WIDTH), F32),
            jax.ShapeDtypeStruct((t, C_SEG), F32),
        ],
        compiler_params=_cparams(("arbitrary",)),
        name="in_proj",
    )(*xs, g.reshape(1, D_MODEL), *w_args)


def _mixer_a_kernel(pa_ref, cprev_ref, s0_ref, cw_ref, alog_ref, dtb_ref, nw_ref,
                    oa_ref, cnew_ref, snew_ref, tail_ref, s_ref, *, c, nch, n_steps, precise):
    dot = _mdot(precise)
    step = pl.program_id(1)
    rows = c * nch

    @pl.when(step == 0)
    def _():
        tail_ref[...] = jnp.zeros_like(tail_ref)
        tail_ref[SUBLANES - (A_CONV - 1):SUBLANES, :] = cprev_ref[...]
        s_ref[...] = s0_ref[...]

    x = pa_ref[...]
    qkv = x[:, :A_CONV_CH]
    z = x[:, A_CONV_CH:A_CONV_CH + A_V]
    ab = x[:, A_CONV_CH + A_V:A_SEG]

    xp = jnp.concatenate([tail_ref[...], qkv], axis=0)
    cw = cw_ref[...]
    y = qkv * cw[A_CONV - 1:A_CONV]
    for j in range(A_CONV - 1):
        o = SUBLANES - (A_CONV - 1) + j
        y = y + xp[o:o + rows] * cw[j:j + 1]
    tail_ref[...] = qkv[rows - SUBLANES:rows]

    @pl.when(step == n_steps - 1)
    def _():
        cnew_ref[...] = qkv[rows - (A_CONV - 1):rows]

    act = _silu(y)
    g_all = -jnp.exp(alog_ref[...]) * _softplus(ab + dtb_ref[...])
    beta_all = jax.nn.sigmoid(ab)
    gc_all = _cumsum_rows(g_all, _chunk_tril(rows, c))
    gc_t = gc_all.T
    tril, strict, eye = _tri_masks(c)
    nw = nw_ref[...]

    qn = [_l2n(act[:, h * A_DK:(h + 1) * A_DK]) * (A_DK ** -0.5) for h in range(A_HEADS)]
    kn = [_l2n(act[:, A_QK + h * A_DK:A_QK + (h + 1) * A_DK]) for h in range(A_HEADS)]
    vv = [act[:, 2 * A_QK + h * A_DV:2 * A_QK + (h + 1) * A_DV] for h in range(A_HEADS)]

    items = [(ci, h) for ci in range(nch) for h in range(A_HEADS)]
    q_i, k_i, kb_i, vb_i, dec_i, gcc_i, gl_i = [], [], [], [], [], [], []
    for ci, h in items:
        r0 = ci * c
        beta = beta_all[r0:r0 + c, A_HEADS + h:A_HEADS + h + 1]
        gcc = gc_all[r0:r0 + c, h:h + 1]
        gcr = gc_t[h:h + 1, r0:r0 + c]
        k = kn[h][r0:r0 + c]
        q_i.append(qn[h][r0:r0 + c])
        k_i.append(k)
        kb_i.append(k * beta)
        vb_i.append(vv[h][r0:r0 + c] * beta)
        dec_i.append(jnp.exp(jnp.where(tril, gcc - gcr, -1e30)))
        gcc_i.append(gcc)
        gl_i.append(gc_all[r0 + c - 1:r0 + c, h:h + 1])
    a_i = [jnp.where(strict, dot(kb, k, _NT) * dec, 0.0) for kb, k, dec in zip(kb_i, k_i, dec_i)]
    t_i = _neumann_inv_many([-a for a in a_i], eye)
    egc_i = [jnp.exp(g) for g in gcc_i]
    sol_i = [_dot3(t, jnp.concatenate([vb, kb * e], axis=1)) for t, vb, kb, e in zip(t_i, vb_i, kb_i, egc_i)]
    attn_i = [jnp.where(tril, dot(q, k, _NT) * dec, 0.0) for q, k, dec in zip(q_i, k_i, dec_i)]
    kq_i = [jnp.concatenate([sol[:, A_DV:], q * e], axis=0) for sol, q, e in zip(sol_i, q_i, egc_i)]
    kg_i = [k * jnp.exp(gl - g) for k, gl, g in zip(k_i, gl_i, gcc_i)]

    s = [s_ref[h] for h in range(A_HEADS)]
    out_rows = []
    for ci in range(nch):
        ids = [ci * A_HEADS + h for h in range(A_HEADS)]
        ks = [dot(kq_i[i], s[h]) for h, i in enumerate(ids)]
        u = [sol_i[i][:, :A_DV] - ks[h][:c] for h, i in enumerate(ids)]
        o = [ks[h][c:] + dot(attn_i[i], u[h]) for h, i in enumerate(ids)]
        s = [s[h] * jnp.exp(gl_i[i]) + dot(kg_i[i], u[h], _TN) for h, i in enumerate(ids)]
        r0 = ci * c
        outs = [_rms(o[h]) * nw * _silu(z[r0:r0 + c, h * A_DV:(h + 1) * A_DV]) for h in range(A_HEADS)]
        out_rows.append(jnp.concatenate(outs, axis=1))
    for h in range(A_HEADS):
        s_ref[h] = s[h]
    oa_ref[...] = jnp.concatenate(out_rows, axis=0).astype(oa_ref.dtype)

    @pl.when(step == n_steps - 1)
    def _():
        snew_ref[...] = s_ref[...]


def _row(v, width=LANES):
    v = v.reshape(1, -1).astype(F32)
    return jnp.pad(v, ((0, 0), (0, width - v.shape[1])))


def _mixer_geometry(row0, seq):
    c = min(CHUNK, seq)
    nch = _pick(seq // c, (MIXER_CHUNKS_PER_STEP, 2, 1))
    rows = c * nch
    assert row0 % rows == 0
    return c, nch, rows, seq // rows, row0 // rows


def _mixer_a(pa, row0, bsz, seq, precise, conv_prev, s0, conv_w, a_log, dt_bias, norm_w):
    c, nch, rows, n_steps, blk0 = _mixer_geometry(row0, seq)
    kern = functools.partial(_mixer_a_kernel, c=c, nch=nch, n_steps=n_steps, precise=precise)
    full = lambda shape: pl.BlockSpec(shape, lambda b, n: (0,) * len(shape))
    return pl.pallas_call(
        kern,
        grid=(bsz, n_steps),
        in_specs=[
            pl.BlockSpec((rows, A_SEG), lambda b, n: (blk0 + b * n_steps + n, 0)),
            pl.BlockSpec((None, A_CONV - 1, A_CONV_CH), lambda b, n: (b, 0, 0)),
            pl.BlockSpec((None, A_HEADS, A_DK, A_DV), lambda b, n: (b, 0, 0, 0)),
            full((A_CONV, A_CONV_CH)),
            full((1, LANES)),
            full((1, LANES)),
            full((1, A_DV)),
        ],
        out_specs=[
            pl.BlockSpec((rows, A_V), lambda b, n: (b * n_steps + n, 0)),
            pl.BlockSpec((None, A_CONV - 1, A_CONV_CH), lambda b, n: (b, 0, 0)),
            pl.BlockSpec((None, A_HEADS, A_DK, A_DV), lambda b, n: (b, 0, 0, 0)),
        ],
        out_shape=[
            jax.ShapeDtypeStruct((bsz * seq, A_V), F32 if precise else BF16),
            jax.ShapeDtypeStruct((bsz, A_CONV - 1, A_CONV_CH), F32),
            jax.ShapeDtypeStruct((bsz, A_HEADS, A_DK, A_DV), F32),
        ],
        scratch_shapes=[pltpu.VMEM((SUBLANES, A_CONV_CH), F32), pltpu.VMEM((A_HEADS, A_DK, A_DV), F32)],
        compiler_params=_cparams(("arbitrary", "arbitrary")),
        name="mixer_a",
    )(pa, conv_prev, s0, conv_w, _row(a_log), _row(dt_bias), norm_w.reshape(1, A_DV))


def _mixer_b_kernel(pb_ref, sprev_ref, s0_ref, mu_ref, w0_ref, w2_ref, a0_ref, a2_ref, g2_ref,
                    kk_ref, ka_ref, rk_ref, lnw_ref, lnb_ref,
                    ob_ref, shnew_ref, snew_ref, last_ref, s_ref, *, c, nch, n_steps, precise):
    dot = _mdot(precise)
    step = pl.program_id(1)
    rows = c * nch

    @pl.when(step == 0)
    def _():
        last_ref[...] = sprev_ref[...]
        s_ref[...] = s0_ref[...]

    p = pb_ref[...]
    rowi = lax.broadcasted_iota(I32, (rows, 1), 0)
    prev = jnp.where(rowi == 0, last_ref[...], pltpu.roll(p, 1, 0))
    last_ref[...] = p[rows - 1:rows]

    @pl.when(step == n_steps - 1)
    def _():
        shnew_ref[...] = p[rows - 1:rows]

    xs = p + (prev - p) * mu_ref[...]
    r_all = xs[:, 0:B_C]
    k_all = xs[:, B_C:2 * B_C]
    v_all = xs[:, 2 * B_C:3 * B_C]
    xwa = xs[:, 3 * B_C:3 * B_C + B_W_LORA + B_A_LORA]
    xg = xs[:, 3 * B_C + B_W_LORA + B_A_LORA:B_WIDTH]

    w_log = -_softplus(-(w0_ref[...] + dot(jnp.tanh(xwa), w2_ref[...]))) - 0.5
    lw = -jnp.exp(w_log)
    rate = jax.nn.sigmoid(a0_ref[...] + dot(xwa, a2_ref[...]))
    gate = dot(jax.nn.sigmoid(xg), g2_ref[...])
    kkr = k_all * kk_ref[...]
    k2_all = k_all * (1.0 + (rate - 1.0) * ka_ref[...])

    cl = _cumsum_rows(lw, _chunk_tril(rows, c))
    e_cl = jnp.exp(cl)
    e_neg = jnp.exp(-cl)
    at_all = jnp.exp(cl - lw)
    bt_all = rate * e_neg
    kt_all = k2_all * e_neg
    rt_all = r_all * e_cl
    tril, strict, eye = _tri_masks(c)
    rk = rk_ref[...]
    lnw = lnw_ref[...]
    lnb = lnb_ref[...]
    kkn = [_l2n(kkr[:, h * B_N:(h + 1) * B_N]) for h in range(B_HEADS)]

    items = [(ci, h) for ci in range(nch) for h in range(B_HEADS)]
    at_i, bt_i, kt_i, rt_i, v_i = [], [], [], [], []
    for ci, h in items:
        rs = slice(ci * c, (ci + 1) * c)
        ls = slice(h * B_N, (h + 1) * B_N)
        kk = kkn[h][rs]
        at_i.append(-kk * at_all[rs, ls])
        bt_i.append(kk * bt_all[rs, ls])
        kt_i.append(kt_all[rs, ls])
        rt_i.append(rt_all[rs, ls])
        v_i.append(v_all[rs, ls])
    bk_i = [jnp.concatenate([b, k], axis=0) for b, k in zip(bt_i, kt_i)]
    ar_i = [jnp.concatenate([a, r], axis=0) for a, r in zip(at_i, rt_i)]
    row2 = lax.broadcasted_iota(I32, (c, 2 * c), 0)
    col2 = jnp.bitwise_and(lax.broadcasted_iota(I32, (c, 2 * c), 1), c - 1)
    strict2 = row2 > col2
    tril2 = row2 >= col2
    g_a = [jnp.where(strict2, _dot3(a, bk, _NT), 0.0) for a, bk in zip(at_i, bk_i)]
    g_r = [jnp.where(tril2, dot(r, bk, _NT), 0.0) for r, bk in zip(rt_i, bk_i)]
    t_i = _neumann_inv_many([g[:, :c] for g in g_a], eye)
    mv_i = [dot(g[:, c:], v) for g, v in zip(g_a, v_i)]

    s = [s_ref[h] for h in range(B_HEADS)]
    out_rows = []
    for ci in range(nch):
        ids = [ci * B_HEADS + h for h in range(B_HEADS)]
        rs = slice(ci * c, (ci + 1) * c)
        ars = [dot(ar_i[i], s[h], _NT) for h, i in enumerate(ids)]
        u = [_dot3(t_i[i], ars[h][:c] + mv_i[i]) for h, i in enumerate(ids)]
        uv = [jnp.concatenate([u[h], v_i[i]], axis=0) for h, i in enumerate(ids)]
        o = [ars[h][c:] + dot(g_r[i], uv[h]) for h, i in enumerate(ids)]
        e_last = e_cl[(ci + 1) * c - 1:(ci + 1) * c]
        s = [(s[h] + dot(uv[h], bk_i[i], _TN)) * e_last[:, h * B_N:(h + 1) * B_N] for h, i in enumerate(ids)]
        outs = []
        for h in range(B_HEADS):
            ls = slice(h * B_N, (h + 1) * B_N)
            mean = jnp.mean(o[h], axis=-1, keepdims=True)
            var = jnp.mean(jnp.square(o[h] - mean), axis=-1, keepdims=True)
            gn = (o[h] - mean) * lax.rsqrt(var + B_GN_EPS) * lnw[:, ls] + lnb[:, ls]
            bonus = jnp.sum(r_all[rs, ls] * k2_all[rs, ls] * rk[:, ls], axis=-1, keepdims=True) * v_all[rs, ls]
            outs.append((gn + bonus) * gate[rs, ls])
        out_rows.append(jnp.concatenate(outs, axis=1))
    for h in range(B_HEADS):
        s_ref[h] = s[h]
    ob_ref[...] = jnp.concatenate(out_rows, axis=0).astype(ob_ref.dtype)

    @pl.when(step == n_steps - 1)
    def _():
        snew_ref[...] = s_ref[...]


def _mixer_b(pb, row0, bsz, seq, precise, shift_prev, s0, mu, w0, w2, a0, a2, g2, k_k, k_a, r_k, ln_w, ln_b):
    c, nch, rows, n_steps, blk0 = _mixer_geometry(row0, seq)
    kern = functools.partial(_mixer_b_kernel, c=c, nch=nch, n_steps=n_steps, precise=precise)
    full = lambda shape: pl.BlockSpec(shape, lambda b, n: (0,) * len(shape))
    lora = B_W_LORA + B_A_LORA
    w2p = jnp.zeros((lora, B_C), F32).at[:B_W_LORA].set(w2)
    a2p = jnp.zeros((lora, B_C), F32).at[B_W_LORA:].set(a2)
    vec = lambda v: v.reshape(1, -1).astype(F32)
    return pl.pallas_call(
        kern,
        grid=(bsz, n_steps),
        in_specs=[
            pl.BlockSpec((rows, B_WIDTH), lambda b, n: (blk0 + b * n_steps + n, 0)),
            pl.BlockSpec((None, 1, B_WIDTH), lambda b, n: (b, 0, 0)),
            pl.BlockSpec((None, B_HEADS, B_N, B_N), lambda b, n: (b, 0, 0, 0)),
            full((1, B_WIDTH)),
            full((1, B_C)), full((lora, B_C)),
            full((1, B_C)), full((lora, B_C)),
            full((B_G_LORA, B_C)),
            full((1, B_C)), full((1, B_C)), full((1, B_C)), full((1, B_C)), full((1, B_C)),
        ],
        out_specs=[
            pl.BlockSpec((rows, B_C), lambda b, n: (b * n_steps + n, 0)),
            pl.BlockSpec((None, 1, B_WIDTH), lambda b, n: (b, 0, 0)),
            pl.BlockSpec((None, B_HEADS, B_N, B_N), lambda b, n: (b, 0, 0, 0)),
        ],
        out_shape=[
            jax.ShapeDtypeStruct((bsz * seq, B_C), F32 if precise else BF16),
            jax.ShapeDtypeStruct((bsz, 1, B_WIDTH), F32),
            jax.ShapeDtypeStruct((bsz, B_HEADS, B_N, B_N), F32),
        ],
        scratch_shapes=[pltpu.VMEM((1, B_WIDTH), F32), pltpu.VMEM((B_HEADS, B_N, B_N), F32)],
        compiler_params=_cparams(("arbitrary", "arbitrary")),
        name="mixer_b",
    )(pb, shift_prev, s0, vec(mu), vec(w0), w2p, vec(a0), a2p, g2,
      vec(k_k), vec(k_a), vec(r_k), vec(ln_w), vec(ln_b))


def _kv_expand(ckvn, krope, wuk_ref, wuv_ref, gk_ref, k_ref, v_ref, dot):
    k0 = dot(ckvn, wuk_ref[...])
    gk = gk_ref[...]
    for h in range(C_HEADS):
        kh = k0[:, h * C_HPAD:(h + 1) * C_HPAD] + krope
        ss = jnp.sum(kh * kh, axis=-1, keepdims=True) * (1.0 / C_QK)
        k_ref[:, h * C_HPAD:(h + 1) * C_HPAD] = (kh * lax.rsqrt(ss + EPS) * gk).astype(k_ref.dtype)
    v_ref[...] = dot(ckvn, wuv_ref[...]).astype(v_ref.dtype)


def _c_prep_kernel(pc_ref, cos_ref, sin_ref, qn_ref, kvn_ref, wq_ref, wqr_ref, wuk_ref, wuv_ref, gq_ref, gk_ref,
                   q_ref, k_ref, v_ref, ckv_ref, kr_ref, *, precise):
    dot = _mdot(precise)
    pc = pc_ref[...]
    cq = pc[:, 0:C_Q_LORA]
    ckv_raw = pc[:, C_Q_LORA:C_Q_LORA + C_KV_LORA]
    krp = pc[:, C_Q_LORA + C_KV_LORA:C_Q_LORA + C_KV_LORA + LANES]
    krr = pc[:, C_Q_LORA + C_KV_LORA + LANES:C_SEG]
    cos = cos_ref[...]
    sin = sin_ref[...]
    lane = lax.broadcasted_iota(I32, cos.shape, 1)
    cosq = jnp.where(lane < C_NOPE, 1.0, cos)

    cqn = _rms(cq) * qn_ref[...]
    q0 = dot(cqn, wq_ref[...])
    q1 = dot(cqn, wqr_ref[...])
    gq = gq_ref[...] * (C_QK ** -0.5)
    for h in range(C_HEADS):
        sl = slice(h * C_HPAD, (h + 1) * C_HPAD)
        qh = q0[:, sl] * cosq + q1[:, sl] * sin
        ss = jnp.sum(qh * qh, axis=-1, keepdims=True) * (1.0 / C_QK)
        q_ref[:, sl] = (qh * lax.rsqrt(ss + EPS) * gq).astype(q_ref.dtype)

    ckvn = _rms(ckv_raw) * kvn_ref[...]
    ckv_ref[...] = ckvn
    krope = krp * cos + krr * sin
    kr_ref[...] = krope[:, C_NOPE:C_NOPE + C_ROPE]
    _kv_expand(ckvn, krope, wuk_ref, wuv_ref, gk_ref, k_ref, v_ref, dot)


def _kv_expand_kernel(ckv_ref, kr_ref, wuk_ref, wuv_ref, gk_ref, k_ref, v_ref, *, precise):
    _kv_expand(ckv_ref[...], kr_ref[...], wuk_ref, wuv_ref, gk_ref, k_ref, v_ref, _mdot(precise))


def _c_weights(q_norm, kv_norm, w_uq, w_ukv, q_gain, k_gain):
    half = C_ROPE // 2
    wq = jnp.zeros((C_Q_LORA, C_HEADS * C_HPAD), F32)
    wqr = jnp.zeros((C_Q_LORA, C_HEADS * C_HPAD), F32)
    wuk = jnp.zeros((C_KV_LORA, C_HEADS * C_HPAD), F32)
    wuv = jnp.zeros((C_KV_LORA, C_HEADS * C_VDIM), F32)
    for h in range(C_HEADS):
        wh = w_uq[:, h * C_QK:(h + 1) * C_QK]
        wq = wq.at[:, h * C_HPAD:h * C_HPAD + C_QK].set(wh)
        rot = jnp.concatenate([-wh[:, C_NOPE + half:], wh[:, C_NOPE:C_NOPE + half]], axis=1)
        wqr = wqr.at[:, h * C_HPAD + C_NOPE:h * C_HPAD + C_QK].set(rot)
        kvh = w_ukv[:, h * (C_NOPE + C_VDIM):(h + 1) * (C_NOPE + C_VDIM)]
        wuk = wuk.at[:, h * C_HPAD:h * C_HPAD + C_NOPE].set(kvh[:, :C_NOPE])
        wuv = wuv.at[:, h * C_VDIM:(h + 1) * C_VDIM].set(kvh[:, C_NOPE:])
    gain = lambda g: _row(jnp.concatenate([g[:C_NOPE], g[C_NOPE:], g[C_NOPE:]]))
    return (q_norm.reshape(1, -1), kv_norm.reshape(1, -1), wq, wqr, wuk, wuv, gain(q_gain), gain(k_gain))


def _rope_tables(pos):
    half = C_ROPE // 2
    inv = ROPE_THETA ** (-(jnp.arange(0, C_ROPE, 2, dtype=F32) / C_ROPE))
    ang = pos.astype(F32)[:, None] * inv[None, :]
    cos, sin = jnp.cos(ang), jnp.sin(ang)
    pad = lambda t: jnp.pad(jnp.concatenate([t, t], axis=1), ((0, 0), (C_NOPE, LANES - C_NOPE - 2 * half)))
    return pad(cos), pad(sin)


def _c_prep(pc, cos, sin, cw, precise):
    t = pc.shape[0]
    tm = _pick(t, (512, 256, 128, 64, 32, 16, 8))
    qn, kvn, wq, wqr, wuk, wuv, gq, gk = cw
    full = lambda a: pl.BlockSpec(a.shape, lambda i: (0,) * a.ndim)
    rows = lambda w: pl.BlockSpec((tm, w), lambda i: (i, 0))
    act = F32 if precise else BF16
    return pl.pallas_call(
        functools.partial(_c_prep_kernel, precise=precise),
        grid=(t // tm,),
        in_specs=[rows(C_SEG), rows(LANES), rows(LANES)] + [full(a) for a in (qn, kvn, wq, wqr, wuk, wuv, gq, gk)],
        out_specs=[rows(C_HEADS * C_HPAD), rows(C_HEADS * C_HPAD), rows(C_HEADS * C_VDIM), rows(C_KV_LORA),
                   rows(C_ROPE)],
        out_shape=[
            jax.ShapeDtypeStruct((t, C_HEADS * C_HPAD), act),
            jax.ShapeDtypeStruct((t, C_HEADS * C_HPAD), act),
            jax.ShapeDtypeStruct((t, C_HEADS * C_VDIM), act),
            jax.ShapeDtypeStruct((t, C_KV_LORA), F32),
            jax.ShapeDtypeStruct((t, C_ROPE), F32),
        ],
        compiler_params=_cparams(("arbitrary",)),
        name="c_prep",
    )(pc, cos, sin, qn, kvn, wq, wqr, wuk, wuv, gq, gk)


def _kv_expand_call(ckv, krp, cw, precise):
    t = ckv.shape[0]
    tm = _pick(t, (1024, 512, 256, 128, 64, 32, 16, 8))
    _, _, _, _, wuk, wuv, _, gk = cw
    full = lambda a: pl.BlockSpec(a.shape, lambda i: (0,) * a.ndim)
    rows = lambda w: pl.BlockSpec((tm, w), lambda i: (i, 0))
    act = F32 if precise else BF16
    return pl.pallas_call(
        functools.partial(_kv_expand_kernel, precise=precise),
        grid=(t // tm,),
        in_specs=[rows(C_KV_LORA), rows(LANES), full(wuk), full(wuv), full(gk)],
        out_specs=[rows(C_HEADS * C_HPAD), rows(C_HEADS * C_VDIM)],
        out_shape=[
            jax.ShapeDtypeStruct((t, C_HEADS * C_HPAD), act),
            jax.ShapeDtypeStruct((t, C_HEADS * C_VDIM), act),
        ],
        compiler_params=_cparams(("arbitrary",)),
        name="kv_expand",
    )(ckv, krp, wuk, wuv, gk)


def _attn_kernel(*refs, tq, nq, past, has_cache, precise):
    if has_cache:
        q_ref, kn_ref, vn_ref, kc_ref, vc_ref, o_ref = refs
    else:
        q_ref, kn_ref, vn_ref, o_ref = refs
    dot = _mdot(precise)
    qi = pl.program_id(1)
    shift = CHUNK.bit_length() - 1

    def attend(n_new):
        q = q_ref[...]
        q_chunk = jnp.right_shift(past + qi * tq + lax.broadcasted_iota(I32, (tq, 1), 0), shift)
        parts = [(kn_ref[0:n_new, :], vn_ref[0:n_new, :], past, n_new)]
        if has_cache:
            parts.append((kc_ref[...], vc_ref[...], 0, past))
        vis = [jnp.right_shift(p0 + lax.broadcasted_iota(I32, (1, w), 1), shift) <= q_chunk for _, _, p0, w in parts]
        outs = []
        for h in range(C_HEADS):
            sl = slice(h * C_HPAD, (h + 1) * C_HPAD)
            vsl = slice(h * C_VDIM, (h + 1) * C_VDIM)
            scores = [jnp.where(v, dot(q[:, sl], kb[:, sl], _NT), -1e30) for v, (kb, _, _, _) in zip(vis, parts)]
            m = functools.reduce(jnp.maximum, [jnp.max(s, axis=-1, keepdims=True) for s in scores])
            probs = [jnp.exp(s - m) for s in scores]
            den = sum(jnp.sum(p, axis=-1, keepdims=True) for p in probs)
            acc = sum(dot(p, vb[:, vsl]) for p, (_, vb, _, _) in zip(probs, parts))
            outs.append(acc / den)
        o_ref[...] = jnp.concatenate(outs, axis=1).astype(o_ref.dtype)

    if nq == 1:
        attend(tq)
    else:
        for blk in range(nq):
            pl.when(qi == blk)(functools.partial(attend, (blk + 1) * tq))


def _attention(q, k, v, row0, bsz, seq, precise, k_cache=None, v_cache=None):
    has_cache = k_cache is not None
    past = k_cache.shape[0] // bsz if has_cache else 0
    tq = _pick(seq, (256, 128, 64, 32, 16, 8))
    assert tq % CHUNK == 0 or tq == seq
    nq = seq // tq
    kern = functools.partial(_attn_kernel, tq=tq, nq=nq, past=past, has_cache=has_cache, precise=precise)
    qw, vw = C_HEADS * C_HPAD, C_HEADS * C_VDIM
    in_specs = [
        pl.BlockSpec((tq, qw), lambda b, i: (row0 // tq + b * nq + i, 0)),
        pl.BlockSpec((seq, qw), lambda b, i: (row0 // seq + b, 0)),
        pl.BlockSpec((seq, vw), lambda b, i: (row0 // seq + b, 0)),
    ]
    args = [q, k, v]
    if has_cache:
        in_specs += [pl.BlockSpec((past, qw), lambda b, i: (b, 0)), pl.BlockSpec((past, vw), lambda b, i: (b, 0))]
        args += [k_cache, v_cache]
    return pl.pallas_call(
        kern,
        grid=(bsz, nq),
        in_specs=in_specs,
        out_specs=pl.BlockSpec((tq, vw), lambda b, i: (b * nq + i, 0)),
        out_shape=jax.ShapeDtypeStruct((bsz * seq, vw), F32 if precise else BF16),
        compiler_params=_cparams(("arbitrary", "arbitrary")),
        name="attention",
    )(*args)


def _out_proj_kernel(*refs, n_prompt_tiles, precise, split_x):
    dot = _mdot(precise)
    is_p = pl.program_id(0) < n_prompt_tiles
    if split_x:
        y = jnp.where(is_p, refs[0][...], refs[1][...])
        refs = refs[2:]
    else:
        y = refs[0][...]
        refs = refs[1:]
    (oap_ref, obp_ref, ocp_ref, oas_ref, obs_ref, ocs_ref, w_ref, g_ref, wr_ref, br_ref,
     y_ref, xt_ref, ri_ref, rw_ref) = refs
    oa = jnp.where(is_p, oap_ref[...], oas_ref[...])
    ob = jnp.where(is_p, obp_ref[...], obs_ref[...])
    oc = jnp.where(is_p, ocp_ref[...], ocs_ref[...])
    y = y + dot(oa, w_ref[0:A_V])
    y = y + dot(ob, w_ref[A_V:A_V + B_C])
    y = y + dot(oc, w_ref[A_V + B_C:D_MIX])
    y_ref[...] = y
    xn = _rms(y) * g_ref[...]
    _rows_to_tiles(xt_ref, xn)

    logits = _dot3(xn, wr_ref[...]) + br_ref[...]
    lane = lax.broadcasted_iota(I32, logits.shape, 1)
    lanef = lane.astype(F32)
    neg = -1e30
    lg = jnp.where(lane < N_GROUPS, logits, neg)
    mg = jnp.max(lg, axis=-1, keepdims=True)
    pg = 1.0 / jnp.sum(jnp.exp(lg - mg), axis=-1, keepdims=True)
    gidx = jnp.min(jnp.where(lg == mg, lanef, float(LANES)), axis=-1, keepdims=True)
    lo = N_GROUPS + EXPERTS_PER_GROUP * gidx
    in_grp = (lanef >= lo) & (lanef < lo + EXPERTS_PER_GROUP)
    el = jnp.where(in_grp, logits, neg)
    m1 = jnp.max(el, axis=-1, keepdims=True)
    i1 = jnp.min(jnp.where(el == m1, lanef, float(LANES)), axis=-1, keepdims=True)
    el2 = jnp.where(lanef == i1, neg, el)
    m2 = jnp.max(el2, axis=-1, keepdims=True)
    i2 = jnp.min(jnp.where(el2 == m2, lanef, float(LANES)), axis=-1, keepdims=True)
    den = jnp.sum(jnp.exp(el - m1), axis=-1, keepdims=True)
    p1 = 1.0 / den
    p2 = jnp.exp(m2 - m1) / den
    w1 = pg * p1 / (p1 + p2)
    w2 = pg * p2 / (p1 + p2)
    e1 = (i1 - N_GROUPS).astype(I32)
    e2 = (i2 - N_GROUPS).astype(I32)
    ri_ref[...] = jnp.where(lane == 0, e1, jnp.where(lane == 1, e2, 0))
    rw_ref[...] = jnp.where(lane == 0, w1, jnp.where(lane == 1, w2, 0.0))


def _out_proj(xs, mix_p, mix_s, w_out, g, w_router, b_router, precise):
    tp, ts = mix_p[0].shape[0], mix_s[0].shape[0]
    t = tp + ts
    tm = _pick(ts, (512, 256, 128, 64, 32, 16, 8))
    assert tp % tm == 0 and sum(x.shape[0] for x in xs) == t
    npt = tp // tm
    kern = functools.partial(_out_proj_kernel, n_prompt_tiles=npt, precise=precise, split_x=len(xs) == 2)
    pspec = lambda w: pl.BlockSpec((tm, w), lambda i: (jnp.minimum(i, npt - 1), 0))
    sspec = lambda w: pl.BlockSpec((tm, w), lambda i: (jnp.maximum(i - npt, 0), 0))
    full = lambda a: pl.BlockSpec(a.shape, lambda i: (0,) * a.ndim)
    rows = lambda w: pl.BlockSpec((tm, w), lambda i: (i, 0))
    cw = C_HEADS * C_VDIM
    g2 = g.reshape(1, D_MODEL)
    x_specs = [pspec(D_MODEL), sspec(D_MODEL)] if len(xs) == 2 else [rows(D_MODEL)]
    return pl.pallas_call(
        kern,
        grid=(t // tm,),
        in_specs=x_specs + [pspec(A_V), pspec(B_C), pspec(cw), sspec(A_V), sspec(B_C), sspec(cw),
                            full(w_out), full(g2), full(w_router), full(b_router)],
        out_specs=[rows(D_MODEL), pl.BlockSpec((tm * SUBLANES, LANES), lambda i: (i, 0)), rows(LANES), rows(LANES)],
        out_shape=[
            jax.ShapeDtypeStruct((t, D_MODEL), F32),
            jax.ShapeDtypeStruct((t * SUBLANES, LANES), F32),
            jax.ShapeDtypeStruct((t, LANES), I32),
            jax.ShapeDtypeStruct((t, LANES), F32),
        ],
        compiler_params=_cparams(("arbitrary",)),
        name="out_proj_router",
    )(*xs, *mix_p, *mix_s, w_out, g2, w_router, b_router)


def _rank_kernel(ri_ref, rank_ref, cnt_ref, base_ref):
    @pl.when(pl.program_id(0) == 0)
    def _():
        base_ref[...] = jnp.zeros_like(base_ref)

    ri = ri_ref[...]
    tm = ri.shape[0]
    lane = lax.broadcasted_iota(I32, ri.shape, 1)
    oh0 = lane == ri[:, 0:1]
    oh1 = lane == ri[:, 1:2]
    cnt = oh0.astype(F32) + oh1.astype(F32)
    row = lax.broadcasted_iota(I32, (tm, tm), 0)
    col = lax.broadcasted_iota(I32, (tm, tm), 1)
    before = jnp.dot((row > col).astype(BF16), cnt.astype(BF16), preferred_element_type=F32) + base_ref[...]
    r0 = jnp.sum(jnp.where(oh0, before, 0.0), axis=-1, keepdims=True)
    r1 = jnp.sum(jnp.where(oh1, before, 0.0), axis=-1, keepdims=True)
    rank_ref[...] = jnp.where(lane == 0, r0, jnp.where(lane == 1, r1, 0.0)).astype(I32)
    total = base_ref[...] + jnp.sum(cnt, axis=0, keepdims=True)
    base_ref[...] = total
    cnt_ref[...] = total.astype(I32)


def _rank(ri):
    t = ri.shape[0]
    tm = _pick(t, (256, 128, 64, 32, 16, 8))
    return pl.pallas_call(
        _rank_kernel,
        grid=(t // tm,),
        in_specs=[pl.BlockSpec((tm, LANES), lambda i: (i, 0))],
        out_specs=[pl.BlockSpec((tm, LANES), lambda i: (i, 0)), pl.BlockSpec((1, LANES), lambda i: (0, 0))],
        out_shape=[jax.ShapeDtypeStruct((t, LANES), I32), jax.ShapeDtypeStruct((1, LANES), I32)],
        scratch_shapes=[pltpu.VMEM((1, LANES), F32)],
        compiler_params=_cparams(("arbitrary",)),
        name="moe_rank",
    )(ri)


def _gather_tiles(idx_ref, n, src_hbm, buf, slot, sem):
    def body(j, carry):
        for queue in range(2):
            r = 2 * j + queue
            src = pl.multiple_of(idx_ref[0, r] * SUBLANES, SUBLANES)
            dst = pl.multiple_of((slot * n + r) * SUBLANES, SUBLANES)
            pltpu.make_async_copy(src_hbm.at[pl.ds(src, SUBLANES)], buf.at[pl.ds(dst, SUBLANES)],
                                  sem.at[slot]).start(priority=queue)
        return carry

    lax.fori_loop(0, n // 2, body, 0, unroll=4)


def _gather_wait(n, src_hbm, buf, slot, sem):
    base = pl.multiple_of(slot * n * SUBLANES, SUBLANES)
    pltpu.make_async_copy(src_hbm.at[pl.ds(0, n * SUBLANES)], buf.at[pl.ds(base, n * SUBLANES)], sem.at[slot]).wait()
    return base


def _expert_kernel(te_ref, nu_ref, idc_ref, idn_ref, xt_hbm, wg_ref, wu_ref, wd_ref, o_ref,
                   xbuf, wgb, wub, wdb, sem):
    i = pl.program_id(0)
    n_used = nu_ref[0]
    slot = lax.rem(i, 2)

    @pl.when((i == 0) & (n_used > 0))
    def _():
        _gather_tiles(idc_ref, MOE_TILE, xt_hbm, xbuf, 0, sem)

    @pl.when(i + 1 < n_used)
    def _():
        _gather_tiles(idn_ref, MOE_TILE, xt_hbm, xbuf, 1 - slot, sem)

    @pl.when((i == 0) | (te_ref[i] != te_ref[jnp.maximum(i - 1, 0)]))
    def _():
        wgb[...] = wg_ref[...].astype(BF16)
        wub[...] = wu_ref[...].astype(BF16)
        wdb[...] = wd_ref[...].astype(BF16)

    @pl.when(i < n_used)
    def _():
        base = _gather_wait(MOE_TILE, xt_hbm, xbuf, slot, sem)
        x = _tiles_to_rows(xbuf, base, MOE_TILE).astype(BF16)
        hg = jnp.dot(x, wgb[...], preferred_element_type=F32)
        hu = jnp.dot(x, wub[...], preferred_element_type=F32)
        hidden = (_silu(hg) * hu).astype(BF16)
        _rows_to_tiles(o_ref, jnp.dot(hidden, wdb[...], preferred_element_type=F32))

    @pl.when(i >= n_used)
    def _():
        o_ref[...] = jnp.zeros_like(o_ref)


def _expert_ffn(xt, token_of_pos, tile_expert, n_used, e_gate, e_up, e_down):
    nt = token_of_pos.shape[0] // MOE_TILE
    idx = token_of_pos.reshape(nt, 1, MOE_TILE)
    tile_rows = MOE_TILE * SUBLANES
    smem = lambda imap: pl.BlockSpec((None, 1, MOE_TILE), imap, memory_space=pltpu.SMEM)
    grid_spec = pltpu.PrefetchScalarGridSpec(
        num_scalar_prefetch=2,
        grid=(nt,),
        in_specs=[
            smem(lambda i, te, nu: (i, 0, 0)),
            smem(lambda i, te, nu: (jnp.minimum(i + 1, nt - 1), 0, 0)),
            pl.BlockSpec(memory_space=pl.ANY),
            pl.BlockSpec((None, D_MODEL, D_EXPERT), lambda i, te, nu: (te[i], 0, 0)),
            pl.BlockSpec((None, D_MODEL, D_EXPERT), lambda i, te, nu: (te[i], 0, 0)),
            pl.BlockSpec((None, D_EXPERT, D_MODEL), lambda i, te, nu: (te[i], 0, 0)),
        ],
        out_specs=pl.BlockSpec((tile_rows, LANES), lambda i, te, nu: (i, 0)),
        scratch_shapes=[pltpu.VMEM((2 * tile_rows, LANES), F32),
                        pltpu.VMEM((D_MODEL, D_EXPERT), BF16), pltpu.VMEM((D_MODEL, D_EXPERT), BF16),
                        pltpu.VMEM((D_EXPERT, D_MODEL), BF16), pltpu.SemaphoreType.DMA((2,))],
    )
    return pl.pallas_call(
        _expert_kernel,
        grid_spec=grid_spec,
        out_shape=jax.ShapeDtypeStruct((nt * tile_rows, LANES), F32),
        compiler_params=_cparams(("arbitrary",)),
        name="expert_ffn",
    )(tile_expert, n_used, idx, idx, xt, e_gate, e_up, e_down)


def _combine_kernel(pc_ref, pn_ref, x_ref, rw_ref, ot_hbm, *refs, tm, n_first_tiles):
    i = pl.program_id(0)
    nt = pl.num_programs(0)
    slot = lax.rem(i, 2)
    n = TOP_K * tm
    obuf, sem = refs[-2:]

    @pl.when(i == 0)
    def _():
        _gather_tiles(pc_ref, n, ot_hbm, obuf, 0, sem)

    @pl.when(i + 1 < nt)
    def _():
        _gather_tiles(pn_ref, n, ot_hbm, obuf, 1 - slot, sem)

    base = _gather_wait(n, ot_hbm, obuf, slot, sem)
    rw = rw_ref[...]
    o0 = _tiles_to_rows(obuf, base, tm)
    o1 = _tiles_to_rows(obuf, base + tm * SUBLANES, tm)
    y = x_ref[...] + rw[:, 0:1] * o0 + rw[:, 1:2] * o1
    if n_first_tiles is None:
        refs[0][...] = y
    else:
        @pl.when(i < n_first_tiles)
        def _():
            refs[0][...] = y

        @pl.when(i >= n_first_tiles)
        def _():
            refs[1][...] = y


def _combine(x, out_tiles, pos, rw, split_rows=None):
    t = x.shape[0]
    rows = lambda w: pl.BlockSpec((tm, w), lambda i: (i, 0))
    if split_rows is None:
        tm = _pick(t, (256, 128, 64, 32, 16, 8))
        nft = None
        out_specs = rows(D_MODEL)
        out_shape = jax.ShapeDtypeStruct((t, D_MODEL), F32)
    else:
        tm = _pick(t - split_rows, (256, 128, 64, 32, 16, 8))
        assert split_rows % tm == 0
        nft = split_rows // tm
        out_specs = [pl.BlockSpec((tm, D_MODEL), lambda i: (jnp.minimum(i, nft - 1), 0)),
                     pl.BlockSpec((tm, D_MODEL), lambda i: (jnp.maximum(i - nft, 0), 0))]
        out_shape = [jax.ShapeDtypeStruct((split_rows, D_MODEL), F32),
                     jax.ShapeDtypeStruct((t - split_rows, D_MODEL), F32)]
    nt = t // tm
    n = TOP_K * tm
    idx = pos.reshape(nt, tm, TOP_K).transpose(0, 2, 1).reshape(nt, 1, n)
    kern = functools.partial(_combine_kernel, tm=tm, n_first_tiles=nft)
    smem = lambda imap: pl.BlockSpec((None, 1, n), imap, memory_space=pltpu.SMEM)
    return pl.pallas_call(
        kern,
        grid=(nt,),
        in_specs=[smem(lambda i: (i, 0, 0)), smem(lambda i: (jnp.minimum(i + 1, nt - 1), 0, 0)),
                  rows(D_MODEL), rows(LANES), pl.BlockSpec(memory_space=pl.ANY)],
        out_specs=out_specs,
        out_shape=out_shape,
        scratch_shapes=[pltpu.VMEM((2 * n * SUBLANES, LANES), F32), pltpu.SemaphoreType.DMA((2,))],
        compiler_params=_cparams(("arbitrary",)),
        name="moe_combine",
    )(idx, idx, x, rw, out_tiles)


def _route_tables(eid, rank, counts):
    t = eid.shape[0]
    padded = ((counts + MOE_TILE - 1) // MOE_TILE) * MOE_TILE
    ends = jnp.cumsum(padded)
    starts = ends - padded
    experts = jnp.arange(N_EXPERTS, dtype=I32)
    pos = jnp.sum(jnp.where(eid[:, :, None] == experts, starts, 0), axis=-1) + rank
    np_rows = -(-(TOP_K * t + N_EXPERTS * MOE_TILE) // MOE_TILE) * MOE_TILE
    token_of_pos = jnp.zeros((np_rows,), I32).at[pos.reshape(-1)].set(jnp.arange(TOP_K * t, dtype=I32) // TOP_K)
    tile_start = jnp.arange(np_rows // MOE_TILE, dtype=I32) * MOE_TILE
    tile_expert = jnp.minimum(jnp.sum(tile_start[:, None] >= ends[None, :], axis=1), N_EXPERTS - 1).astype(I32)
    n_used = (ends[-1] // MOE_TILE).astype(I32).reshape(1)
    return pos.astype(I32), token_of_pos, tile_expert, n_used


def _moe(y, xt, ri, rw, e_gate, e_up, e_down, split_rows):
    rank, counts = _rank(ri)
    pos, token_of_pos, tile_expert, n_used = _route_tables(ri[:, :TOP_K], rank[:, :TOP_K], counts[0, :N_EXPERTS])
    out_tiles = _expert_ffn(xt, token_of_pos, tile_expert, n_used, e_gate, e_up, e_down)
    return _combine(y, out_tiles, pos, rw, split_rows)


def _in_weights(w_in):
    half = C_ROPE // 2
    zeros = lambda n: jnp.zeros((D_MODEL, n), F32)
    c0 = A_WIDTH + B_WIDTH
    kr = w_in[:, c0 + C_Q_LORA + C_KV_LORA:c0 + C_WIDTH]
    kr_rot = jnp.concatenate([-kr[:, half:], kr[:, :half]], axis=1)
    tail = LANES - C_NOPE - C_ROPE
    cols = [
        w_in[:, :A_CONV_CH + A_V], w_in[:, A_CONV_CH + A_V:A_WIDTH], zeros(LANES - 2 * A_HEADS),
        w_in[:, A_WIDTH:c0],
        w_in[:, c0:c0 + C_Q_LORA + C_KV_LORA],
        zeros(C_NOPE), kr, zeros(tail),
        zeros(C_NOPE), kr_rot, zeros(tail),
    ]
    return jnp.concatenate(cols, axis=1)


def _router_weights(rg, rgb, re, reb):
    w = jnp.zeros((D_MODEL, LANES), F32).at[:, :N_GROUPS].set(rg).at[:, N_GROUPS:N_GROUPS + N_EXPERTS].set(re)
    b = jnp.zeros((1, LANES), F32).at[0, :N_GROUPS].set(rgb).at[0, N_GROUPS:N_GROUPS + N_EXPERTS].set(reb)
    return w, b


def _layer(xs, geom, st_p, st_s, cos, sin, wts, precise, split_out):
    (norm_mix, w_in, a_conv_w, a_A_log, a_dt_bias, a_norm_w,
     b_mu, b_w0, b_w2, b_a0, b_a2, b_g2, b_k_k, b_k_a, b_r_k, b_ln_w, b_ln_b,
     c_q_norm, c_kv_norm, c_w_uq, c_w_ukv, c_q_gain, c_k_gain,
     w_out, norm_ffn, router_group, router_group_bias, router_expert, router_expert_bias,
     e_gate, e_up, e_down) = wts
    bp, lp, bs, ls, past = geom
    tp = bp * lp

    pa, pb, pc = _in_proj(xs, norm_mix, _in_weights(w_in), precise)
    cw = _c_weights(c_q_norm, c_kv_norm, c_w_uq, c_w_ukv, c_q_gain, c_k_gain)
    q, k, v, ckv, krope = _c_prep(pc, cos, sin, cw, precise)

    a_args = (a_conv_w, a_A_log, a_dt_bias, a_norm_w)
    b_args = (b_mu, b_w0, b_w2, b_a0, b_a2, b_g2, b_k_k, b_k_a, b_r_k.reshape(-1), b_ln_w, b_ln_b)
    conv_p, delta_p, shift_p, wkv_p = st_p
    conv_s, delta_s, shift_s, wkv_s, ckv_past, krope_past = st_s

    oa_p, conv_np, delta_np = _mixer_a(pa, 0, bp, lp, precise, conv_p, delta_p, *a_args)
    oa_s, conv_ns, delta_ns = _mixer_a(pa, tp, bs, ls, precise, conv_s, delta_s, *a_args)
    ob_p, shift_np, wkv_np = _mixer_b(pb, 0, bp, lp, precise, shift_p, wkv_p, *b_args)
    ob_s, shift_ns, wkv_ns = _mixer_b(pb, tp, bs, ls, precise, shift_s, wkv_s, *b_args)

    oc_p = _attention(q, k, v, 0, bp, lp, precise)
    krp_past = jnp.pad(krope_past.reshape(bs * past, C_ROPE), ((0, 0), (C_NOPE, LANES - C_NOPE - C_ROPE)))
    k_cache, v_cache = _kv_expand_call(ckv_past.reshape(bs * past, C_KV_LORA), krp_past, cw, precise)
    oc_s = _attention(q, k, v, tp, bs, ls, precise, k_cache, v_cache)

    w_router, b_router = _router_weights(router_group, router_group_bias, router_expert, router_expert_bias)
    y, xt, ri, rw = _out_proj(xs, (oa_p, ob_p, oc_p), (oa_s, ob_s, oc_s), w_out, norm_ffn,
                              w_router, b_router, precise)
    x_new = _moe(y, xt, ri, rw, e_gate, e_up, e_down, tp if split_out else None)

    new_p = (conv_np, delta_np, shift_np, wkv_np, ckv[:tp].reshape(bp, lp, C_KV_LORA),
             krope[:tp].reshape(bp, lp, C_ROPE))
    new_s = (conv_ns, delta_ns, shift_ns, wkv_ns, ckv[tp:].reshape(bs, ls, C_KV_LORA),
             krope[tp:].reshape(bs, ls, C_ROPE))
    return x_new, new_p, new_s


def _forward(x_prompt, x_sample, cache_c_kv, cache_k_rope, state_conv_a, state_delta_a, state_shift_b,
             state_wkv_b, weights):
    bp, lp, _ = x_prompt.shape
    bs, ls, _ = x_sample.shape
    depth = cache_c_kv.shape[0]
    past = cache_c_kv.shape[2]
    geom = (bp, lp, bs, ls, past)
    xs = [x_prompt.reshape(bp * lp, D_MODEL), x_sample.reshape(bs * ls, D_MODEL)]
    pos = jnp.concatenate([jnp.tile(jnp.arange(lp), bp), jnp.tile(past + jnp.arange(ls), bs)])
    cos, sin = _rope_tables(pos)
    zeros = lambda *s: jnp.zeros(s, F32)
    st_p = (zeros(bp, A_CONV - 1, A_CONV_CH), zeros(bp, A_HEADS, A_DK, A_DV), zeros(bp, 1, B_WIDTH),
            zeros(bp, B_HEADS, B_N, B_N))
    news_p, news_s = [], []
    for l in range(depth):
        st_s = (state_conv_a[l], state_delta_a[l], state_shift_b[l], state_wkv_b[l], cache_c_kv[l], cache_k_rope[l])
        last = l == depth - 1
        x, new_p, new_s = _layer(xs, geom, st_p, st_s, cos, sin, [w[l] for w in weights], precise=not last,
                                 split_out=last)
        xs = x if last else [x]
        news_p.append(new_p)
        news_s.append(new_s)
    stack = lambda news, i: jnp.stack([n[i] for n in news])
    y_prompt = xs[0].reshape(bp, lp, D_MODEL)
    y_sample = xs[1].reshape(bs, ls, D_MODEL)
    p_conv, p_delta, p_shift, p_wkv, p_ckv, p_krope = (stack(news_p, i) for i in range(6))
    s_conv, s_delta, s_shift, s_wkv, s_ckv, s_krope = (stack(news_s, i) for i in range(6))
    return (y_prompt, y_sample, p_ckv, p_krope, p_conv, p_delta, p_shift, p_wkv,
            s_ckv, s_krope, s_conv, s_delta, s_shift, s_wkv)


def kernel(x_prompt, x_sample, cache_c_kv, cache_k_rope, state_conv_a, state_delta_a, state_shift_b, state_wkv_b,
           norm_mix, w_in, a_conv_w, a_A_log, a_dt_bias, a_norm_w,
           b_mu, b_w0, b_w2, b_a0, b_a2, b_g2, b_k_k, b_k_a, b_r_k, b_ln_w, b_ln_b,
           c_q_norm, c_kv_norm, c_w_uq, c_w_ukv, c_q_gain, c_k_gain,
           w_out, norm_ffn, router_group, router_group_bias, router_expert, router_expert_bias,
           e_gate, e_up, e_down):
    weights = (norm_mix, w_in, a_conv_w, a_A_log, a_dt_bias, a_norm_w,
               b_mu, b_w0, b_w2, b_a0, b_a2, b_g2, b_k_k, b_k_a, b_r_k, b_ln_w, b_ln_b,
               c_q_norm, c_kv_norm, c_w_uq, c_w_ukv, c_q_gain, c_k_gain,
               w_out, norm_ffn, router_group, router_group_bias, router_expert, router_expert_bias,
               e_gate, e_up, e_down)
    return _forward(x_prompt, x_sample, cache_c_kv, cache_k_rope, state_conv_a, state_delta_a, state_shift_b,
                    state_wkv_b, weights)
```

```python
import functools

import jax
import jax.numpy as jnp
from jax import lax
from jax.experimental import pallas as pl
from jax.experimental.pallas import tpu as pltpu

F32 = jnp.float32
BF16 = jnp.bfloat16
I32 = jnp.int32

D_MODEL = 1024
DEPTH = 2
CHUNK = 64
EPS = 1e-6
LANES = 128
SUBLANES = 8
ROW_TILES = D_MODEL // LANES

A_HEADS, A_DK, A_DV, A_CONV = 6, 64, 64, 4
A_QK = A_HEADS * A_DK
A_V = A_HEADS * A_DV
A_CONV_CH = 2 * A_QK + A_V
A_WIDTH = A_CONV_CH + A_V + 2 * A_HEADS
A_SEG = A_CONV_CH + A_V + LANES

B_HEADS, B_N = 6, 64
B_C = B_HEADS * B_N
B_W_LORA, B_A_LORA, B_G_LORA = 64, 64, 128
B_WIDTH = 3 * B_C + B_W_LORA + B_A_LORA + B_G_LORA
B_GN_EPS = 64e-5

C_HEADS, C_NOPE, C_ROPE, C_VDIM = 4, 64, 32, 64
C_QK = C_NOPE + C_ROPE
C_Q_LORA, C_KV_LORA = 256, 128
C_WIDTH = C_Q_LORA + C_KV_LORA + C_ROPE
C_SEG = C_Q_LORA + C_KV_LORA + 2 * LANES
C_HPAD = LANES
ROPE_THETA = 10000.0

P_TOTAL = A_WIDTH + B_WIDTH + C_WIDTH
D_MIX = A_V + B_C + C_HEADS * C_VDIM

N_GROUPS, EXPERTS_PER_GROUP = 4, 8
N_EXPERTS = N_GROUPS * EXPERTS_PER_GROUP
TOP_K = 2
D_EXPERT = 256
MOE_TILE = 256
MIXER_CHUNKS_PER_STEP = 4
ATTN_KEY_BLOCK = 512

VMEM_LIMIT = 48 * 1024 * 1024

_NN = (((1,), (0,)), ((), ()))
_NT = (((1,), (1,)), ((), ()))
_TN = (((0,), (0,)), ((), ()))


def _pick(n, prefs):
    for p in prefs:
        if n % p == 0:
            return p
    raise ValueError(f"no tile for {n} in {prefs}")


def _cparams(sem):
    return pltpu.CompilerParams(dimension_semantics=sem, vmem_limit_bytes=VMEM_LIMIT)


def _bdot(a, b, dims=_NN):
    return lax.dot_general(a.astype(BF16), b.astype(BF16), dims, preferred_element_type=F32)


def _split2(a):
    hi = a.astype(BF16)
    lo = (a - hi.astype(F32)).astype(BF16)
    return hi, lo


def _dot3(a, b, dims=_NN):
    ah, al = _split2(a)
    bh, bl = _split2(b)
    f = lambda x, y: lax.dot_general(x, y, dims, preferred_element_type=F32)
    return f(ah, bh) + (f(ah, bl) + f(al, bh))


def _mdot(precise):
    return _dot3 if precise else _bdot


def _cumsum_rows(x, ltri):
    h = x.astype(BF16)
    r = x - h.astype(F32)
    m = r.astype(BF16)
    l = (r - m.astype(F32)).astype(BF16)
    d = lambda y: jnp.dot(ltri, y, preferred_element_type=F32)
    return d(h) + (d(m) + d(l))


def _tri_masks(c):
    row = lax.broadcasted_iota(I32, (c, c), 0)
    col = lax.broadcasted_iota(I32, (c, c), 1)
    return row >= col, row > col, (row == col).astype(F32)


def _chunk_tril(rows, c):
    shift = c.bit_length() - 1
    row = lax.broadcasted_iota(I32, (rows, rows), 0)
    col = lax.broadcasted_iota(I32, (rows, rows), 1)
    same = jnp.right_shift(row, shift) == jnp.right_shift(col, shift)
    return (same & (row >= col)).astype(BF16)


def _neumann_inv_many(ns, eye):
    c = eye.shape[0]
    keep_t = lax.broadcasted_iota(I32, (c, 2 * c), 1) >= c
    pts = [jnp.concatenate([n, eye], axis=1) for n in ns]
    m = 1
    while m < c:
        rs = [_dot3(pt[:, :c], pt) for pt in pts]
        pts = [r + jnp.where(keep_t, pt, 0.0) for r, pt in zip(rs, pts)]
        m *= 2
    return [pt[:, c:] for pt in pts]


def _softplus(x):
    return jnp.maximum(x, 0.0) + jnp.log1p(jnp.exp(-jnp.abs(x)))


def _silu(x):
    return x * jax.nn.sigmoid(x)


def _rms(x, eps=EPS):
    return x * lax.rsqrt(jnp.mean(x * x, axis=-1, keepdims=True) + eps)


def _l2n(x):
    return x * lax.rsqrt(jnp.sum(x * x, axis=-1, keepdims=True) + 1e-6)


def _rows_to_tiles(ref, val):
    n = val.shape[0]
    for j in range(ROW_TILES):
        ref[pl.ds(j, n, stride=SUBLANES), :] = val[:, j * LANES:(j + 1) * LANES]


def _tiles_to_rows(ref, start, n):
    return jnp.concatenate([ref[pl.ds(start + j, n, stride=SUBLANES), :] for j in range(ROW_TILES)], axis=1)


def _in_proj_kernel(*refs, precise, n_first_tiles):
    if n_first_tiles is None:
        x = refs[0][...]
        g_ref, *refs = refs[1:]
    else:
        x = jnp.where(pl.program_id(0) < n_first_tiles, refs[0][...], refs[1][...])
        g_ref, *refs = refs[2:]
    if precise:
        wh_ref, wl_ref, pa_ref, pb_ref, pc_ref = refs
    else:
        wh_ref, pa_ref, pb_ref, pc_ref = refs
    xn = _rms(x) * g_ref[...]
    xh = xn.astype(BF16)
    if precise:
        xl = (xn - xh.astype(F32)).astype(BF16)
    lo = 0
    for out_ref, width in ((pa_ref, A_SEG), (pb_ref, B_WIDTH), (pc_ref, C_SEG)):
        acc = jnp.dot(xh, wh_ref[:, lo:lo + width], preferred_element_type=F32)
        if precise:
            acc = acc + (jnp.dot(xh, wl_ref[:, lo:lo + width], preferred_element_type=F32)
                         + jnp.dot(xl, wh_ref[:, lo:lo + width], preferred_element_type=F32))
        out_ref[...] = acc
        lo += width


def _split_kernel(w_ref, hi_ref, lo_ref):
    hi, lo = _split2(w_ref[...])
    hi_ref[...] = hi
    lo_ref[...] = lo


def _split_hi_lo(w):
    r, c = w.shape
    tc = _pick(c, (512, 256, 128))
    spec = pl.BlockSpec((r, tc), lambda j: (0, j))
    return pl.pallas_call(
        _split_kernel,
        grid=(c // tc,),
        in_specs=[spec],
        out_specs=[spec, spec],
        out_shape=[jax.ShapeDtypeStruct((r, c), BF16), jax.ShapeDtypeStruct((r, c), BF16)],
        compiler_params=_cparams(("arbitrary",)),
        name="split_hi_lo",
    )(w)


def _in_proj(xs, g, w, precise):
    prefs = (256, 128, 64, 32, 16, 8) if precise else (512, 256, 128, 64, 32, 16, 8)
    t = sum(x.shape[0] for x in xs)
    if len(xs) == 1:
        tm = _pick(t, prefs)
        nft = None
        x_specs = [pl.BlockSpec((tm, D_MODEL), lambda i: (i, 0))]
    else:
        tm = _pick(xs[1].shape[0], prefs)
        assert xs[0].shape[0] % tm == 0
        nft = xs[0].shape[0] // tm
        x_specs = [pl.BlockSpec((tm, D_MODEL), lambda i: (jnp.minimum(i, nft - 1), 0)),
                   pl.BlockSpec((tm, D_MODEL), lambda i: (jnp.maximum(i - nft, 0), 0))]
    wtot = A_SEG + B_WIDTH + C_SEG
    wspec = pl.BlockSpec((D_MODEL, wtot), lambda i: (0, 0))
    w_args = _split_hi_lo(w) if precise else [w.astype(BF16)]
    return pl.pallas_call(
        functools.partial(_in_proj_kernel, precise=precise, n_first_tiles=nft),
        grid=(t // tm,),
        in_specs=x_specs + [pl.BlockSpec((1, D_MODEL), lambda i: (0, 0))] + [wspec] * len(w_args),
        out_specs=[
            pl.BlockSpec((tm, A_SEG), lambda i: (i, 0)),
            pl.BlockSpec((tm, B_WIDTH), lambda i: (i, 0)),
            pl.BlockSpec((tm, C_SEG), lambda i: (i, 0)),
        ],
        out_shape=[
            jax.ShapeDtypeStruct((t, A_SEG), F32),
            jax.ShapeDtypeStruct((t, B_WIDTH), F32),
            jax.ShapeDtypeStruct((t, C_SEG), F32),
        ],
        compiler_params=_cparams(("arbitrary",)),
        name="in_proj",
    )(*xs, g.reshape(1, D_MODEL), *w_args)


def _mixer_a_kernel(pa_ref, cprev_ref, s0_ref, cw_ref, alog_ref, dtb_ref, nw_ref,
                    oa_ref, cnew_ref, snew_ref, tail_ref, s_ref, *, c, nch, n_steps, precise):
    dot = _mdot(precise)
    step = pl.program_id(1)
    rows = c * nch

    @pl.when(step == 0)
    def _():
        tail_ref[...] = jnp.zeros_like(tail_ref)
        tail_ref[SUBLANES - (A_CONV - 1):SUBLANES, :] = cprev_ref[...]
        s_ref[...] = s0_ref[...]

    x = pa_ref[...]
    qkv = x[:, :A_CONV_CH]
    z = x[:, A_CONV_CH:A_CONV_CH + A_V]
    ab = x[:, A_CONV_CH + A_V:A_SEG]

    xp = jnp.concatenate([tail_ref[...], qkv], axis=0)
    cw = cw_ref[...]
    y = qkv * cw[A_CONV - 1:A_CONV]
    for j in range(A_CONV - 1):
        o = SUBLANES - (A_CONV - 1) + j
        y = y + xp[o:o + rows] * cw[j:j + 1]
    tail_ref[...] = qkv[rows - SUBLANES:rows]

    @pl.when(step == n_steps - 1)
    def _():
        cnew_ref[...] = qkv[rows - (A_CONV - 1):rows]

    act = _silu(y)
    g_all = -jnp.exp(alog_ref[...]) * _softplus(ab + dtb_ref[...])
    beta_all = jax.nn.sigmoid(ab)
    gc_all = _cumsum_rows(g_all, _chunk_tril(rows, c))
    gc_t = gc_all.T
    tril, strict, eye = _tri_masks(c)
    nw = nw_ref[...]

    qn = [_l2n(act[:, h * A_DK:(h + 1) * A_DK]) * (A_DK ** -0.5) for h in range(A_HEADS)]
    kn = [_l2n(act[:, A_QK + h * A_DK:A_QK + (h + 1) * A_DK]) for h in range(A_HEADS)]
    vv = [act[:, 2 * A_QK + h * A_DV:2 * A_QK + (h + 1) * A_DV] for h in range(A_HEADS)]

    items = [(ci, h) for ci in range(nch) for h in range(A_HEADS)]
    q_i, k_i, kb_i, vb_i, dec_i, gcc_i, gl_i = [], [], [], [], [], [], []
    for ci, h in items:
        r0 = ci * c
        beta = beta_all[r0:r0 + c, A_HEADS + h:A_HEADS + h + 1]
        gcc = gc_all[r0:r0 + c, h:h + 1]
        gcr = gc_t[h:h + 1, r0:r0 + c]
        k = kn[h][r0:r0 + c]
        q_i.append(qn[h][r0:r0 + c])
        k_i.append(k)
        kb_i.append(k * beta)
        vb_i.append(vv[h][r0:r0 + c] * beta)
        dec_i.append(jnp.exp(jnp.where(tril, gcc - gcr, -1e30)))
        gcc_i.append(gcc)
        gl_i.append(gc_all[r0 + c - 1:r0 + c, h:h + 1])
    a_i = [jnp.where(strict, dot(kb, k, _NT) * dec, 0.0) for kb, k, dec in zip(kb_i, k_i, dec_i)]
    t_i = _neumann_inv_many([-a for a in a_i], eye)
    egc_i = [jnp.exp(g) for g in gcc_i]
    sol_i = [_dot3(t, jnp.concatenate([vb, kb * e], axis=1)) for t, vb, kb, e in zip(t_i, vb_i, kb_i, egc_i)]
    attn_i = [jnp.where(tril, dot(q, k, _NT) * dec, 0.0) for q, k, dec in zip(q_i, k_i, dec_i)]
    kq_i = [jnp.concatenate([sol[:, A_DV:], q * e], axis=0) for sol, q, e in zip(sol_i, q_i, egc_i)]
    kg_i = [k * jnp.exp(gl - g) for k, gl, g in zip(k_i, gl_i, gcc_i)]

    s = [s_ref[h] for h in range(A_HEADS)]
    out_rows = []
    for ci in range(nch):
        ids = [ci * A_HEADS + h for h in range(A_HEADS)]
        ks = [dot(kq_i[i], s[h]) for h, i in enumerate(ids)]
        u = [sol_i[i][:, :A_DV] - ks[h][:c] for h, i in enumerate(ids)]
        o = [ks[h][c:] + dot(attn_i[i], u[h]) for h, i in enumerate(ids)]
        s = [s[h] * jnp.exp(gl_i[i]) + dot(kg_i[i], u[h], _TN) for h, i in enumerate(ids)]
        r0 = ci * c
        outs = [_rms(o[h]) * nw * _silu(z[r0:r0 + c, h * A_DV:(h + 1) * A_DV]) for h in range(A_HEADS)]
        out_rows.append(jnp.concatenate(outs, axis=1))
    for h in range(A_HEADS):
        s_ref[h] = s[h]
    oa_ref[...] = jnp.concatenate(out_rows, axis=0).astype(oa_ref.dtype)

    @pl.when(step == n_steps - 1)
    def _():
        snew_ref[...] = s_ref[...]


def _row(v, width=LANES):
    v = v.reshape(1, -1).astype(F32)
    return jnp.pad(v, ((0, 0), (0, width - v.shape[1])))


def _mixer_geometry(row0, seq):
    c = min(CHUNK, seq)
    nch = _pick(seq // c, (MIXER_CHUNKS_PER_STEP, 2, 1))
    rows = c * nch
    assert row0 % rows == 0
    return c, nch, rows, seq // rows, row0 // rows


def _mixer_a(pa, row0, bsz, seq, precise, conv_prev, s0, conv_w, a_log, dt_bias, norm_w):
    c, nch, rows, n_steps, blk0 = _mixer_geometry(row0, seq)
    kern = functools.partial(_mixer_a_kernel, c=c, nch=nch, n_steps=n_steps, precise=precise)
    full = lambda shape: pl.BlockSpec(shape, lambda b, n: (0,) * len(shape))
    return pl.pallas_call(
        kern,
        grid=(bsz, n_steps),
        in_specs=[
            pl.BlockSpec((rows, A_SEG), lambda b, n: (blk0 + b * n_steps + n, 0)),
            pl.BlockSpec((None, A_CONV - 1, A_CONV_CH), lambda b, n: (b, 0, 0)),
            pl.BlockSpec((None, A_HEADS, A_DK, A_DV), lambda b, n: (b, 0, 0, 0)),
            full((A_CONV, A_CONV_CH)),
            full((1, LANES)),
            full((1, LANES)),
            full((1, A_DV)),
        ],
        out_specs=[
            pl.BlockSpec((rows, A_V), lambda b, n: (b * n_steps + n, 0)),
            pl.BlockSpec((None, A_CONV - 1, A_CONV_CH), lambda b, n: (b, 0, 0)),
            pl.BlockSpec((None, A_HEADS, A_DK, A_DV), lambda b, n: (b, 0, 0, 0)),
        ],
        out_shape=[
            jax.ShapeDtypeStruct((bsz * seq, A_V), F32 if precise else BF16),
            jax.ShapeDtypeStruct((bsz, A_CONV - 1, A_CONV_CH), F32),
            jax.ShapeDtypeStruct((bsz, A_HEADS, A_DK, A_DV), F32),
        ],
        scratch_shapes=[pltpu.VMEM((SUBLANES, A_CONV_CH), F32), pltpu.VMEM((A_HEADS, A_DK, A_DV), F32)],
        compiler_params=_cparams(("arbitrary", "arbitrary")),
        name="mixer_a",
    )(pa, conv_prev, s0, conv_w, _row(a_log), _row(dt_bias), norm_w.reshape(1, A_DV))


def _mixer_b_kernel(pb_ref, sprev_ref, s0_ref, mu_ref, w0_ref, w2_ref, a0_ref, a2_ref, g2_ref,
                    kk_ref, ka_ref, rk_ref, lnw_ref, lnb_ref,
                    ob_ref, shnew_ref, snew_ref, last_ref, s_ref, *, c, nch, n_steps, precise):
    dot = _mdot(precise)
    step = pl.program_id(1)
    rows = c * nch

    @pl.when(step == 0)
    def _():
        last_ref[...] = sprev_ref[...]
        s_ref[...] = s0_ref[...]

    p = pb_ref[...]
    rowi = lax.broadcasted_iota(I32, (rows, 1), 0)
    prev = jnp.where(rowi == 0, last_ref[...], pltpu.roll(p, 1, 0))
    last_ref[...] = p[rows - 1:rows]

    @pl.when(step == n_steps - 1)
    def _():
        shnew_ref[...] = p[rows - 1:rows]

    xs = p + (prev - p) * mu_ref[...]
    r_all = xs[:, 0:B_C]
    k_all = xs[:, B_C:2 * B_C]
    v_all = xs[:, 2 * B_C:3 * B_C]
    xwa = xs[:, 3 * B_C:3 * B_C + B_W_LORA + B_A_LORA]
    xg = xs[:, 3 * B_C + B_W_LORA + B_A_LORA:B_WIDTH]

    w_log = -_softplus(-(w0_ref[...] + dot(jnp.tanh(xwa), w2_ref[...]))) - 0.5
    lw = -jnp.exp(w_log)
    rate = jax.nn.sigmoid(a0_ref[...] + dot(xwa, a2_ref[...]))
    gate = dot(jax.nn.sigmoid(xg), g2_ref[...])
    kkr = k_all * kk_ref[...]
    k2_all = k_all * (1.0 + (rate - 1.0) * ka_ref[...])

    cl = _cumsum_rows(lw, _chunk_tril(rows, c))
    e_cl = jnp.exp(cl)
    e_neg = jnp.exp(-cl)
    at_all = jnp.exp(cl - lw)
    bt_all = rate * e_neg
    kt_all = k2_all * e_neg
    rt_all = r_all * e_cl
    tril, strict, eye = _tri_masks(c)
    rk = rk_ref[...]
    lnw = lnw_ref[...]
    lnb = lnb_ref[...]
    kkn = [_l2n(kkr[:, h * B_N:(h + 1) * B_N]) for h in range(B_HEADS)]

    items = [(ci, h) for ci in range(nch) for h in range(B_HEADS)]
    at_i, bt_i, kt_i, rt_i, v_i = [], [], [], [], []
    for ci, h in items:
        rs = slice(ci * c, (ci + 1) * c)
        ls = slice(h * B_N, (h + 1) * B_N)
        kk = kkn[h][rs]
        at_i.append(-kk * at_all[rs, ls])
        bt_i.append(kk * bt_all[rs, ls])
        kt_i.append(kt_all[rs, ls])
        rt_i.append(rt_all[rs, ls])
        v_i.append(v_all[rs, ls])
    bk_i = [jnp.concatenate([b, k], axis=0) for b, k in zip(bt_i, kt_i)]
    ar_i = [jnp.concatenate([a, r], axis=0) for a, r in zip(at_i, rt_i)]
    row2 = lax.broadcasted_iota(I32, (c, 2 * c), 0)
    col2 = jnp.bitwise_and(lax.broadcasted_iota(I32, (c, 2 * c), 1), c - 1)
    strict2 = row2 > col2
    tril2 = row2 >= col2
    g_a = [jnp.where(strict2, _dot3(a, bk, _NT), 0.0) for a, bk in zip(at_i, bk_i)]
    g_r = [jnp.where(tril2, dot(r, bk, _NT), 0.0) for r, bk in zip(rt_i, bk_i)]
    t_i = _neumann_inv_many([g[:, :c] for g in g_a], eye)
    mv_i = [dot(g[:, c:], v) for g, v in zip(g_a, v_i)]

    s = [s_ref[h] for h in range(B_HEADS)]
    out_rows = []
    for ci in range(nch):
        ids = [ci * B_HEADS + h for h in range(B_HEADS)]
        rs = slice(ci * c, (ci + 1) * c)
        ars = [dot(ar_i[i], s[h], _NT) for h, i in enumerate(ids)]
        u = [_dot3(t_i[i], ars[h][:c] + mv_i[i]) for h, i in enumerate(ids)]
        uv = [jnp.concatenate([u[h], v_i[i]], axis=0) for h, i in enumerate(ids)]
        o = [ars[h][c:] + dot(g_r[i], uv[h]) for h, i in enumerate(ids)]
        e_last = e_cl[(ci + 1) * c - 1:(ci + 1) * c]
        s = [(s[h] + dot(uv[h], bk_i[i], _TN)) * e_last[:, h * B_N:(h + 1) * B_N] for h, i in enumerate(ids)]
        outs = []
        for h in range(B_HEADS):
            ls = slice(h * B_N, (h + 1) * B_N)
            mean = jnp.mean(o[h], axis=-1, keepdims=True)
            var = jnp.mean(jnp.square(o[h] - mean), axis=-1, keepdims=True)
            gn = (o[h] - mean) * lax.rsqrt(var + B_GN_EPS) * lnw[:, ls] + lnb[:, ls]
            bonus = jnp.sum(r_all[rs, ls] * k2_all[rs, ls] * rk[:, ls], axis=-1, keepdims=True) * v_all[rs, ls]
            outs.append((gn + bonus) * gate[rs, ls])
        out_rows.append(jnp.concatenate(outs, axis=1))
    for h in range(B_HEADS):
        s_ref[h] = s[h]
    ob_ref[...] = jnp.concatenate(out_rows, axis=0).astype(ob_ref.dtype)

    @pl.when(step == n_steps - 1)
    def _():
        snew_ref[...] = s_ref[...]


def _mixer_b(pb, row0, bsz, seq, precise, shift_prev, s0, mu, w0, w2, a0, a2, g2, k_k, k_a, r_k, ln_w, ln_b):
    c, nch, rows, n_steps, blk0 = _mixer_geometry(row0, seq)
    kern = functools.partial(_mixer_b_kernel, c=c, nch=nch, n_steps=n_steps, precise=precise)
    full = lambda shape: pl.BlockSpec(shape, lambda b, n: (0,) * len(shape))
    lora = B_W_LORA + B_A_LORA
    w2p = jnp.zeros((lora, B_C), F32).at[:B_W_LORA].set(w2)
    a2p = jnp.zeros((lora, B_C), F32).at[B_W_LORA:].set(a2)
    vec = lambda v: v.reshape(1, -1).astype(F32)
    return pl.pallas_call(
        kern,
        grid=(bsz, n_steps),
        in_specs=[
            pl.BlockSpec((rows, B_WIDTH), lambda b, n: (blk0 + b * n_steps + n, 0)),
            pl.BlockSpec((None, 1, B_WIDTH), lambda b, n: (b, 0, 0)),
            pl.BlockSpec((None, B_HEADS, B_N, B_N), lambda b, n: (b, 0, 0, 0)),
            full((1, B_WIDTH)),
            full((1, B_C)), full((lora, B_C)),
            full((1, B_C)), full((lora, B_C)),
            full((B_G_LORA, B_C)),
            full((1, B_C)), full((1, B_C)), full((1, B_C)), full((1, B_C)), full((1, B_C)),
        ],
        out_specs=[
            pl.BlockSpec((rows, B_C), lambda b, n: (b * n_steps + n, 0)),
            pl.BlockSpec((None, 1, B_WIDTH), lambda b, n: (b, 0, 0)),
            pl.BlockSpec((None, B_HEADS, B_N, B_N), lambda b, n: (b, 0, 0, 0)),
        ],
        out_shape=[
            jax.ShapeDtypeStruct((bsz * seq, B_C), F32 if precise else BF16),
            jax.ShapeDtypeStruct((bsz, 1, B_WIDTH), F32),
            jax.ShapeDtypeStruct((bsz, B_HEADS, B_N, B_N), F32),
        ],
        scratch_shapes=[pltpu.VMEM((1, B_WIDTH), F32), pltpu.VMEM((B_HEADS, B_N, B_N), F32)],
        compiler_params=_cparams(("arbitrary", "arbitrary")),
        name="mixer_b",
    )(pb, shift_prev, s0, vec(mu), vec(w0), w2p, vec(a0), a2p, g2,
      vec(k_k), vec(k_a), vec(r_k), vec(ln_w), vec(ln_b))


def _kv_expand(ckvn, krope, wuk_ref, wuv_ref, gk_ref, k_ref, v_ref, dot):
    k0 = dot(ckvn, wuk_ref[...])
    gk = gk_ref[...]
    for h in range(C_HEADS):
        kh = k0[:, h * C_HPAD:(h + 1) * C_HPAD] + krope
        ss = jnp.sum(kh * kh, axis=-1, keepdims=True) * (1.0 / C_QK)
        k_ref[:, h * C_HPAD:(h + 1) * C_HPAD] = (kh * lax.rsqrt(ss + EPS) * gk).astype(k_ref.dtype)
    v_ref[...] = dot(ckvn, wuv_ref[...]).astype(v_ref.dtype)


def _c_prep_kernel(pc_ref, cos_ref, sin_ref, qn_ref, kvn_ref, wq_ref, wqr_ref, wuk_ref, wuv_ref, gq_ref, gk_ref,
                   q_ref, k_ref, v_ref, ckv_ref, kr_ref, *, precise):
    dot = _mdot(precise)
    pc = pc_ref[...]
    cq = pc[:, 0:C_Q_LORA]
    ckv_raw = pc[:, C_Q_LORA:C_Q_LORA + C_KV_LORA]
    krp = pc[:, C_Q_LORA + C_KV_LORA:C_Q_LORA + C_KV_LORA + LANES]
    krr = pc[:, C_Q_LORA + C_KV_LORA + LANES:C_SEG]
    cos = cos_ref[...]
    sin = sin_ref[...]
    lane = lax.broadcasted_iota(I32, cos.shape, 1)
    cosq = jnp.where(lane < C_NOPE, 1.0, cos)

    cqn = _rms(cq) * qn_ref[...]
    q0 = dot(cqn, wq_ref[...])
    q1 = dot(cqn, wqr_ref[...])
    gq = gq_ref[...] * (C_QK ** -0.5)
    for h in range(C_HEADS):
        sl = slice(h * C_HPAD, (h + 1) * C_HPAD)
        qh = q0[:, sl] * cosq + q1[:, sl] * sin
        ss = jnp.sum(qh * qh, axis=-1, keepdims=True) * (1.0 / C_QK)
        q_ref[:, sl] = (qh * lax.rsqrt(ss + EPS) * gq).astype(q_ref.dtype)

    ckvn = _rms(ckv_raw) * kvn_ref[...]
    ckv_ref[...] = ckvn
    krope = krp * cos + krr * sin
    kr_ref[...] = krope[:, C_NOPE:C_NOPE + C_ROPE]
    _kv_expand(ckvn, krope, wuk_ref, wuv_ref, gk_ref, k_ref, v_ref, dot)


def _kv_expand_kernel(ckv_ref, kr_ref, wuk_ref, wuv_ref, gk_ref, k_ref, v_ref, *, precise):
    _kv_expand(ckv_ref[...], kr_ref[...], wuk_ref, wuv_ref, gk_ref, k_ref, v_ref, _mdot(precise))


def _c_weights(q_norm, kv_norm, w_uq, w_ukv, q_gain, k_gain):
    half = C_ROPE // 2
    wq = jnp.zeros((C_Q_LORA, C_HEADS * C_HPAD), F32)
    wqr = jnp.zeros((C_Q_LORA, C_HEADS * C_HPAD), F32)
    wuk = jnp.zeros((C_KV_LORA, C_HEADS * C_HPAD), F32)
    wuv = jnp.zeros((C_KV_LORA, C_HEADS * C_VDIM), F32)
    for h in range(C_HEADS):
        wh = w_uq[:, h * C_QK:(h + 1) * C_QK]
        wq = wq.at[:, h * C_HPAD:h * C_HPAD + C_QK].set(wh)
        rot = jnp.concatenate([-wh[:, C_NOPE + half:], wh[:, C_NOPE:C_NOPE + half]], axis=1)
        wqr = wqr.at[:, h * C_HPAD + C_NOPE:h * C_HPAD + C_QK].set(rot)
        kvh = w_ukv[:, h * (C_NOPE + C_VDIM):(h + 1) * (C_NOPE + C_VDIM)]
        wuk = wuk.at[:, h * C_HPAD:h * C_HPAD + C_NOPE].set(kvh[:, :C_NOPE])
        wuv = wuv.at[:, h * C_VDIM:(h + 1) * C_VDIM].set(kvh[:, C_NOPE:])
    gain = lambda g: _row(jnp.concatenate([g[:C_NOPE], g[C_NOPE:], g[C_NOPE:]]))
    return (q_norm.reshape(1, -1), kv_norm.reshape(1, -1), wq, wqr, wuk, wuv, gain(q_gain), gain(k_gain))


def _rope_tables(pos):
    half = C_ROPE // 2
    inv = ROPE_THETA ** (-(jnp.arange(0, C_ROPE, 2, dtype=F32) / C_ROPE))
    ang = pos.astype(F32)[:, None] * inv[None, :]
    cos, sin = jnp.cos(ang), jnp.sin(ang)
    pad = lambda t: jnp.pad(jnp.concatenate([t, t], axis=1), ((0, 0), (C_NOPE, LANES - C_NOPE - 2 * half)))
    return pad(cos), pad(sin)


def _c_prep(pc, cos, sin, cw, precise):
    t = pc.shape[0]
    tm = _pick(t, (512, 256, 128, 64, 32, 16, 8))
    qn, kvn, wq, wqr, wuk, wuv, gq, gk = cw
    full = lambda a: pl.BlockSpec(a.shape, lambda i: (0,) * a.ndim)
    rows = lambda w: pl.BlockSpec((tm, w), lambda i: (i, 0))
    act = F32 if precise else BF16
    return pl.pallas_call(
        functools.partial(_c_prep_kernel, precise=precise),
        grid=(t // tm,),
        in_specs=[rows(C_SEG), rows(LANES), rows(LANES)] + [full(a) for a in (qn, kvn, wq, wqr, wuk, wuv, gq, gk)],
        out_specs=[rows(C_HEADS * C_HPAD), rows(C_HEADS * C_HPAD), rows(C_HEADS * C_VDIM), rows(C_KV_LORA),
                   rows(C_ROPE)],
        out_shape=[
            jax.ShapeDtypeStruct((t, C_HEADS * C_HPAD), act),
            jax.ShapeDtypeStruct((t, C_HEADS * C_HPAD), act),
            jax.ShapeDtypeStruct((t, C_HEADS * C_VDIM), act),
            jax.ShapeDtypeStruct((t, C_KV_LORA), F32),
            jax.ShapeDtypeStruct((t, C_ROPE), F32),
        ],
        compiler_params=_cparams(("arbitrary",)),
        name="c_prep",
    )(pc, cos, sin, qn, kvn, wq, wqr, wuk, wuv, gq, gk)


def _kv_expand_call(ckv, krp, cw, precise):
    t = ckv.shape[0]
    tm = _pick(t, (1024, 512, 256, 128, 64, 32, 16, 8))
    _, _, _, _, wuk, wuv, _, gk = cw
    full = lambda a: pl.BlockSpec(a.shape, lambda i: (0,) * a.ndim)
    rows = lambda w: pl.BlockSpec((tm, w), lambda i: (i, 0))
    act = F32 if precise else BF16
    return pl.pallas_call(
        functools.partial(_kv_expand_kernel, precise=precise),
        grid=(t // tm,),
        in_specs=[rows(C_KV_LORA), rows(LANES), full(wuk), full(wuv), full(gk)],
        out_specs=[rows(C_HEADS * C_HPAD), rows(C_HEADS * C_VDIM)],
        out_shape=[
            jax.ShapeDtypeStruct((t, C_HEADS * C_HPAD), act),
            jax.ShapeDtypeStruct((t, C_HEADS * C_VDIM), act),
        ],
        compiler_params=_cparams(("arbitrary",)),
        name="kv_expand",
    )(ckv, krp, wuk, wuv, gk)


def _attn_kernel(*refs, tq, nq, past, has_cache, precise):
    if has_cache:
        q_ref, kn_ref, vn_ref, kc_ref, vc_ref, o_ref = refs
    else:
        q_ref, kn_ref, vn_ref, o_ref = refs
    dot = _mdot(precise)
    qi = pl.program_id(1)
    shift = CHUNK.bit_length() - 1

    def attend(n_new):
        q = q_ref[...]
        q_chunk = jnp.right_shift(past + qi * tq + lax.broadcasted_iota(I32, (tq, 1), 0), shift)
        blocks = [(kn_ref, vn_ref, r0, min(ATTN_KEY_BLOCK, n_new - r0), past + r0)
                  for r0 in range(0, n_new, ATTN_KEY_BLOCK)]
        if has_cache:
            blocks += [(kc_ref, vc_ref, r0, min(ATTN_KEY_BLOCK, past - r0), r0) for r0 in range(0, past, ATTN_KEY_BLOCK)]
        qh = [q[:, h * C_HPAD:(h + 1) * C_HPAD] for h in range(C_HEADS)]
        state = [None] * C_HEADS
        for k_ref, v_ref, r0, w, p0 in blocks:
            kb = k_ref[r0:r0 + w, :]
            vb = v_ref[r0:r0 + w, :]
            vis = jnp.right_shift(p0 + lax.broadcasted_iota(I32, (1, w), 1), shift) <= q_chunk
            for h in range(C_HEADS):
                s = jnp.where(vis, dot(qh[h], kb[:, h * C_HPAD:(h + 1) * C_HPAD], _NT), -1e30)
                m_blk = jnp.max(s, axis=-1, keepdims=True)
                if state[h] is None:
                    m_new = m_blk
                    p = jnp.exp(s - m_new)
                    den = jnp.sum(p, axis=-1, keepdims=True)
                    acc = dot(p, vb[:, h * C_VDIM:(h + 1) * C_VDIM])
                else:
                    m, den, acc = state[h]
                    m_new = jnp.maximum(m, m_blk)
                    scale = jnp.exp(m - m_new)
                    p = jnp.exp(s - m_new)
                    den = den * scale + jnp.sum(p, axis=-1, keepdims=True)
                    acc = acc * scale + dot(p, vb[:, h * C_VDIM:(h + 1) * C_VDIM])
                state[h] = (m_new, den, acc)
        o_ref[...] = jnp.concatenate([acc / den for _, den, acc in state], axis=1).astype(o_ref.dtype)

    if nq == 1:
        attend(tq)
    else:
        for blk in range(nq):
            pl.when(qi == blk)(functools.partial(attend, (blk + 1) * tq))


def _attention(q, k, v, row0, bsz, seq, precise, k_cache=None, v_cache=None):
    has_cache = k_cache is not None
    past = k_cache.shape[0] // bsz if has_cache else 0
    tq = _pick(seq, (256, 128, 64, 32, 16, 8))
    assert tq % CHUNK == 0 or tq == seq
    nq = seq // tq
    kern = functools.partial(_attn_kernel, tq=tq, nq=nq, past=past, has_cache=has_cache, precise=precise)
    qw, vw = C_HEADS * C_HPAD, C_HEADS * C_VDIM
    in_specs = [
        pl.BlockSpec((tq, qw), lambda b, i: (row0 // tq + b * nq + i, 0)),
        pl.BlockSpec((seq, qw), lambda b, i: (row0 // seq + b, 0)),
        pl.BlockSpec((seq, vw), lambda b, i: (row0 // seq + b, 0)),
    ]
    args = [q, k, v]
    if has_cache:
        in_specs += [pl.BlockSpec((past, qw), lambda b, i: (b, 0)), pl.BlockSpec((past, vw), lambda b, i: (b, 0))]
        args += [k_cache, v_cache]
    return pl.pallas_call(
        kern,
        grid=(bsz, nq),
        in_specs=in_specs,
        out_specs=pl.BlockSpec((tq, vw), lambda b, i: (b * nq + i, 0)),
        out_shape=jax.ShapeDtypeStruct((bsz * seq, vw), F32 if precise else BF16),
        compiler_params=_cparams(("arbitrary", "arbitrary")),
        name="attention",
    )(*args)


def _out_proj_kernel(*refs, n_prompt_tiles, precise, split_x):
    dot = _mdot(precise)
    is_p = pl.program_id(0) < n_prompt_tiles
    if split_x:
        y = jnp.where(is_p, refs[0][...], refs[1][...])
        refs = refs[2:]
    else:
        y = refs[0][...]
        refs = refs[1:]
    (oap_ref, obp_ref, ocp_ref, oas_ref, obs_ref, ocs_ref, w_ref, g_ref, wr_ref, br_ref,
     y_ref, xt_ref, ri_ref, rw_ref) = refs
    oa = jnp.where(is_p, oap_ref[...], oas_ref[...])
    ob = jnp.where(is_p, obp_ref[...], obs_ref[...])
    oc = jnp.where(is_p, ocp_ref[...], ocs_ref[...])
    y = y + dot(oa, w_ref[0:A_V])
    y = y + dot(ob, w_ref[A_V:A_V + B_C])
    y = y + dot(oc, w_ref[A_V + B_C:D_MIX])
    y_ref[...] = y
    xn = _rms(y) * g_ref[...]
    _rows_to_tiles(xt_ref, xn)

    logits = _dot3(xn, wr_ref[...]) + br_ref[...]
    lane = lax.broadcasted_iota(I32, logits.shape, 1)
    lanef = lane.astype(F32)
    neg = -1e30
    lg = jnp.where(lane < N_GROUPS, logits, neg)
    mg = jnp.max(lg, axis=-1, keepdims=True)
    pg = 1.0 / jnp.sum(jnp.exp(lg - mg), axis=-1, keepdims=True)
    gidx = jnp.min(jnp.where(lg == mg, lanef, float(LANES)), axis=-1, keepdims=True)
    lo = N_GROUPS + EXPERTS_PER_GROUP * gidx
    in_grp = (lanef >= lo) & (lanef < lo + EXPERTS_PER_GROUP)
    el = jnp.where(in_grp, logits, neg)
    m1 = jnp.max(el, axis=-1, keepdims=True)
    i1 = jnp.min(jnp.where(el == m1, lanef, float(LANES)), axis=-1, keepdims=True)
    el2 = jnp.where(lanef == i1, neg, el)
    m2 = jnp.max(el2, axis=-1, keepdims=True)
    i2 = jnp.min(jnp.where(el2 == m2, lanef, float(LANES)), axis=-1, keepdims=True)
    den = jnp.sum(jnp.exp(el - m1), axis=-1, keepdims=True)
    p1 = 1.0 / den
    p2 = jnp.exp(m2 - m1) / den
    w1 = pg * p1 / (p1 + p2)
    w2 = pg * p2 / (p1 + p2)
    e1 = (i1 - N_GROUPS).astype(I32)
    e2 = (i2 - N_GROUPS).astype(I32)
    ri_ref[...] = jnp.where(lane == 0, e1, jnp.where(lane == 1, e2, 0))
    rw_ref[...] = jnp.where(lane == 0, w1, jnp.where(lane == 1, w2, 0.0))


def _out_proj(xs, mix_p, mix_s, w_out, g, w_router, b_router, precise):
    tp, ts = mix_p[0].shape[0], mix_s[0].shape[0]
    t = tp + ts
    tm = _pick(ts, (512, 256, 128, 64, 32, 16, 8))
    assert tp % tm == 0 and sum(x.shape[0] for x in xs) == t
    npt = tp // tm
    kern = functools.partial(_out_proj_kernel, n_prompt_tiles=npt, precise=precise, split_x=len(xs) == 2)
    pspec = lambda w: pl.BlockSpec((tm, w), lambda i: (jnp.minimum(i, npt - 1), 0))
    sspec = lambda w: pl.BlockSpec((tm, w), lambda i: (jnp.maximum(i - npt, 0), 0))
    full = lambda a: pl.BlockSpec(a.shape, lambda i: (0,) * a.ndim)
    rows = lambda w: pl.BlockSpec((tm, w), lambda i: (i, 0))
    cw = C_HEADS * C_VDIM
    g2 = g.reshape(1, D_MODEL)
    x_specs = [pspec(D_MODEL), sspec(D_MODEL)] if len(xs) == 2 else [rows(D_MODEL)]
    return pl.pallas_call(
        kern,
        grid=(t // tm,),
        in_specs=x_specs + [pspec(A_V), pspec(B_C), pspec(cw), sspec(A_V), sspec(B_C), sspec(cw),
                            full(w_out), full(g2), full(w_router), full(b_router)],
        out_specs=[rows(D_MODEL), pl.BlockSpec((tm * SUBLANES, LANES), lambda i: (i, 0)), rows(LANES), rows(LANES)],
        out_shape=[
            jax.ShapeDtypeStruct((t, D_MODEL), F32),
            jax.ShapeDtypeStruct((t * SUBLANES, LANES), F32),
            jax.ShapeDtypeStruct((t, LANES), I32),
            jax.ShapeDtypeStruct((t, LANES), F32),
        ],
        compiler_params=_cparams(("arbitrary",)),
        name="out_proj_router",
    )(*xs, *mix_p, *mix_s, w_out, g2, w_router, b_router)


def _rank_kernel(ri_ref, rank_ref, cnt_ref, base_ref):
    @pl.when(pl.program_id(0) == 0)
    def _():
        base_ref[...] = jnp.zeros_like(base_ref)

    ri = ri_ref[...]
    tm = ri.shape[0]
    lane = lax.broadcasted_iota(I32, ri.shape, 1)
    oh0 = lane == ri[:, 0:1]
    oh1 = lane == ri[:, 1:2]
    cnt = oh0.astype(F32) + oh1.astype(F32)
    row = lax.broadcasted_iota(I32, (tm, tm), 0)
    col = lax.broadcasted_iota(I32, (tm, tm), 1)
    before = jnp.dot((row > col).astype(BF16), cnt.astype(BF16), preferred_element_type=F32) + base_ref[...]
    r0 = jnp.sum(jnp.where(oh0, before, 0.0), axis=-1, keepdims=True)
    r1 = jnp.sum(jnp.where(oh1, before, 0.0), axis=-1, keepdims=True)
    rank_ref[...] = jnp.where(lane == 0, r0, jnp.where(lane == 1, r1, 0.0)).astype(I32)
    total = base_ref[...] + jnp.sum(cnt, axis=0, keepdims=True)
    base_ref[...] = total
    cnt_ref[...] = total.astype(I32)


def _rank(ri):
    t = ri.shape[0]
    tm = _pick(t, (256, 128, 64, 32, 16, 8))
    return pl.pallas_call(
        _rank_kernel,
        grid=(t // tm,),
        in_specs=[pl.BlockSpec((tm, LANES), lambda i: (i, 0))],
        out_specs=[pl.BlockSpec((tm, LANES), lambda i: (i, 0)), pl.BlockSpec((1, LANES), lambda i: (0, 0))],
        out_shape=[jax.ShapeDtypeStruct((t, LANES), I32), jax.ShapeDtypeStruct((1, LANES), I32)],
        scratch_shapes=[pltpu.VMEM((1, LANES), F32)],
        compiler_params=_cparams(("arbitrary",)),
        name="moe_rank",
    )(ri)


def _gather_tiles(idx_ref, n, src_hbm, buf, slot, sem):
    def body(j, carry):
        for queue in range(2):
            r = 2 * j + queue
            src = pl.multiple_of(idx_ref[0, r] * SUBLANES, SUBLANES)
            dst = pl.multiple_of((slot * n + r) * SUBLANES, SUBLANES)
            pltpu.make_async_copy(src_hbm.at[pl.ds(src, SUBLANES)], buf.at[pl.ds(dst, SUBLANES)],
                                  sem.at[slot]).start(priority=queue)
        return carry

    lax.fori_loop(0, n // 2, body, 0, unroll=4)


def _gather_wait(n, src_hbm, buf, slot, sem):
    base = pl.multiple_of(slot * n * SUBLANES, SUBLANES)
    pltpu.make_async_copy(src_hbm.at[pl.ds(0, n * SUBLANES)], buf.at[pl.ds(base, n * SUBLANES)], sem.at[slot]).wait()
    return base


def _expert_kernel(te_ref, nu_ref, idc_ref, idn_ref, xt_hbm, wg_ref, wu_ref, wd_ref, o_ref,
                   xbuf, wgb, wub, wdb, sem):
    i = pl.program_id(0)
    n_used = nu_ref[0]
    slot = lax.rem(i, 2)

    @pl.when((i == 0) & (n_used > 0))
    def _():
        _gather_tiles(idc_ref, MOE_TILE, xt_hbm, xbuf, 0, sem)

    @pl.when(i + 1 < n_used)
    def _():
        _gather_tiles(idn_ref, MOE_TILE, xt_hbm, xbuf, 1 - slot, sem)

    @pl.when((i == 0) | (te_ref[i] != te_ref[jnp.maximum(i - 1, 0)]))
    def _():
        wgb[...] = wg_ref[...].astype(BF16)
        wub[...] = wu_ref[...].astype(BF16)
        wdb[...] = wd_ref[...].astype(BF16)

    @pl.when(i < n_used)
    def _():
        base = _gather_wait(MOE_TILE, xt_hbm, xbuf, slot, sem)
        x = _tiles_to_rows(xbuf, base, MOE_TILE).astype(BF16)
        hg = jnp.dot(x, wgb[...], preferred_element_type=F32)
        hu = jnp.dot(x, wub[...], preferred_element_type=F32)
        hidden = (_silu(hg) * hu).astype(BF16)
        _rows_to_tiles(o_ref, jnp.dot(hidden, wdb[...], preferred_element_type=F32))

    @pl.when(i >= n_used)
    def _():
        o_ref[...] = jnp.zeros_like(o_ref)


def _expert_ffn(xt, token_of_pos, tile_expert, n_used, e_gate, e_up, e_down):
    nt = token_of_pos.shape[0] // MOE_TILE
    idx = token_of_pos.reshape(nt, 1, MOE_TILE)
    tile_rows = MOE_TILE * SUBLANES
    smem = lambda imap: pl.BlockSpec((None, 1, MOE_TILE), imap, memory_space=pltpu.SMEM)
    grid_spec = pltpu.PrefetchScalarGridSpec(
        num_scalar_prefetch=2,
        grid=(nt,),
        in_specs=[
            smem(lambda i, te, nu: (i, 0, 0)),
            smem(lambda i, te, nu: (jnp.minimum(i + 1, nt - 1), 0, 0)),
            pl.BlockSpec(memory_space=pl.ANY),
            pl.BlockSpec((None, D_MODEL, D_EXPERT), lambda i, te, nu: (te[i], 0, 0)),
            pl.BlockSpec((None, D_MODEL, D_EXPERT), lambda i, te, nu: (te[i], 0, 0)),
            pl.BlockSpec((None, D_EXPERT, D_MODEL), lambda i, te, nu: (te[i], 0, 0)),
        ],
        out_specs=pl.BlockSpec((tile_rows, LANES), lambda i, te, nu: (i, 0)),
        scratch_shapes=[pltpu.VMEM((2 * tile_rows, LANES), F32),
                        pltpu.VMEM((D_MODEL, D_EXPERT), BF16), pltpu.VMEM((D_MODEL, D_EXPERT), BF16),
                        pltpu.VMEM((D_EXPERT, D_MODEL), BF16), pltpu.SemaphoreType.DMA((2,))],
    )
    return pl.pallas_call(
        _expert_kernel,
        grid_spec=grid_spec,
        out_shape=jax.ShapeDtypeStruct((nt * tile_rows, LANES), F32),
        compiler_params=_cparams(("arbitrary",)),
        name="expert_ffn",
    )(tile_expert, n_used, idx, idx, xt, e_gate, e_up, e_down)


def _combine_kernel(pc_ref, pn_ref, x_ref, rw_ref, ot_hbm, *refs, tm, n_first_tiles):
    i = pl.program_id(0)
    nt = pl.num_programs(0)
    slot = lax.rem(i, 2)
    n = TOP_K * tm
    obuf, sem = refs[-2:]

    @pl.when(i == 0)
    def _():
        _gather_tiles(pc_ref, n, ot_hbm, obuf, 0, sem)

    @pl.when(i + 1 < nt)
    def _():
        _gather_tiles(pn_ref, n, ot_hbm, obuf, 1 - slot, sem)

    base = _gather_wait(n, ot_hbm, obuf, slot, sem)
    rw = rw_ref[...]
    o0 = _tiles_to_rows(obuf, base, tm)
    o1 = _tiles_to_rows(obuf, base + tm * SUBLANES, tm)
    y = x_ref[...] + rw[:, 0:1] * o0 + rw[:, 1:2] * o1
    if n_first_tiles is None:
        refs[0][...] = y
    else:
        @pl.when(i < n_first_tiles)
        def _():
            refs[0][...] = y

        @pl.when(i >= n_first_tiles)
        def _():
            refs[1][...] = y


def _combine(x, out_tiles, pos, rw, split_rows=None):
    t = x.shape[0]
    rows = lambda w: pl.BlockSpec((tm, w), lambda i: (i, 0))
    if split_rows is None:
        tm = _pick(t, (256, 128, 64, 32, 16, 8))
        nft = None
        out_specs = rows(D_MODEL)
        out_shape = jax.ShapeDtypeStruct((t, D_MODEL), F32)
    else:
        tm = _pick(t - split_rows, (256, 128, 64, 32, 16, 8))
        assert split_rows % tm == 0
        nft = split_rows // tm
        out_specs = [pl.BlockSpec((tm, D_MODEL), lambda i: (jnp.minimum(i, nft - 1), 0)),
                     pl.BlockSpec((tm, D_MODEL), lambda i: (jnp.maximum(i - nft, 0), 0))]
        out_shape = [jax.ShapeDtypeStruct((split_rows, D_MODEL), F32),
                     jax.ShapeDtypeStruct((t - split_rows, D_MODEL), F32)]
    nt = t // tm
    n = TOP_K * tm
    idx = pos.reshape(nt, tm, TOP_K).transpose(0, 2, 1).reshape(nt, 1, n)
    kern = functools.partial(_combine_kernel, tm=tm, n_first_tiles=nft)
    smem = lambda imap: pl.BlockSpec((None, 1, n), imap, memory_space=pltpu.SMEM)
    return pl.pallas_call(
        kern,
        grid=(nt,),
        in_specs=[smem(lambda i: (i, 0, 0)), smem(lambda i: (jnp.minimum(i + 1, nt - 1), 0, 0)),
                  rows(D_MODEL), rows(LANES), pl.BlockSpec(memory_space=pl.ANY)],
        out_specs=out_specs,
        out_shape=out_shape,
        scratch_shapes=[pltpu.VMEM((2 * n * SUBLANES, LANES), F32), pltpu.SemaphoreType.DMA((2,))],
        compiler_params=_cparams(("arbitrary",)),
        name="moe_combine",
    )(idx, idx, x, rw, out_tiles)


def _route_tables(eid, rank, counts):
    t = eid.shape[0]
    padded = ((counts + MOE_TILE - 1) // MOE_TILE) * MOE_TILE
    ends = jnp.cumsum(padded)
    starts = ends - padded
    experts = jnp.arange(N_EXPERTS, dtype=I32)
    pos = jnp.sum(jnp.where(eid[:, :, None] == experts, starts, 0), axis=-1) + rank
    np_rows = -(-(TOP_K * t + N_EXPERTS * MOE_TILE) // MOE_TILE) * MOE_TILE
    token_of_pos = jnp.zeros((np_rows,), I32).at[pos.reshape(-1)].set(jnp.arange(TOP_K * t, dtype=I32) // TOP_K)
    tile_start = jnp.arange(np_rows // MOE_TILE, dtype=I32) * MOE_TILE
    tile_expert = jnp.minimum(jnp.sum(tile_start[:, None] >= ends[None, :], axis=1), N_EXPERTS - 1).astype(I32)
    n_used = (ends[-1] // MOE_TILE).astype(I32).reshape(1)
    return pos.astype(I32), token_of_pos, tile_expert, n_used


def _moe(y, xt, ri, rw, e_gate, e_up, e_down, split_rows):
    rank, counts = _rank(ri)
    pos, token_of_pos, tile_expert, n_used = _route_tables(ri[:, :TOP_K], rank[:, :TOP_K], counts[0, :N_EXPERTS])
    out_tiles = _expert_ffn(xt, token_of_pos, tile_expert, n_used, e_gate, e_up, e_down)
    return _combine(y, out_tiles, pos, rw, split_rows)


def _in_weights(w_in):
    half = C_ROPE // 2
    zeros = lambda n: jnp.zeros((D_MODEL, n), F32)
    c0 = A_WIDTH + B_WIDTH
    kr = w_in[:, c0 + C_Q_LORA + C_KV_LORA:c0 + C_WIDTH]
    kr_rot = jnp.concatenate([-kr[:, half:], kr[:, :half]], axis=1)
    tail = LANES - C_NOPE - C_ROPE
    cols = [
        w_in[:, :A_CONV_CH + A_V], w_in[:, A_CONV_CH + A_V:A_WIDTH], zeros(LANES - 2 * A_HEADS),
        w_in[:, A_WIDTH:c0],
        w_in[:, c0:c0 + C_Q_LORA + C_KV_LORA],
        zeros(C_NOPE), kr, zeros(tail),
        zeros(C_NOPE), kr_rot, zeros(tail),
    ]
    return jnp.concatenate(cols, axis=1)


def _router_weights(rg, rgb, re, reb):
    w = jnp.zeros((D_MODEL, LANES), F32).at[:, :N_GROUPS].set(rg).at[:, N_GROUPS:N_GROUPS + N_EXPERTS].set(re)
    b = jnp.zeros((1, LANES), F32).at[0, :N_GROUPS].set(rgb).at[0, N_GROUPS:N_GROUPS + N_EXPERTS].set(reb)
    return w, b


def _layer(xs, geom, st_p, st_s, cos, sin, wts, precise, split_out):
    (norm_mix, w_in, a_conv_w, a_A_log, a_dt_bias, a_norm_w,
     b_mu, b_w0, b_w2, b_a0, b_a2, b_g2, b_k_k, b_k_a, b_r_k, b_ln_w, b_ln_b,
     c_q_norm, c_kv_norm, c_w_uq, c_w_ukv, c_q_gain, c_k_gain,
     w_out, norm_ffn, router_group, router_group_bias, router_expert, router_expert_bias,
     e_gate, e_up, e_down) = wts
    bp, lp, bs, ls, past = geom
    tp = bp * lp

    pa, pb, pc = _in_proj(xs, norm_mix, _in_weights(w_in), precise)
    cw = _c_weights(c_q_norm, c_kv_norm, c_w_uq, c_w_ukv, c_q_gain, c_k_gain)
    q, k, v, ckv, krope = _c_prep(pc, cos, sin, cw, precise)

    a_args = (a_conv_w, a_A_log, a_dt_bias, a_norm_w)
    b_args = (b_mu, b_w0, b_w2, b_a0, b_a2, b_g2, b_k_k, b_k_a, b_r_k.reshape(-1), b_ln_w, b_ln_b)
    conv_p, delta_p, shift_p, wkv_p = st_p
    conv_s, delta_s, shift_s, wkv_s, ckv_past, krope_past = st_s

    oa_p, conv_np, delta_np = _mixer_a(pa, 0, bp, lp, precise, conv_p, delta_p, *a_args)
    oa_s, conv_ns, delta_ns = _mixer_a(pa, tp, bs, ls, precise, conv_s, delta_s, *a_args)
    ob_p, shift_np, wkv_np = _mixer_b(pb, 0, bp, lp, precise, shift_p, wkv_p, *b_args)
    ob_s, shift_ns, wkv_ns = _mixer_b(pb, tp, bs, ls, precise, shift_s, wkv_s, *b_args)

    oc_p = _attention(q, k, v, 0, bp, lp, precise)
    krp_past = jnp.pad(krope_past.reshape(bs * past, C_ROPE), ((0, 0), (C_NOPE, LANES - C_NOPE - C_ROPE)))
    k_cache, v_cache = _kv_expand_call(ckv_past.reshape(bs * past, C_KV_LORA), krp_past, cw, precise)
    oc_s = _attention(q, k, v, tp, bs, ls, precise, k_cache, v_cache)

    w_router, b_router = _router_weights(router_group, router_group_bias, router_expert, router_expert_bias)
    y, xt, ri, rw = _out_proj(xs, (oa_p, ob_p, oc_p), (oa_s, ob_s, oc_s), w_out, norm_ffn,
                              w_router, b_router, precise)
    x_new = _moe(y, xt, ri, rw, e_gate, e_up, e_down, tp if split_out else None)

    new_p = (conv_np, delta_np, shift_np, wkv_np, ckv[:tp].reshape(bp, lp, C_KV_LORA),
             krope[:tp].reshape(bp, lp, C_ROPE))
    new_s = (conv_ns, delta_ns, shift_ns, wkv_ns, ckv[tp:].reshape(bs, ls, C_KV_LORA),
             krope[tp:].reshape(bs, ls, C_ROPE))
    return x_new, new_p, new_s


def _forward(x_prompt, x_sample, cache_c_kv, cache_k_rope, state_conv_a, state_delta_a, state_shift_b,
             state_wkv_b, weights):
    bp, lp, _ = x_prompt.shape
    bs, ls, _ = x_sample.shape
    depth = cache_c_kv.shape[0]
    past = cache_c_kv.shape[2]
    geom = (bp, lp, bs, ls, past)
    xs = [x_prompt.reshape(bp * lp, D_MODEL), x_sample.reshape(bs * ls, D_MODEL)]
    pos = jnp.concatenate([jnp.tile(jnp.arange(lp), bp), jnp.tile(past + jnp.arange(ls), bs)])
    cos, sin = _rope_tables(pos)
    zeros = lambda *s: jnp.zeros(s, F32)
    st_p = (zeros(bp, A_CONV - 1, A_CONV_CH), zeros(bp, A_HEADS, A_DK, A_DV), zeros(bp, 1, B_WIDTH),
            zeros(bp, B_HEADS, B_N, B_N))
    news_p, news_s = [], []
    for l in range(depth):
        st_s = (state_conv_a[l], state_delta_a[l], state_shift_b[l], state_wkv_b[l], cache_c_kv[l], cache_k_rope[l])
        last = l == depth - 1
        x, new_p, new_s = _layer(xs, geom, st_p, st_s, cos, sin, [w[l] for w in weights], precise=not last,
                                 split_out=last)
        xs = x if last else [x]
        news_p.append(new_p)
        news_s.append(new_s)
    stack = lambda news, i: jnp.stack([n[i] for n in news])
    y_prompt = xs[0].reshape(bp, lp, D_MODEL)
    y_sample = xs[1].reshape(bs, ls, D_MODEL)
    p_conv, p_delta, p_shift, p_wkv, p_ckv, p_krope = (stack(news_p, i) for i in range(6))
    s_conv, s_delta, s_shift, s_wkv, s_ckv, s_krope = (stack(news_s, i) for i in range(6))
    return (y_prompt, y_sample, p_ckv, p_krope, p_conv, p_delta, p_shift, p_wkv,
            s_ckv, s_krope, s_conv, s_delta, s_shift, s_wkv)


def kernel(x_prompt, x_sample, cache_c_kv, cache_k_rope, state_conv_a, state_delta_a, state_shift_b, state_wkv_b,
           norm_mix, w_in, a_conv_w, a_A_log, a_dt_bias, a_norm_w,
           b_mu, b_w0, b_w2, b_a0, b_a2, b_g2, b_k_k, b_k_a, b_r_k, b_ln_w, b_ln_b,
           c_q_norm, c_kv_norm, c_w_uq, c_w_ukv, c_q_gain, c_k_gain,
           w_out, norm_ffn, router_group, router_group_bias, router_expert, router_expert_bias,
           e_gate, e_up, e_down):
    weights = (norm_mix, w_in, a_conv_w, a_A_log, a_dt_bias, a_norm_w,
               b_mu, b_w0, b_w2, b_a0, b_a2, b_g2, b_k_k, b_k_a, b_r_k, b_ln_w, b_ln_b,
               c_q_norm, c_kv_norm, c_w_uq, c_w_ukv, c_q_gain, c_k_gain,
               w_out, norm_ffn, router_group, router_group_bias, router_expert, router_expert_bias,
               e_gate, e_up, e_down)
    return _forward(x_prompt, x_sample, cache_c_kv, cache_k_rope, state_conv_a, state_delta_a, state_shift_b,
                    state_wkv_b, weights)
```

```python
import functools

import jax
import jax.numpy as jnp
from jax import lax
from jax.experimental import pallas as pl
from jax.experimental.pallas import tpu as pltpu

F32 = jnp.float32
BF16 = jnp.bfloat16
I32 = jnp.int32

D_MODEL = 1024
DEPTH = 2
CHUNK = 64
EPS = 1e-6
LANES = 128
SUBLANES = 8
ROW_TILES = D_MODEL // LANES

A_HEADS, A_DK, A_DV, A_CONV = 6, 64, 64, 4
A_QK = A_HEADS * A_DK
A_V = A_HEADS * A_DV
A_CONV_CH = 2 * A_QK + A_V
A_WIDTH = A_CONV_CH + A_V + 2 * A_HEADS
A_SEG = A_CONV_CH + A_V + LANES

B_HEADS, B_N = 6, 64
B_C = B_HEADS * B_N
B_W_LORA, B_A_LORA, B_G_LORA = 64, 64, 128
B_WIDTH = 3 * B_C + B_W_LORA + B_A_LORA + B_G_LORA
B_GN_EPS = 64e-5

C_HEADS, C_NOPE, C_ROPE, C_VDIM = 4, 64, 32, 64
C_QK = C_NOPE + C_ROPE
C_Q_LORA, C_KV_LORA = 256, 128
C_WIDTH = C_Q_LORA + C_KV_LORA + C_ROPE
C_SEG = C_Q_LORA + C_KV_LORA + 2 * LANES
C_HPAD = LANES
ROPE_THETA = 10000.0

P_TOTAL = A_WIDTH + B_WIDTH + C_WIDTH
D_MIX = A_V + B_C + C_HEADS * C_VDIM

N_GROUPS, EXPERTS_PER_GROUP = 4, 8
N_EXPERTS = N_GROUPS * EXPERTS_PER_GROUP
TOP_K = 2
D_EXPERT = 256
MOE_TILE = 256
MIXER_CHUNKS_PER_STEP = 4
ATTN_KEY_BLOCK = 512

VMEM_LIMIT = 48 * 1024 * 1024

_NN = (((1,), (0,)), ((), ()))
_NT = (((1,), (1,)), ((), ()))
_TN = (((0,), (0,)), ((), ()))


def _pick(n, prefs):
    for p in prefs:
        if n % p == 0:
            return p
    raise ValueError(f"no tile for {n} in {prefs}")


def _cparams(sem):
    return pltpu.CompilerParams(dimension_semantics=sem, vmem_limit_bytes=VMEM_LIMIT)


def _bdot(a, b, dims=_NN):
    return lax.dot_general(a.astype(BF16), b.astype(BF16), dims, preferred_element_type=F32)


def _split2(a):
    hi = a.astype(BF16)
    lo = (a - hi.astype(F32)).astype(BF16)
    return hi, lo


def _dot3(a, b, dims=_NN):
    ah, al = _split2(a)
    bh, bl = _split2(b)
    f = lambda x, y: lax.dot_general(x, y, dims, preferred_element_type=F32)
    return f(ah, bh) + (f(ah, bl) + f(al, bh))


def _mdot(precise):
    return _dot3 if precise else _bdot


def _cumsum_rows(x, ltri):
    h = x.astype(BF16)
    r = x - h.astype(F32)
    m = r.astype(BF16)
    l = (r - m.astype(F32)).astype(BF16)
    d = lambda y: jnp.dot(ltri, y, preferred_element_type=F32)
    return d(h) + (d(m) + d(l))


def _tri_masks(c):
    row = lax.broadcasted_iota(I32, (c, c), 0)
    col = lax.broadcasted_iota(I32, (c, c), 1)
    return row >= col, row > col, (row == col).astype(F32)


def _chunk_tril(rows, c):
    shift = c.bit_length() - 1
    row = lax.broadcasted_iota(I32, (rows, rows), 0)
    col = lax.broadcasted_iota(I32, (rows, rows), 1)
    same = jnp.right_shift(row, shift) == jnp.right_shift(col, shift)
    return (same & (row >= col)).astype(BF16)


def _neumann_inv_many(ns, eye):
    c = eye.shape[0]
    keep_t = lax.broadcasted_iota(I32, (c, 2 * c), 1) >= c
    pts = [jnp.concatenate([n, eye], axis=1) for n in ns]
    m = 1
    while m < c:
        rs = [_dot3(pt[:, :c], pt) for pt in pts]
        pts = [r + jnp.where(keep_t, pt, 0.0) for r, pt in zip(rs, pts)]
        m *= 2
    return [pt[:, c:] for pt in pts]


def _softplus(x):
    return jnp.maximum(x, 0.0) + jnp.log1p(jnp.exp(-jnp.abs(x)))


def _silu(x):
    return x * jax.nn.sigmoid(x)


def _rms(x, eps=EPS):
    return x * lax.rsqrt(jnp.mean(x * x, axis=-1, keepdims=True) + eps)


def _l2n(x):
    return x * lax.rsqrt(jnp.sum(x * x, axis=-1, keepdims=True) + 1e-6)


def _rows_to_tiles(ref, val):
    n = val.shape[0]
    for j in range(ROW_TILES):
        ref[pl.ds(j, n, stride=SUBLANES), :] = val[:, j * LANES:(j + 1) * LANES]


def _tiles_to_rows(ref, start, n):
    return jnp.concatenate([ref[pl.ds(start + j, n, stride=SUBLANES), :] for j in range(ROW_TILES)], axis=1)


def _in_proj_kernel(*refs, precise, n_first_tiles):
    if n_first_tiles is None:
        x = refs[0][...]
        g_ref, *refs = refs[1:]
    else:
        x = jnp.where(pl.program_id(0) < n_first_tiles, refs[0][...], refs[1][...])
        g_ref, *refs = refs[2:]
    if precise:
        wh_ref, wl_ref, pa_ref, pb_ref, pc_ref = refs
    else:
        wh_ref, pa_ref, pb_ref, pc_ref = refs
    xn = _rms(x) * g_ref[...]
    xh = xn.astype(BF16)
    if precise:
        xl = (xn - xh.astype(F32)).astype(BF16)
    lo = 0
    for out_ref, width in ((pa_ref, A_SEG), (pb_ref, B_WIDTH), (pc_ref, C_SEG)):
        acc = jnp.dot(xh, wh_ref[:, lo:lo + width], preferred_element_type=F32)
        if precise:
            acc = acc + (jnp.dot(xh, wl_ref[:, lo:lo + width], preferred_element_type=F32)
                         + jnp.dot(xl, wh_ref[:, lo:lo + width], preferred_element_type=F32))
        out_ref[...] = acc
        lo += width


def _split_kernel(w_ref, hi_ref, lo_ref):
    hi, lo = _split2(w_ref[...])
    hi_ref[...] = hi
    lo_ref[...] = lo


def _split_hi_lo(w):
    r, c = w.shape
    tc = _pick(c, (512, 256, 128))
    spec = pl.BlockSpec((r, tc), lambda j: (0, j))
    return pl.pallas_call(
        _split_kernel,
        grid=(c // tc,),
        in_specs=[spec],
        out_specs=[spec, spec],
        out_shape=[jax.ShapeDtypeStruct((r, c), BF16), jax.ShapeDtypeStruct((r, c), BF16)],
        compiler_params=_cparams(("arbitrary",)),
        name="split_hi_lo",
    )(w)


def _in_proj(xs, g, w, precise):
    prefs = (256, 128, 64, 32, 16, 8) if precise else (512, 256, 128, 64, 32, 16, 8)
    t = sum(x.shape[0] for x in xs)
    if len(xs) == 1:
        tm = _pick(t, prefs)
        nft = None
        x_specs = [pl.BlockSpec((tm, D_MODEL), lambda i: (i, 0))]
    else:
        tm = _pick(xs[1].shape[0], prefs)
        assert xs[0].shape[0] % tm == 0
        nft = xs[0].shape[0] // tm
        x_specs = [pl.BlockSpec((tm, D_MODEL), lambda i: (jnp.minimum(i, nft - 1), 0)),
                   pl.BlockSpec((tm, D_MODEL), lambda i: (jnp.maximum(i - nft, 0), 0))]
    wtot = A_SEG + B_WIDTH + C_SEG
    wspec = pl.BlockSpec((D_MODEL, wtot), lambda i: (0, 0))
    w_args = _split_hi_lo(w) if precise else [w.astype(BF16)]
    return pl.pallas_call(
        functools.partial(_in_proj_kernel, precise=precise, n_first_tiles=nft),
        grid=(t // tm,),
        in_specs=x_specs + [pl.BlockSpec((1, D_MODEL), lambda i: (0, 0))] + [wspec] * len(w_args),
        out_specs=[
            pl.BlockSpec((tm, A_SEG), lambda i: (i, 0)),
            pl.BlockSpec((tm, B_WIDTH), lambda i: (i, 0)),
            pl.BlockSpec((tm, C_SEG), lambda i: (i, 0)),
        ],
        out_shape=[
            jax.ShapeDtypeStruct((t, A_SEG), F32),
            jax.ShapeDtypeStruct((t, B_WIDTH), F32),
            jax.ShapeDtypeStruct((t, C_SEG), F32),
        ],
        compiler_params=_cparams(("arbitrary",)),
        name="in_proj",
    )(*xs, g.reshape(1, D_MODEL), *w_args)


def _mixer_a_kernel(pa_ref, cprev_ref, s0_ref, cw_ref, alog_ref, dtb_ref, nw_ref,
                    oa_ref, cnew_ref, snew_ref, tail_ref, s_ref, *, c, nch, n_steps, precise):
    dot = _mdot(precise)
    step = pl.program_id(1)
    rows = c * nch

    @pl.when(step == 0)
    def _():
        tail_ref[...] = jnp.zeros_like(tail_ref)
        tail_ref[SUBLANES - (A_CONV - 1):SUBLANES, :] = cprev_ref[...]
        s_ref[...] = s0_ref[...]

    x = pa_ref[...]
    qkv = x[:, :A_CONV_CH]
    z = x[:, A_CONV_CH:A_CONV_CH + A_V]
    ab = x[:, A_CONV_CH + A_V:A_SEG]

    xp = jnp.concatenate([tail_ref[...], qkv], axis=0)
    cw = cw_ref[...]
    y = qkv * cw[A_CONV - 1:A_CONV]
    for j in range(A_CONV - 1):
        o = SUBLANES - (A_CONV - 1) + j
        y = y + xp[o:o + rows] * cw[j:j + 1]
    tail_ref[...] = qkv[rows - SUBLANES:rows]

    @pl.when(step == n_steps - 1)
    def _():
        cnew_ref[...] = qkv[rows - (A_CONV - 1):rows]

    act = _silu(y)
    g_all = -jnp.exp(alog_ref[...]) * _softplus(ab + dtb_ref[...])
    beta_all = jax.nn.sigmoid(ab)
    gc_all = _cumsum_rows(g_all, _chunk_tril(rows, c))
    gc_t = gc_all.T
    tril, strict, eye = _tri_masks(c)
    nw = nw_ref[...]

    qn = [_l2n(act[:, h * A_DK:(h + 1) * A_DK]) * (A_DK ** -0.5) for h in range(A_HEADS)]
    kn = [_l2n(act[:, A_QK + h * A_DK:A_QK + (h + 1) * A_DK]) for h in range(A_HEADS)]
    vv = [act[:, 2 * A_QK + h * A_DV:2 * A_QK + (h + 1) * A_DV] for h in range(A_HEADS)]

    items = [(ci, h) for ci in range(nch) for h in range(A_HEADS)]
    q_i, k_i, kb_i, vb_i, dec_i, gcc_i, gl_i = [], [], [], [], [], [], []
    for ci, h in items:
        r0 = ci * c
        beta = beta_all[r0:r0 + c, A_HEADS + h:A_HEADS + h + 1]
        gcc = gc_all[r0:r0 + c, h:h + 1]
        gcr = gc_t[h:h + 1, r0:r0 + c]
        k = kn[h][r0:r0 + c]
        q_i.append(qn[h][r0:r0 + c])
        k_i.append(k)
        kb_i.append(k * beta)
        vb_i.append(vv[h][r0:r0 + c] * beta)
        dec_i.append(jnp.exp(jnp.where(tril, gcc - gcr, -1e30)))
        gcc_i.append(gcc)
        gl_i.append(gc_all[r0 + c - 1:r0 + c, h:h + 1])
    a_i = [jnp.where(strict, dot(kb, k, _NT) * dec, 0.0) for kb, k, dec in zip(kb_i, k_i, dec_i)]
    t_i = _neumann_inv_many([-a for a in a_i], eye)
    egc_i = [jnp.exp(g) for g in gcc_i]
    sol_i = [_dot3(t, jnp.concatenate([vb, kb * e], axis=1)) for t, vb, kb, e in zip(t_i, vb_i, kb_i, egc_i)]
    attn_i = [jnp.where(tril, dot(q, k, _NT) * dec, 0.0) for q, k, dec in zip(q_i, k_i, dec_i)]
    kq_i = [jnp.concatenate([sol[:, A_DV:], q * e], axis=0) for sol, q, e in zip(sol_i, q_i, egc_i)]
    kg_i = [k * jnp.exp(gl - g) for k, gl, g in zip(k_i, gl_i, gcc_i)]

    s = [s_ref[h] for h in range(A_HEADS)]
    out_rows = []
    for ci in range(nch):
        ids = [ci * A_HEADS + h for h in range(A_HEADS)]
        ks = [dot(kq_i[i], s[h]) for h, i in enumerate(ids)]
        u = [sol_i[i][:, :A_DV] - ks[h][:c] for h, i in enumerate(ids)]
        o = [ks[h][c:] + dot(attn_i[i], u[h]) for h, i in enumerate(ids)]
        s = [s[h] * jnp.exp(gl_i[i]) + dot(kg_i[i], u[h], _TN) for h, i in enumerate(ids)]
        r0 = ci * c
        outs = [_rms(o[h]) * nw * _silu(z[r0:r0 + c, h * A_DV:(h + 1) * A_DV]) for h in range(A_HEADS)]
        out_rows.append(jnp.concatenate(outs, axis=1))
    for h in range(A_HEADS):
        s_ref[h] = s[h]
    oa_ref[...] = jnp.concatenate(out_rows, axis=0).astype(oa_ref.dtype)

    @pl.when(step == n_steps - 1)
    def _():
        snew_ref[...] = s_ref[...]


def _row(v, width=LANES):
    v = v.reshape(1, -1).astype(F32)
    return jnp.pad(v, ((0, 0), (0, width - v.shape[1])))


def _mixer_geometry(row0, seq):
    c = min(CHUNK, seq)
    nch = _pick(seq // c, (MIXER_CHUNKS_PER_STEP, 2, 1))
    rows = c * nch
    assert row0 % rows == 0
    return c, nch, rows, seq // rows, row0 // rows


def _mixer_a(pa, row0, bsz, seq, precise, conv_prev, s0, conv_w, a_log, dt_bias, norm_w):
    c, nch, rows, n_steps, blk0 = _mixer_geometry(row0, seq)
    kern = functools.partial(_mixer_a_kernel, c=c, nch=nch, n_steps=n_steps, precise=precise)
    full = lambda shape: pl.BlockSpec(shape, lambda b, n: (0,) * len(shape))
    return pl.pallas_call(
        kern,
        grid=(bsz, n_steps),
        in_specs=[
            pl.BlockSpec((rows, A_SEG), lambda b, n: (blk0 + b * n_steps + n, 0)),
            pl.BlockSpec((None, A_CONV - 1, A_CONV_CH), lambda b, n: (b, 0, 0)),
            pl.BlockSpec((None, A_HEADS, A_DK, A_DV), lambda b, n: (b, 0, 0, 0)),
            full((A_CONV, A_CONV_CH)),
            full((1, LANES)),
            full((1, LANES)),
            full((1, A_DV)),
        ],
        out_specs=[
            pl.BlockSpec((rows, A_V), lambda b, n: (b * n_steps + n, 0)),
            pl.BlockSpec((None, A_CONV - 1, A_CONV_CH), lambda b, n: (b, 0, 0)),
            pl.BlockSpec((None, A_HEADS, A_DK, A_DV), lambda b, n: (b, 0, 0, 0)),
        ],
        out_shape=[
            jax.ShapeDtypeStruct((bsz * seq, A_V), F32 if precise else BF16),
            jax.ShapeDtypeStruct((bsz, A_CONV - 1, A_CONV_CH), F32),
            jax.ShapeDtypeStruct((bsz, A_HEADS, A_DK, A_DV), F32),
        ],
        scratch_shapes=[pltpu.VMEM((SUBLANES, A_CONV_CH), F32), pltpu.VMEM((A_HEADS, A_DK, A_DV), F32)],
        compiler_params=_cparams(("arbitrary", "arbitrary")),
        name="mixer_a",
    )(pa, conv_prev, s0, conv_w, _row(a_log), _row(dt_bias), norm_w.reshape(1, A_DV))


def _mixer_b_kernel(pb_ref, sprev_ref, s0_ref, mu_ref, w0_ref, w2_ref, a0_ref, a2_ref, g2_ref,
                    kk_ref, ka_ref, rk_ref, lnw_ref, lnb_ref,
                    ob_ref, shnew_ref, snew_ref, last_ref, s_ref, *, c, nch, n_steps, precise):
    dot = _mdot(precise)
    step = pl.program_id(1)
    rows = c * nch

    @pl.when(step == 0)
    def _():
        last_ref[...] = sprev_ref[...]
        s_ref[...] = s0_ref[...]

    p = pb_ref[...]
    rowi = lax.broadcasted_iota(I32, (rows, 1), 0)
    prev = jnp.where(rowi == 0, last_ref[...], pltpu.roll(p, 1, 0))
    last_ref[...] = p[rows - 1:rows]

    @pl.when(step == n_steps - 1)
    def _():
        shnew_ref[...] = p[rows - 1:rows]

    xs = p + (prev - p) * mu_ref[...]
    r_all = xs[:, 0:B_C]
    k_all = xs[:, B_C:2 * B_C]
    v_all = xs[:, 2 * B_C:3 * B_C]
    xwa = xs[:, 3 * B_C:3 * B_C + B_W_LORA + B_A_LORA]
    xg = xs[:, 3 * B_C + B_W_LORA + B_A_LORA:B_WIDTH]

    w_log = -_softplus(-(w0_ref[...] + dot(jnp.tanh(xwa), w2_ref[...]))) - 0.5
    lw = -jnp.exp(w_log)
    rate = jax.nn.sigmoid(a0_ref[...] + dot(xwa, a2_ref[...]))
    gate = dot(jax.nn.sigmoid(xg), g2_ref[...])
    kkr = k_all * kk_ref[...]
    k2_all = k_all * (1.0 + (rate - 1.0) * ka_ref[...])

    cl = _cumsum_rows(lw, _chunk_tril(rows, c))
    e_cl = jnp.exp(cl)
    e_neg = jnp.exp(-cl)
    at_all = jnp.exp(cl - lw)
    bt_all = rate * e_neg
    kt_all = k2_all * e_neg
    rt_all = r_all * e_cl
    tril, strict, eye = _tri_masks(c)
    rk = rk_ref[...]
    lnw = lnw_ref[...]
    lnb = lnb_ref[...]
    kkn = [_l2n(kkr[:, h * B_N:(h + 1) * B_N]) for h in range(B_HEADS)]

    items = [(ci, h) for ci in range(nch) for h in range(B_HEADS)]
    at_i, bt_i, kt_i, rt_i, v_i = [], [], [], [], []
    for ci, h in items:
        rs = slice(ci * c, (ci + 1) * c)
        ls = slice(h * B_N, (h + 1) * B_N)
        kk = kkn[h][rs]
        at_i.append(-kk * at_all[rs, ls])
        bt_i.append(kk * bt_all[rs, ls])
        kt_i.append(kt_all[rs, ls])
        rt_i.append(rt_all[rs, ls])
        v_i.append(v_all[rs, ls])
    bk_i = [jnp.concatenate([b, k], axis=0) for b, k in zip(bt_i, kt_i)]
    ar_i = [jnp.concatenate([a, r], axis=0) for a, r in zip(at_i, rt_i)]
    row2 = lax.broadcasted_iota(I32, (c, 2 * c), 0)
    col2 = jnp.bitwise_and(lax.broadcasted_iota(I32, (c, 2 * c), 1), c - 1)
    strict2 = row2 > col2
    tril2 = row2 >= col2
    g_a = [jnp.where(strict2, _dot3(a, bk, _NT), 0.0) for a, bk in zip(at_i, bk_i)]
    g_r = [jnp.where(tril2, dot(r, bk, _NT), 0.0) for r, bk in zip(rt_i, bk_i)]
    t_i = _neumann_inv_many([g[:, :c] for g in g_a], eye)
    mv_i = [dot(g[:, c:], v) for g, v in zip(g_a, v_i)]

    s = [s_ref[h] for h in range(B_HEADS)]
    out_rows = []
    for ci in range(nch):
        ids = [ci * B_HEADS + h for h in range(B_HEADS)]
        rs = slice(ci * c, (ci + 1) * c)
        ars = [dot(ar_i[i], s[h], _NT) for h, i in enumerate(ids)]
        u = [_dot3(t_i[i], ars[h][:c] + mv_i[i]) for h, i in enumerate(ids)]
        uv = [jnp.concatenate([u[h], v_i[i]], axis=0) for h, i in enumerate(ids)]
        o = [ars[h][c:] + dot(g_r[i], uv[h]) for h, i in enumerate(ids)]
        e_last = e_cl[(ci + 1) * c - 1:(ci + 1) * c]
        s = [(s[h] + dot(uv[h], bk_i[i], _TN)) * e_last[:, h * B_N:(h + 1) * B_N] for h, i in enumerate(ids)]
        outs = []
        for h in range(B_HEADS):
            ls = slice(h * B_N, (h + 1) * B_N)
            mean = jnp.mean(o[h], axis=-1, keepdims=True)
            var = jnp.mean(jnp.square(o[h] - mean), axis=-1, keepdims=True)
            gn = (o[h] - mean) * lax.rsqrt(var + B_GN_EPS) * lnw[:, ls] + lnb[:, ls]
            bonus = jnp.sum(r_all[rs, ls] * k2_all[rs, ls] * rk[:, ls], axis=-1, keepdims=True) * v_all[rs, ls]
            outs.append((gn + bonus) * gate[rs, ls])
        out_rows.append(jnp.concatenate(outs, axis=1))
    for h in range(B_HEADS):
        s_ref[h] = s[h]
    ob_ref[...] = jnp.concatenate(out_rows, axis=0).astype(ob_ref.dtype)

    @pl.when(step == n_steps - 1)
    def _():
        snew_ref[...] = s_ref[...]


def _mixer_b(pb, row0, bsz, seq, precise, shift_prev, s0, mu, w0, w2, a0, a2, g2, k_k, k_a, r_k, ln_w, ln_b):
    c, nch, rows, n_steps, blk0 = _mixer_geometry(row0, seq)
    kern = functools.partial(_mixer_b_kernel, c=c, nch=nch, n_steps=n_steps, precise=precise)
    full = lambda shape: pl.BlockSpec(shape, lambda b, n: (0,) * len(shape))
    lora = B_W_LORA + B_A_LORA
    w2p = jnp.zeros((lora, B_C), F32).at[:B_W_LORA].set(w2)
    a2p = jnp.zeros((lora, B_C), F32).at[B_W_LORA:].set(a2)
    vec = lambda v: v.reshape(1, -1).astype(F32)
    return pl.pallas_call(
        kern,
        grid=(bsz, n_steps),
        in_specs=[
            pl.BlockSpec((rows, B_WIDTH), lambda b, n: (blk0 + b * n_steps + n, 0)),
            pl.BlockSpec((None, 1, B_WIDTH), lambda b, n: (b, 0, 0)),
            pl.BlockSpec((None, B_HEADS, B_N, B_N), lambda b, n: (b, 0, 0, 0)),
            full((1, B_WIDTH)),
            full((1, B_C)), full((lora, B_C)),
            full((1, B_C)), full((lora, B_C)),
            full((B_G_LORA, B_C)),
            full((1, B_C)), full((1, B_C)), full((1, B_C)), full((1, B_C)), full((1, B_C)),
        ],
        out_specs=[
            pl.BlockSpec((rows, B_C), lambda b, n: (b * n_steps + n, 0)),
            pl.BlockSpec((None, 1, B_WIDTH), lambda b, n: (b, 0, 0)),
            pl.BlockSpec((None, B_HEADS, B_N, B_N), lambda b, n: (b, 0, 0, 0)),
        ],
        out_shape=[
            jax.ShapeDtypeStruct((bsz * seq, B_C), F32 if precise else BF16),
            jax.ShapeDtypeStruct((bsz, 1, B_WIDTH), F32),
            jax.ShapeDtypeStruct((bsz, B_HEADS, B_N, B_N), F32),
        ],
        scratch_shapes=[pltpu.VMEM((1, B_WIDTH), F32), pltpu.VMEM((B_HEADS, B_N, B_N), F32)],
        compiler_params=_cparams(("arbitrary", "arbitrary")),
        name="mixer_b",
    )(pb, shift_prev, s0, vec(mu), vec(w0), w2p, vec(a0), a2p, g2,
      vec(k_k), vec(k_a), vec(r_k), vec(ln_w), vec(ln_b))


def _kv_expand(ckvn, krope, wuk_ref, wuv_ref, gk_ref, k_ref, v_ref, dot):
    k0 = dot(ckvn, wuk_ref[...])
    gk = gk_ref[...]
    for h in range(C_HEADS):
        kh = k0[:, h * C_HPAD:(h + 1) * C_HPAD] + krope
        ss = jnp.sum(kh * kh, axis=-1, keepdims=True) * (1.0 / C_QK)
        k_ref[:, h * C_HPAD:(h + 1) * C_HPAD] = (kh * lax.rsqrt(ss + EPS) * gk).astype(k_ref.dtype)
    v_ref[...] = dot(ckvn, wuv_ref[...]).astype(v_ref.dtype)


def _c_prep_kernel(pc_ref, cos_ref, sin_ref, qn_ref, kvn_ref, wq_ref, wqr_ref, wuk_ref, wuv_ref, gq_ref, gk_ref,
                   q_ref, k_ref, v_ref, ckv_ref, kr_ref, *, precise):
    dot = _mdot(precise)
    pc = pc_ref[...]
    cq = pc[:, 0:C_Q_LORA]
    ckv_raw = pc[:, C_Q_LORA:C_Q_LORA + C_KV_LORA]
    krp = pc[:, C_Q_LORA + C_KV_LORA:C_Q_LORA + C_KV_LORA + LANES]
    krr = pc[:, C_Q_LORA + C_KV_LORA + LANES:C_SEG]
    cos = cos_ref[...]
    sin = sin_ref[...]
    lane = lax.broadcasted_iota(I32, cos.shape, 1)
    cosq = jnp.where(lane < C_NOPE, 1.0, cos)

    cqn = _rms(cq) * qn_ref[...]
    q0 = dot(cqn, wq_ref[...])
    q1 = dot(cqn, wqr_ref[...])
    gq = gq_ref[...] * (C_QK ** -0.5)
    for h in range(C_HEADS):
        sl = slice(h * C_HPAD, (h + 1) * C_HPAD)
        qh = q0[:, sl] * cosq + q1[:, sl] * sin
        ss = jnp.sum(qh * qh, axis=-1, keepdims=True) * (1.0 / C_QK)
        q_ref[:, sl] = (qh * lax.rsqrt(ss + EPS) * gq).astype(q_ref.dtype)

    ckvn = _rms(ckv_raw) * kvn_ref[...]
    ckv_ref[...] = ckvn
    krope = krp * cos + krr * sin
    kr_ref[...] = krope[:, C_NOPE:C_NOPE + C_ROPE]
    _kv_expand(ckvn, krope, wuk_ref, wuv_ref, gk_ref, k_ref, v_ref, dot)


def _kv_expand_kernel(ckv_ref, kr_ref, wuk_ref, wuv_ref, gk_ref, k_ref, v_ref, *, precise):
    _kv_expand(ckv_ref[...], kr_ref[...], wuk_ref, wuv_ref, gk_ref, k_ref, v_ref, _mdot(precise))


def _c_weights(q_norm, kv_norm, w_uq, w_ukv, q_gain, k_gain):
    half = C_ROPE // 2
    wq = jnp.zeros((C_Q_LORA, C_HEADS * C_HPAD), F32)
    wqr = jnp.zeros((C_Q_LORA, C_HEADS * C_HPAD), F32)
    wuk = jnp.zeros((C_KV_LORA, C_HEADS * C_HPAD), F32)
    wuv = jnp.zeros((C_KV_LORA, C_HEADS * C_VDIM), F32)
    for h in range(C_HEADS):
        wh = w_uq[:, h * C_QK:(h + 1) * C_QK]
        wq = wq.at[:, h * C_HPAD:h * C_HPAD + C_QK].set(wh)
        rot = jnp.concatenate([-wh[:, C_NOPE + half:], wh[:, C_NOPE:C_NOPE + half]], axis=1)
        wqr = wqr.at[:, h * C_HPAD + C_NOPE:h * C_HPAD + C_QK].set(rot)
        kvh = w_ukv[:, h * (C_NOPE + C_VDIM):(h + 1) * (C_NOPE + C_VDIM)]
        wuk = wuk.at[:, h * C_HPAD:h * C_HPAD + C_NOPE].set(kvh[:, :C_NOPE])
        wuv = wuv.at[:, h * C_VDIM:(h + 1) * C_VDIM].set(kvh[:, C_NOPE:])
    gain = lambda g: _row(jnp.concatenate([g[:C_NOPE], g[C_NOPE:], g[C_NOPE:]]))
    return (q_norm.reshape(1, -1), kv_norm.reshape(1, -1), wq, wqr, wuk, wuv, gain(q_gain), gain(k_gain))


def _rope_tables(pos):
    half = C_ROPE // 2
    inv = ROPE_THETA ** (-(jnp.arange(0, C_ROPE, 2, dtype=F32) / C_ROPE))
    ang = pos.astype(F32)[:, None] * inv[None, :]
    cos, sin = jnp.cos(ang), jnp.sin(ang)
    pad = lambda t: jnp.pad(jnp.concatenate([t, t], axis=1), ((0, 0), (C_NOPE, LANES - C_NOPE - 2 * half)))
    return pad(cos), pad(sin)


def _c_prep(pc, cos, sin, cw, precise):
    t = pc.shape[0]
    tm = _pick(t, (512, 256, 128, 64, 32, 16, 8))
    qn, kvn, wq, wqr, wuk, wuv, gq, gk = cw
    full = lambda a: pl.BlockSpec(a.shape, lambda i: (0,) * a.ndim)
    rows = lambda w: pl.BlockSpec((tm, w), lambda i: (i, 0))
    act = F32 if precise else BF16
    return pl.pallas_call(
        functools.partial(_c_prep_kernel, precise=precise),
        grid=(t // tm,),
        in_specs=[rows(C_SEG), rows(LANES), rows(LANES)] + [full(a) for a in (qn, kvn, wq, wqr, wuk, wuv, gq, gk)],
        out_specs=[rows(C_HEADS * C_HPAD), rows(C_HEADS * C_HPAD), rows(C_HEADS * C_VDIM), rows(C_KV_LORA),
                   rows(C_ROPE)],
        out_shape=[
            jax.ShapeDtypeStruct((t, C_HEADS * C_HPAD), act),
            jax.ShapeDtypeStruct((t, C_HEADS * C_HPAD), act),
            jax.ShapeDtypeStruct((t, C_HEADS * C_VDIM), act),
            jax.ShapeDtypeStruct((t, C_KV_LORA), F32),
            jax.ShapeDtypeStruct((t, C_ROPE), F32),
        ],
        compiler_params=_cparams(("arbitrary",)),
        name="c_prep",
    )(pc, cos, sin, qn, kvn, wq, wqr, wuk, wuv, gq, gk)


def _kv_expand_call(ckv, krp, cw, precise):
    t = ckv.shape[0]
    tm = _pick(t, (1024, 512, 256, 128, 64, 32, 16, 8))
    _, _, _, _, wuk, wuv, _, gk = cw
    full = lambda a: pl.BlockSpec(a.shape, lambda i: (0,) * a.ndim)
    rows = lambda w: pl.BlockSpec((tm, w), lambda i: (i, 0))
    act = F32 if precise else BF16
    return pl.pallas_call(
        functools.partial(_kv_expand_kernel, precise=precise),
        grid=(t // tm,),
        in_specs=[rows(C_KV_LORA), rows(LANES), full(wuk), full(wuv), full(gk)],
        out_specs=[rows(C_HEADS * C_HPAD), rows(C_HEADS * C_VDIM)],
        out_shape=[
            jax.ShapeDtypeStruct((t, C_HEADS * C_HPAD), act),
            jax.ShapeDtypeStruct((t, C_HEADS * C_VDIM), act),
        ],
        compiler_params=_cparams(("arbitrary",)),
        name="kv_expand",
    )(ckv, krp, wuk, wuv, gk)


def _attn_kernel(*refs, tq, nq, past, has_cache, precise):
    if has_cache:
        q_ref, kn_ref, vn_ref, kc_ref, vc_ref, o_ref = refs
    else:
        q_ref, kn_ref, vn_ref, o_ref = refs
    dot = _mdot(precise)
    qi = pl.program_id(1)
    shift = CHUNK.bit_length() - 1

    q = q_ref[...]
    q_chunk = jnp.right_shift(past + qi * tq + lax.broadcasted_iota(I32, (tq, 1), 0), shift)
    qh = [q[:, h * C_HPAD:(h + 1) * C_HPAD] for h in range(C_HEADS)]
    init = tuple((jnp.full((tq, 1), -1e30, F32), jnp.zeros((tq, 1), F32), jnp.zeros((tq, C_VDIM), F32))
                 for _ in range(C_HEADS))

    def step(state, kb, vb, p0, w):
        vis = jnp.right_shift(p0 + lax.broadcasted_iota(I32, (1, w), 1), shift) <= q_chunk
        new = []
        for h in range(C_HEADS):
            m, den, acc = state[h]
            s = jnp.where(vis, dot(qh[h], kb[:, h * C_HPAD:(h + 1) * C_HPAD], _NT), -1e30)
            m_new = jnp.maximum(m, jnp.max(s, axis=-1, keepdims=True))
            scale = jnp.exp(m - m_new)
            p = jnp.exp(s - m_new)
            new.append((m_new, den * scale + jnp.sum(p, axis=-1, keepdims=True),
                        acc * scale + dot(p, vb[:, h * C_VDIM:(h + 1) * C_VDIM])))
        return tuple(new)

    def finish(state):
        o_ref[...] = jnp.concatenate([acc / den for _, den, acc in state], axis=1).astype(o_ref.dtype)

    if precise:
        def new_step(j, st):
            r0 = pl.multiple_of(j * tq, tq)
            return step(st, kn_ref[pl.ds(r0, tq), :], vn_ref[pl.ds(r0, tq), :], past + r0, tq)

        state = lax.fori_loop(0, qi + 1, new_step, init)
        if has_cache:
            tc = _pick(past, (ATTN_KEY_BLOCK, 256, 128, 64, 32, 16, 8))

            def cache_step(j, st):
                r0 = pl.multiple_of(j * tc, tc)
                return step(st, kc_ref[pl.ds(r0, tc), :], vc_ref[pl.ds(r0, tc), :], r0, tc)

            state = lax.fori_loop(0, past // tc, cache_step, state)
        finish(state)
    else:
        def attend(n_new):
            blocks = [(kn_ref, vn_ref, r0, min(ATTN_KEY_BLOCK, n_new - r0), past + r0)
                      for r0 in range(0, n_new, ATTN_KEY_BLOCK)]
            if has_cache:
                blocks += [(kc_ref, vc_ref, r0, min(ATTN_KEY_BLOCK, past - r0), r0)
                           for r0 in range(0, past, ATTN_KEY_BLOCK)]
            state = init
            for k_ref, v_ref, r0, w, p0 in blocks:
                state = step(state, k_ref[r0:r0 + w, :], v_ref[r0:r0 + w, :], p0, w)
            finish(state)

        if nq == 1:
            attend(tq)
        else:
            for blk in range(nq):
                pl.when(qi == blk)(functools.partial(attend, (blk + 1) * tq))


def _attention(q, k, v, row0, bsz, seq, precise, k_cache=None, v_cache=None):
    has_cache = k_cache is not None
    past = k_cache.shape[0] // bsz if has_cache else 0
    tq = _pick(seq, (256, 128, 64, 32, 16, 8))
    assert tq % CHUNK == 0 or tq == seq
    nq = seq // tq
    kern = functools.partial(_attn_kernel, tq=tq, nq=nq, past=past, has_cache=has_cache, precise=precise)
    qw, vw = C_HEADS * C_HPAD, C_HEADS * C_VDIM
    in_specs = [
        pl.BlockSpec((tq, qw), lambda b, i: (row0 // tq + b * nq + i, 0)),
        pl.BlockSpec((seq, qw), lambda b, i: (row0 // seq + b, 0)),
        pl.BlockSpec((seq, vw), lambda b, i: (row0 // seq + b, 0)),
    ]
    args = [q, k, v]
    if has_cache:
        in_specs += [pl.BlockSpec((past, qw), lambda b, i: (b, 0)), pl.BlockSpec((past, vw), lambda b, i: (b, 0))]
        args += [k_cache, v_cache]
    return pl.pallas_call(
        kern,
        grid=(bsz, nq),
        in_specs=in_specs,
        out_specs=pl.BlockSpec((tq, vw), lambda b, i: (b * nq + i, 0)),
        out_shape=jax.ShapeDtypeStruct((bsz * seq, vw), F32 if precise else BF16),
        compiler_params=_cparams(("arbitrary", "arbitrary")),
        name="attention",
    )(*args)


def _out_proj_kernel(*refs, n_prompt_tiles, precise, split_x):
    dot = _mdot(precise)
    is_p = pl.program_id(0) < n_prompt_tiles
    if split_x:
        y = jnp.where(is_p, refs[0][...], refs[1][...])
        refs = refs[2:]
    else:
        y = refs[0][...]
        refs = refs[1:]
    (oap_ref, obp_ref, ocp_ref, oas_ref, obs_ref, ocs_ref, w_ref, g_ref, wr_ref, br_ref,
     y_ref, xt_ref, ri_ref, rw_ref) = refs
    oa = jnp.where(is_p, oap_ref[...], oas_ref[...])
    ob = jnp.where(is_p, obp_ref[...], obs_ref[...])
    oc = jnp.where(is_p, ocp_ref[...], ocs_ref[...])
    y = y + dot(oa, w_ref[0:A_V])
    y = y + dot(ob, w_ref[A_V:A_V + B_C])
    y = y + dot(oc, w_ref[A_V + B_C:D_MIX])
    y_ref[...] = y
    xn = _rms(y) * g_ref[...]
    _rows_to_tiles(xt_ref, xn)

    logits = _dot3(xn, wr_ref[...]) + br_ref[...]
    lane = lax.broadcasted_iota(I32, logits.shape, 1)
    lanef = lane.astype(F32)
    neg = -1e30
    lg = jnp.where(lane < N_GROUPS, logits, neg)
    mg = jnp.max(lg, axis=-1, keepdims=True)
    pg = 1.0 / jnp.sum(jnp.exp(lg - mg), axis=-1, keepdims=True)
    gidx = jnp.min(jnp.where(lg == mg, lanef, float(LANES)), axis=-1, keepdims=True)
    lo = N_GROUPS + EXPERTS_PER_GROUP * gidx
    in_grp = (lanef >= lo) & (lanef < lo + EXPERTS_PER_GROUP)
    el = jnp.where(in_grp, logits, neg)
    m1 = jnp.max(el, axis=-1, keepdims=True)
    i1 = jnp.min(jnp.where(el == m1, lanef, float(LANES)), axis=-1, keepdims=True)
    el2 = jnp.where(lanef == i1, neg, el)
    m2 = jnp.max(el2, axis=-1, keepdims=True)
    i2 = jnp.min(jnp.where(el2 == m2, lanef, float(LANES)), axis=-1, keepdims=True)
    den = jnp.sum(jnp.exp(el - m1), axis=-1, keepdims=True)
    p1 = 1.0 / den
    p2 = jnp.exp(m2 - m1) / den
    w1 = pg * p1 / (p1 + p2)
    w2 = pg * p2 / (p1 + p2)
    e1 = (i1 - N_GROUPS).astype(I32)
    e2 = (i2 - N_GROUPS).astype(I32)
    ri_ref[...] = jnp.where(lane == 0, e1, jnp.where(lane == 1, e2, 0))
    rw_ref[...] = jnp.where(lane == 0, w1, jnp.where(lane == 1, w2, 0.0))


def _out_proj(xs, mix_p, mix_s, w_out, g, w_router, b_router, precise):
    tp, ts = mix_p[0].shape[0], mix_s[0].shape[0]
    t = tp + ts
    tm = _pick(ts, (512, 256, 128, 64, 32, 16, 8))
    assert tp % tm == 0 and sum(x.shape[0] for x in xs) == t
    npt = tp // tm
    kern = functools.partial(_out_proj_kernel, n_prompt_tiles=npt, precise=precise, split_x=len(xs) == 2)
    pspec = lambda w: pl.BlockSpec((tm, w), lambda i: (jnp.minimum(i, npt - 1), 0))
    sspec = lambda w: pl.BlockSpec((tm, w), lambda i: (jnp.maximum(i - npt, 0), 0))
    full = lambda a: pl.BlockSpec(a.shape, lambda i: (0,) * a.ndim)
    rows = lambda w: pl.BlockSpec((tm, w), lambda i: (i, 0))
    cw = C_HEADS * C_VDIM
    g2 = g.reshape(1, D_MODEL)
    x_specs = [pspec(D_MODEL), sspec(D_MODEL)] if len(xs) == 2 else [rows(D_MODEL)]
    return pl.pallas_call(
        kern,
        grid=(t // tm,),
        in_specs=x_specs + [pspec(A_V), pspec(B_C), pspec(cw), sspec(A_V), sspec(B_C), sspec(cw),
                            full(w_out), full(g2), full(w_router), full(b_router)],
        out_specs=[rows(D_MODEL), pl.BlockSpec((tm * SUBLANES, LANES), lambda i: (i, 0)), rows(LANES), rows(LANES)],
        out_shape=[
            jax.ShapeDtypeStruct((t, D_MODEL), F32),
            jax.ShapeDtypeStruct((t * SUBLANES, LANES), F32),
            jax.ShapeDtypeStruct((t, LANES), I32),
            jax.ShapeDtypeStruct((t, LANES), F32),
        ],
        compiler_params=_cparams(("arbitrary",)),
        name="out_proj_router",
    )(*xs, *mix_p, *mix_s, w_out, g2, w_router, b_router)


def _rank_kernel(ri_ref, rank_ref, cnt_ref, base_ref):
    @pl.when(pl.program_id(0) == 0)
    def _():
        base_ref[...] = jnp.zeros_like(base_ref)

    ri = ri_ref[...]
    tm = ri.shape[0]
    lane = lax.broadcasted_iota(I32, ri.shape, 1)
    oh0 = lane == ri[:, 0:1]
    oh1 = lane == ri[:, 1:2]
    cnt = oh0.astype(F32) + oh1.astype(F32)
    row = lax.broadcasted_iota(I32, (tm, tm), 0)
    col = lax.broadcasted_iota(I32, (tm, tm), 1)
    before = jnp.dot((row > col).astype(BF16), cnt.astype(BF16), preferred_element_type=F32) + base_ref[...]
    r0 = jnp.sum(jnp.where(oh0, before, 0.0), axis=-1, keepdims=True)
    r1 = jnp.sum(jnp.where(oh1, before, 0.0), axis=-1, keepdims=True)
    rank_ref[...] = jnp.where(lane == 0, r0, jnp.where(lane == 1, r1, 0.0)).astype(I32)
    total = base_ref[...] + jnp.sum(cnt, axis=0, keepdims=True)
    base_ref[...] = total
    cnt_ref[...] = total.astype(I32)


def _rank(ri):
    t = ri.shape[0]
    tm = _pick(t, (256, 128, 64, 32, 16, 8))
    return pl.pallas_call(
        _rank_kernel,
        grid=(t // tm,),
        in_specs=[pl.BlockSpec((tm, LANES), lambda i: (i, 0))],
        out_specs=[pl.BlockSpec((tm, LANES), lambda i: (i, 0)), pl.BlockSpec((1, LANES), lambda i: (0, 0))],
        out_shape=[jax.ShapeDtypeStruct((t, LANES), I32), jax.ShapeDtypeStruct((1, LANES), I32)],
        scratch_shapes=[pltpu.VMEM((1, LANES), F32)],
        compiler_params=_cparams(("arbitrary",)),
        name="moe_rank",
    )(ri)


def _gather_tiles(idx_ref, n, src_hbm, buf, slot, sem):
    def body(j, carry):
        for queue in range(2):
            r = 2 * j + queue
            src = pl.multiple_of(idx_ref[0, r] * SUBLANES, SUBLANES)
            dst = pl.multiple_of((slot * n + r) * SUBLANES, SUBLANES)
            pltpu.make_async_copy(src_hbm.at[pl.ds(src, SUBLANES)], buf.at[pl.ds(dst, SUBLANES)],
                                  sem.at[slot]).start(priority=queue)
        return carry

    lax.fori_loop(0, n // 2, body, 0, unroll=4)


def _gather_wait(n, src_hbm, buf, slot, sem):
    base = pl.multiple_of(slot * n * SUBLANES, SUBLANES)
    pltpu.make_async_copy(src_hbm.at[pl.ds(0, n * SUBLANES)], buf.at[pl.ds(base, n * SUBLANES)], sem.at[slot]).wait()
    return base


def _expert_kernel(te_ref, nu_ref, idc_ref, idn_ref, xt_hbm, wg_ref, wu_ref, wd_ref, o_ref,
                   xbuf, wgb, wub, wdb, sem):
    i = pl.program_id(0)
    n_used = nu_ref[0]
    slot = lax.rem(i, 2)

    @pl.when((i == 0) & (n_used > 0))
    def _():
        _gather_tiles(idc_ref, MOE_TILE, xt_hbm, xbuf, 0, sem)

    @pl.when(i + 1 < n_used)
    def _():
        _gather_tiles(idn_ref, MOE_TILE, xt_hbm, xbuf, 1 - slot, sem)

    @pl.when((i == 0) | (te_ref[i] != te_ref[jnp.maximum(i - 1, 0)]))
    def _():
        wgb[...] = wg_ref[...].astype(BF16)
        wub[...] = wu_ref[...].astype(BF16)
        wdb[...] = wd_ref[...].astype(BF16)

    @pl.when(i < n_used)
    def _():
        base = _gather_wait(MOE_TILE, xt_hbm, xbuf, slot, sem)
        x = _tiles_to_rows(xbuf, base, MOE_TILE).astype(BF16)
        hg = jnp.dot(x, wgb[...], preferred_element_type=F32)
        hu = jnp.dot(x, wub[...], preferred_element_type=F32)
        hidden = (_silu(hg) * hu).astype(BF16)
        _rows_to_tiles(o_ref, jnp.dot(hidden, wdb[...], preferred_element_type=F32))

    @pl.when(i >= n_used)
    def _():
        o_ref[...] = jnp.zeros_like(o_ref)


def _expert_ffn(xt, token_of_pos, tile_expert, n_used, e_gate, e_up, e_down):
    nt = token_of_pos.shape[0] // MOE_TILE
    idx = token_of_pos.reshape(nt, 1, MOE_TILE)
    tile_rows = MOE_TILE * SUBLANES
    smem = lambda imap: pl.BlockSpec((None, 1, MOE_TILE), imap, memory_space=pltpu.SMEM)
    grid_spec = pltpu.PrefetchScalarGridSpec(
        num_scalar_prefetch=2,
        grid=(nt,),
        in_specs=[
            smem(lambda i, te, nu: (i, 0, 0)),
            smem(lambda i, te, nu: (jnp.minimum(i + 1, nt - 1), 0, 0)),
            pl.BlockSpec(memory_space=pl.ANY),
            pl.BlockSpec((None, D_MODEL, D_EXPERT), lambda i, te, nu: (te[i], 0, 0)),
            pl.BlockSpec((None, D_MODEL, D_EXPERT), lambda i, te, nu: (te[i], 0, 0)),
            pl.BlockSpec((None, D_EXPERT, D_MODEL), lambda i, te, nu: (te[i], 0, 0)),
        ],
        out_specs=pl.BlockSpec((tile_rows, LANES), lambda i, te, nu: (i, 0)),
        scratch_shapes=[pltpu.VMEM((2 * tile_rows, LANES), F32),
                        pltpu.VMEM((D_MODEL, D_EXPERT), BF16), pltpu.VMEM((D_MODEL, D_EXPERT), BF16),
                        pltpu.VMEM((D_EXPERT, D_MODEL), BF16), pltpu.SemaphoreType.DMA((2,))],
    )
    return pl.pallas_call(
        _expert_kernel,
        grid_spec=grid_spec,
        out_shape=jax.ShapeDtypeStruct((nt * tile_rows, LANES), F32),
        compiler_params=_cparams(("arbitrary",)),
        name="expert_ffn",
    )(tile_expert, n_used, idx, idx, xt, e_gate, e_up, e_down)


def _combine_kernel(pc_ref, pn_ref, x_ref, rw_ref, ot_hbm, *refs, tm, n_first_tiles):
    i = pl.program_id(0)
    nt = pl.num_programs(0)
    slot = lax.rem(i, 2)
    n = TOP_K * tm
    obuf, sem = refs[-2:]

    @pl.when(i == 0)
    def _():
        _gather_tiles(pc_ref, n, ot_hbm, obuf, 0, sem)

    @pl.when(i + 1 < nt)
    def _():
        _gather_tiles(pn_ref, n, ot_hbm, obuf, 1 - slot, sem)

    base = _gather_wait(n, ot_hbm, obuf, slot, sem)
    rw = rw_ref[...]
    o0 = _tiles_to_rows(obuf, base, tm)
    o1 = _tiles_to_rows(obuf, base + tm * SUBLANES, tm)
    y = x_ref[...] + rw[:, 0:1] * o0 + rw[:, 1:2] * o1
    if n_first_tiles is None:
        refs[0][...] = y
    else:
        @pl.when(i < n_first_tiles)
        def _():
            refs[0][...] = y

        @pl.when(i >= n_first_tiles)
        def _():
            refs[1][...] = y


def _combine(x, out_tiles, pos, rw, split_rows=None):
    t = x.shape[0]
    rows = lambda w: pl.BlockSpec((tm, w), lambda i: (i, 0))
    if split_rows is None:
        tm = _pick(t, (256, 128, 64, 32, 16, 8))
        nft = None
        out_specs = rows(D_MODEL)
        out_shape = jax.ShapeDtypeStruct((t, D_MODEL), F32)
    else:
        tm = _pick(t - split_rows, (256, 128, 64, 32, 16, 8))
        assert split_rows % tm == 0
        nft = split_rows // tm
        out_specs = [pl.BlockSpec((tm, D_MODEL), lambda i: (jnp.minimum(i, nft - 1), 0)),
                     pl.BlockSpec((tm, D_MODEL), lambda i: (jnp.maximum(i - nft, 0), 0))]
        out_shape = [jax.ShapeDtypeStruct((split_rows, D_MODEL), F32),
                     jax.ShapeDtypeStruct((t - split_rows, D_MODEL), F32)]
    nt = t // tm
    n = TOP_K * tm
    idx = pos.reshape(nt, tm, TOP_K).transpose(0, 2, 1).reshape(nt, 1, n)
    kern = functools.partial(_combine_kernel, tm=tm, n_first_tiles=nft)
    smem = lambda imap: pl.BlockSpec((None, 1, n), imap, memory_space=pltpu.SMEM)
    return pl.pallas_call(
        kern,
        grid=(nt,),
        in_specs=[smem(lambda i: (i, 0, 0)), smem(lambda i: (jnp.minimum(i + 1, nt - 1), 0, 0)),
                  rows(D_MODEL), rows(LANES), pl.BlockSpec(memory_space=pl.ANY)],
        out_specs=out_specs,
        out_shape=out_shape,
        scratch_shapes=[pltpu.VMEM((2 * n * SUBLANES, LANES), F32), pltpu.SemaphoreType.DMA((2,))],
        compiler_params=_cparams(("arbitrary",)),
        name="moe_combine",
    )(idx, idx, x, rw, out_tiles)


def _route_tables(eid, rank, counts):
    t = eid.shape[0]
    padded = ((counts + MOE_TILE - 1) // MOE_TILE) * MOE_TILE
    ends = jnp.cumsum(padded)
    starts = ends - padded
    experts = jnp.arange(N_EXPERTS, dtype=I32)
    pos = jnp.sum(jnp.where(eid[:, :, None] == experts, starts, 0), axis=-1) + rank
    np_rows = -(-(TOP_K * t + N_EXPERTS * MOE_TILE) // MOE_TILE) * MOE_TILE
    token_of_pos = jnp.zeros((np_rows,), I32).at[pos.reshape(-1)].set(jnp.arange(TOP_K * t, dtype=I32) // TOP_K)
    tile_start = jnp.arange(np_rows // MOE_TILE, dtype=I32) * MOE_TILE
    tile_expert = jnp.minimum(jnp.sum(tile_start[:, None] >= ends[None, :], axis=1), N_EXPERTS - 1).astype(I32)
    n_used = (ends[-1] // MOE_TILE).astype(I32).reshape(1)
    return pos.astype(I32), token_of_pos, tile_expert, n_used


def _moe(y, xt, ri, rw, e_gate, e_up, e_down, split_rows):
    rank, counts = _rank(ri)
    pos, token_of_pos, tile_expert, n_used = _route_tables(ri[:, :TOP_K], rank[:, :TOP_K], counts[0, :N_EXPERTS])
    out_tiles = _expert_ffn(xt, token_of_pos, tile_expert, n_used, e_gate, e_up, e_down)
    return _combine(y, out_tiles, pos, rw, split_rows)


def _in_weights(w_in):
    half = C_ROPE // 2
    zeros = lambda n: jnp.zeros((D_MODEL, n), F32)
    c0 = A_WIDTH + B_WIDTH
    kr = w_in[:, c0 + C_Q_LORA + C_KV_LORA:c0 + C_WIDTH]
    kr_rot = jnp.concatenate([-kr[:, half:], kr[:, :half]], axis=1)
    tail = LANES - C_NOPE - C_ROPE
    cols = [
        w_in[:, :A_CONV_CH + A_V], w_in[:, A_CONV_CH + A_V:A_WIDTH], zeros(LANES - 2 * A_HEADS),
        w_in[:, A_WIDTH:c0],
        w_in[:, c0:c0 + C_Q_LORA + C_KV_LORA],
        zeros(C_NOPE), kr, zeros(tail),
        zeros(C_NOPE), kr_rot, zeros(tail),
    ]
    return jnp.concatenate(cols, axis=1)


def _router_weights(rg, rgb, re, reb):
    w = jnp.zeros((D_MODEL, LANES), F32).at[:, :N_GROUPS].set(rg).at[:, N_GROUPS:N_GROUPS + N_EXPERTS].set(re)
    b = jnp.zeros((1, LANES), F32).at[0, :N_GROUPS].set(rgb).at[0, N_GROUPS:N_GROUPS + N_EXPERTS].set(reb)
    return w, b


def _layer(xs, geom, st_p, st_s, cos, sin, wts, precise, split_out):
    (norm_mix, w_in, a_conv_w, a_A_log, a_dt_bias, a_norm_w,
     b_mu, b_w0, b_w2, b_a0, b_a2, b_g2, b_k_k, b_k_a, b_r_k, b_ln_w, b_ln_b,
     c_q_norm, c_kv_norm, c_w_uq, c_w_ukv, c_q_gain, c_k_gain,
     w_out, norm_ffn, router_group, router_group_bias, router_expert, router_expert_bias,
     e_gate, e_up, e_down) = wts
    bp, lp, bs, ls, past = geom
    tp = bp * lp

    pa, pb, pc = _in_proj(xs, norm_mix, _in_weights(w_in), precise)
    cw = _c_weights(c_q_norm, c_kv_norm, c_w_uq, c_w_ukv, c_q_gain, c_k_gain)
    q, k, v, ckv, krope = _c_prep(pc, cos, sin, cw, precise)

    a_args = (a_conv_w, a_A_log, a_dt_bias, a_norm_w)
    b_args = (b_mu, b_w0, b_w2, b_a0, b_a2, b_g2, b_k_k, b_k_a, b_r_k.reshape(-1), b_ln_w, b_ln_b)
    conv_p, delta_p, shift_p, wkv_p = st_p
    conv_s, delta_s, shift_s, wkv_s, ckv_past, krope_past = st_s

    oa_p, conv_np, delta_np = _mixer_a(pa, 0, bp, lp, precise, conv_p, delta_p, *a_args)
    oa_s, conv_ns, delta_ns = _mixer_a(pa, tp, bs, ls, precise, conv_s, delta_s, *a_args)
    ob_p, shift_np, wkv_np = _mixer_b(pb, 0, bp, lp, precise, shift_p, wkv_p, *b_args)
    ob_s, shift_ns, wkv_ns = _mixer_b(pb, tp, bs, ls, precise, shift_s, wkv_s, *b_args)

    oc_p = _attention(q, k, v, 0, bp, lp, precise)
    krp_past = jnp.pad(krope_past.reshape(bs * past, C_ROPE), ((0, 0), (C_NOPE, LANES - C_NOPE - C_ROPE)))
    k_cache, v_cache = _kv_expand_call(ckv_past.reshape(bs * past, C_KV_LORA), krp_past, cw, precise)
    oc_s = _attention(q, k, v, tp, bs, ls, precise, k_cache, v_cache)

    w_router, b_router = _router_weights(router_group, router_group_bias, router_expert, router_expert_bias)
    y, xt, ri, rw = _out_proj(xs, (oa_p, ob_p, oc_p), (oa_s, ob_s, oc_s), w_out, norm_ffn,
                              w_router, b_router, precise)
    x_new = _moe(y, xt, ri, rw, e_gate, e_up, e_down, tp if split_out else None)

    new_p = (conv_np, delta_np, shift_np, wkv_np, ckv[:tp].reshape(bp, lp, C_KV_LORA),
             krope[:tp].reshape(bp, lp, C_ROPE))
    new_s = (conv_ns, delta_ns, shift_ns, wkv_ns, ckv[tp:].reshape(bs, ls, C_KV_LORA),
             krope[tp:].reshape(bs, ls, C_ROPE))
    return x_new, new_p, new_s


def _forward(x_prompt, x_sample, cache_c_kv, cache_k_rope, state_conv_a, state_delta_a, state_shift_b,
             state_wkv_b, weights):
    bp, lp, _ = x_prompt.shape
    bs, ls, _ = x_sample.shape
    depth = cache_c_kv.shape[0]
    past = cache_c_kv.shape[2]
    geom = (bp, lp, bs, ls, past)
    xs = [x_prompt.reshape(bp * lp, D_MODEL), x_sample.reshape(bs * ls, D_MODEL)]
    pos = jnp.concatenate([jnp.tile(jnp.arange(lp), bp), jnp.tile(past + jnp.arange(ls), bs)])
    cos, sin = _rope_tables(pos)
    zeros = lambda *s: jnp.zeros(s, F32)
    st_p = (zeros(bp, A_CONV - 1, A_CONV_CH), zeros(bp, A_HEADS, A_DK, A_DV), zeros(bp, 1, B_WIDTH),
            zeros(bp, B_HEADS, B_N, B_N))
    news_p, news_s = [], []
    for l in range(depth):
        st_s = (state_conv_a[l], state_delta_a[l], state_shift_b[l], state_wkv_b[l], cache_c_kv[l], cache_k_rope[l])
        last = l == depth - 1
        x, new_p, new_s = _layer(xs, geom, st_p, st_s, cos, sin, [w[l] for w in weights], precise=not last,
                                 split_out=last)
        xs = x if last else [x]
        news_p.append(new_p)
        news_s.append(new_s)
    stack = lambda news, i: jnp.stack([n[i] for n in news])
    y_prompt = xs[0].reshape(bp, lp, D_MODEL)
    y_sample = xs[1].reshape(bs, ls, D_MODEL)
    p_conv, p_delta, p_shift, p_wkv, p_ckv, p_krope = (stack(news_p, i) for i in range(6))
    s_conv, s_delta, s_shift, s_wkv, s_ckv, s_krope = (stack(news_s, i) for i in range(6))
    return (y_prompt, y_sample, p_ckv, p_krope, p_conv, p_delta, p_shift, p_wkv,
            s_ckv, s_krope, s_conv, s_delta, s_shift, s_wkv)


def kernel(x_prompt, x_sample, cache_c_kv, cache_k_rope, state_conv_a, state_delta_a, state_shift_b, state_wkv_b,
           norm_mix, w_in, a_conv_w, a_A_log, a_dt_bias, a_norm_w,
           b_mu, b_w0, b_w2, b_a0, b_a2, b_g2, b_k_k, b_k_a, b_r_k, b_ln_w, b_ln_b,
           c_q_norm, c_kv_norm, c_w_uq, c_w_ukv, c_q_gain, c_k_gain,
           w_out, norm_ffn, router_group, router_group_bias, router_expert, router_expert_bias,
           e_gate, e_up, e_down):
    weights = (norm_mix, w_in, a_conv_w, a_A_log, a_dt_bias, a_norm_w,
               b_mu, b_w0, b_w2, b_a0, b_a2, b_g2, b_k_k, b_k_a, b_r_k, b_ln_w, b_ln_b,
               c_q_norm, c_kv_norm, c_w_uq, c_w_ukv, c_q_gain, c_k_gain,
               w_out, norm_ffn, router_group, router_group_bias, router_expert, router_expert_bias,
               e_gate, e_up, e_down)
    return _forward(x_prompt, x_sample, cache_c_kv, cache_k_rope, state_conv_a, state_delta_a, state_shift_b,
                    state_wkv_b, weights)
```

```python
import functools

import jax
import jax.numpy as jnp
from jax import lax
from jax.experimental import pallas as pl
from jax.experimental.pallas import tpu as pltpu

F32 = jnp.float32
BF16 = jnp.bfloat16
I32 = jnp.int32

D_MODEL = 1024
DEPTH = 2
CHUNK = 64
EPS = 1e-6
LANES = 128
SUBLANES = 8
ROW_TILES = D_MODEL // LANES

A_HEADS, A_DK, A_DV, A_CONV = 6, 64, 64, 4
A_QK = A_HEADS * A_DK
A_V = A_HEADS * A_DV
A_CONV_CH = 2 * A_QK + A_V
A_WIDTH = A_CONV_CH + A_V + 2 * A_HEADS
A_SEG = A_CONV_CH + A_V + LANES

B_HEADS, B_N = 6, 64
B_C = B_HEADS * B_N
B_W_LORA, B_A_LORA, B_G_LORA = 64, 64, 128
B_WIDTH = 3 * B_C + B_W_LORA + B_A_LORA + B_G_LORA
B_GN_EPS = 64e-5

C_HEADS, C_NOPE, C_ROPE, C_VDIM = 4, 64, 32, 64
C_QK = C_NOPE + C_ROPE
C_Q_LORA, C_KV_LORA = 256, 128
C_WIDTH = C_Q_LORA + C_KV_LORA + C_ROPE
C_SEG = C_Q_LORA + C_KV_LORA + 2 * LANES
C_HPAD = LANES
ROPE_THETA = 10000.0

P_TOTAL = A_WIDTH + B_WIDTH + C_WIDTH
D_MIX = A_V + B_C + C_HEADS * C_VDIM

N_GROUPS, EXPERTS_PER_GROUP = 4, 8
N_EXPERTS = N_GROUPS * EXPERTS_PER_GROUP
TOP_K = 2
D_EXPERT = 256
MOE_TILE = 256
MIXER_CHUNKS_PER_STEP = 4
ATTN_KEY_BLOCK = 512

VMEM_LIMIT = 48 * 1024 * 1024

_NN = (((1,), (0,)), ((), ()))
_NT = (((1,), (1,)), ((), ()))
_TN = (((0,), (0,)), ((), ()))


def _pick(n, prefs):
    for p in prefs:
        if n % p == 0:
            return p
    raise ValueError(f"no tile for {n} in {prefs}")


def _cparams(sem):
    return pltpu.CompilerParams(dimension_semantics=sem, vmem_limit_bytes=VMEM_LIMIT)


def _bdot(a, b, dims=_NN):
    return lax.dot_general(a.astype(BF16), b.astype(BF16), dims, preferred_element_type=F32)


def _split2(a):
    hi = a.astype(BF16)
    lo = (a - hi.astype(F32)).astype(BF16)
    return hi, lo


def _dot3(a, b, dims=_NN):
    ah, al = _split2(a)
    bh, bl = _split2(b)
    f = lambda x, y: lax.dot_general(x, y, dims, preferred_element_type=F32)
    return f(ah, bh) + (f(ah, bl) + f(al, bh))


def _mdot(precise):
    return _dot3 if precise else _bdot


def _cumsum_rows(x, ltri):
    h = x.astype(BF16)
    r = x - h.astype(F32)
    m = r.astype(BF16)
    l = (r - m.astype(F32)).astype(BF16)
    d = lambda y: jnp.dot(ltri, y, preferred_element_type=F32)
    return d(h) + (d(m) + d(l))


def _head_sums(x, width):
    n = x.shape[1]
    shift = width.bit_length() - 1
    row = jnp.right_shift(lax.broadcasted_iota(I32, (n, n), 0), shift)
    col = jnp.right_shift(lax.broadcasted_iota(I32, (n, n), 1), shift)
    ones_bd = (row == col).astype(BF16)
    h = x.astype(BF16)
    r = x - h.astype(F32)
    m = r.astype(BF16)
    l = (r - m.astype(F32)).astype(BF16)
    d = lambda y: jnp.dot(y, ones_bd, preferred_element_type=F32)
    return d(h) + (d(m) + d(l))


def _tri_masks(c):
    row = lax.broadcasted_iota(I32, (c, c), 0)
    col = lax.broadcasted_iota(I32, (c, c), 1)
    return row >= col, row > col, (row == col).astype(F32)


def _chunk_tril(rows, c):
    shift = c.bit_length() - 1
    row = lax.broadcasted_iota(I32, (rows, rows), 0)
    col = lax.broadcasted_iota(I32, (rows, rows), 1)
    same = jnp.right_shift(row, shift) == jnp.right_shift(col, shift)
    return (same & (row >= col)).astype(BF16)


def _neumann_inv_many(ns, eye):
    c = eye.shape[0]
    keep_t = lax.broadcasted_iota(I32, (c, 2 * c), 1) >= c
    pts = [jnp.concatenate([n, eye], axis=1) for n in ns]
    m = 1
    while m < c:
        rs = [_dot3(pt[:, :c], pt) for pt in pts]
        pts = [r + jnp.where(keep_t, pt, 0.0) for r, pt in zip(rs, pts)]
        m *= 2
    return [pt[:, c:] for pt in pts]


def _softplus(x):
    return jnp.maximum(x, 0.0) + jnp.log1p(jnp.exp(-jnp.abs(x)))


def _silu(x):
    return x * jax.nn.sigmoid(x)


def _rms(x, eps=EPS):
    return x * lax.rsqrt(jnp.mean(x * x, axis=-1, keepdims=True) + eps)


def _rows_to_tiles(ref, val):
    n = val.shape[0]
    for j in range(ROW_TILES):
        ref[pl.ds(j, n, stride=SUBLANES), :] = val[:, j * LANES:(j + 1) * LANES]


def _tiles_to_rows(ref, start, n):
    return jnp.concatenate([ref[pl.ds(start + j, n, stride=SUBLANES), :] for j in range(ROW_TILES)], axis=1)


def _in_proj_kernel(*refs, precise, n_first_tiles):
    if n_first_tiles is None:
        x = refs[0][...]
        g_ref, *refs = refs[1:]
    else:
        x = jnp.where(pl.program_id(0) < n_first_tiles, refs[0][...], refs[1][...])
        g_ref, *refs = refs[2:]
    if precise:
        wh_ref, wl_ref, pa_ref, pb_ref, pc_ref = refs
    else:
        wh_ref, pa_ref, pb_ref, pc_ref = refs
    xn = _rms(x) * g_ref[...]
    xh = xn.astype(BF16)
    if precise:
        xl = (xn - xh.astype(F32)).astype(BF16)
    lo = 0
    for out_ref, width in ((pa_ref, A_SEG), (pb_ref, B_WIDTH), (pc_ref, C_SEG)):
        acc = jnp.dot(xh, wh_ref[:, lo:lo + width], preferred_element_type=F32)
        if precise:
            acc = acc + (jnp.dot(xh, wl_ref[:, lo:lo + width], preferred_element_type=F32)
                         + jnp.dot(xl, wh_ref[:, lo:lo + width], preferred_element_type=F32))
        out_ref[...] = acc
        lo += width


def _split_kernel(w_ref, hi_ref, lo_ref):
    hi, lo = _split2(w_ref[...])
    hi_ref[...] = hi
    lo_ref[...] = lo


def _split_hi_lo(w):
    r, c = w.shape
    tc = _pick(c, (512, 256, 128))
    spec = pl.BlockSpec((r, tc), lambda j: (0, j))
    return pl.pallas_call(
        _split_kernel,
        grid=(c // tc,),
        in_specs=[spec],
        out_specs=[spec, spec],
        out_shape=[jax.ShapeDtypeStruct((r, c), BF16), jax.ShapeDtypeStruct((r, c), BF16)],
        compiler_params=_cparams(("arbitrary",)),
        name="split_hi_lo",
    )(w)


def _in_proj(xs, g, w, precise):
    prefs = (256, 128, 64, 32, 16, 8) if precise else (512, 256, 128, 64, 32, 16, 8)
    t = sum(x.shape[0] for x in xs)
    if len(xs) == 1:
        tm = _pick(t, prefs)
        nft = None
        x_specs = [pl.BlockSpec((tm, D_MODEL), lambda i: (i, 0))]
    else:
        tm = _pick(xs[1].shape[0], prefs)
        assert xs[0].shape[0] % tm == 0
        nft = xs[0].shape[0] // tm
        x_specs = [pl.BlockSpec((tm, D_MODEL), lambda i: (jnp.minimum(i, nft - 1), 0)),
                   pl.BlockSpec((tm, D_MODEL), lambda i: (jnp.maximum(i - nft, 0), 0))]
    wtot = A_SEG + B_WIDTH + C_SEG
    wspec = pl.BlockSpec((D_MODEL, wtot), lambda i: (0, 0))
    w_args = _split_hi_lo(w) if precise else [w.astype(BF16)]
    return pl.pallas_call(
        functools.partial(_in_proj_kernel, precise=precise, n_first_tiles=nft),
        grid=(t // tm,),
        in_specs=x_specs + [pl.BlockSpec((1, D_MODEL), lambda i: (0, 0))] + [wspec] * len(w_args),
        out_specs=[
            pl.BlockSpec((tm, A_SEG), lambda i: (i, 0)),
            pl.BlockSpec((tm, B_WIDTH), lambda i: (i, 0)),
            pl.BlockSpec((tm, C_SEG), lambda i: (i, 0)),
        ],
        out_shape=[
            jax.ShapeDtypeStruct((t, A_SEG), F32),
            jax.ShapeDtypeStruct((t, B_WIDTH), F32),
            jax.ShapeDtypeStruct((t, C_SEG), F32),
        ],
        compiler_params=_cparams(("arbitrary",)),
        name="in_proj",
    )(*xs, g.reshape(1, D_MODEL), *w_args)


def _mixer_a_kernel(pa_ref, cprev_ref, s0_ref, cw_ref, alog_ref, dtb_ref, nw_ref,
                    oa_ref, cnew_ref, snew_ref, tail_ref, s_ref, *, c, nch, n_steps, precise):
    dot = _mdot(precise)
    step = pl.program_id(1)
    rows = c * nch

    @pl.when(step == 0)
    def _():
        tail_ref[...] = jnp.zeros_like(tail_ref)
        tail_ref[SUBLANES - (A_CONV - 1):SUBLANES, :] = cprev_ref[...]
        s_ref[...] = s0_ref[...]

    x = pa_ref[...]
    qkv = x[:, :A_CONV_CH]
    z = x[:, A_CONV_CH:A_CONV_CH + A_V]
    ab = x[:, A_CONV_CH + A_V:A_SEG]

    xp = jnp.concatenate([tail_ref[...], qkv], axis=0)
    cw = cw_ref[...]
    y = qkv * cw[A_CONV - 1:A_CONV]
    for j in range(A_CONV - 1):
        o = SUBLANES - (A_CONV - 1) + j
        y = y + xp[o:o + rows] * cw[j:j + 1]
    tail_ref[...] = qkv[rows - SUBLANES:rows]

    @pl.when(step == n_steps - 1)
    def _():
        cnew_ref[...] = qkv[rows - (A_CONV - 1):rows]

    act = _silu(y)
    g_all = -jnp.exp(alog_ref[...]) * _softplus(ab + dtb_ref[...])
    beta_all = jax.nn.sigmoid(ab)
    gc_all = _cumsum_rows(g_all, _chunk_tril(rows, c))
    gc_t = gc_all.T
    tril, strict, eye = _tri_masks(c)
    nw = nw_ref[...]

    q_all = act[:, 0:A_QK]
    k_all = act[:, A_QK:2 * A_QK]
    q_all = q_all * lax.rsqrt(_head_sums(q_all * q_all, A_DK) + 1e-6) * (A_DK ** -0.5)
    k_all = k_all * lax.rsqrt(_head_sums(k_all * k_all, A_DK) + 1e-6)
    qn = [q_all[:, h * A_DK:(h + 1) * A_DK] for h in range(A_HEADS)]
    kn = [k_all[:, h * A_DK:(h + 1) * A_DK] for h in range(A_HEADS)]
    vv = [act[:, 2 * A_QK + h * A_DV:2 * A_QK + (h + 1) * A_DV] for h in range(A_HEADS)]

    items = [(ci, h) for ci in range(nch) for h in range(A_HEADS)]
    q_i, k_i, kb_i, vb_i, dec_i, gcc_i, gl_i = [], [], [], [], [], [], []
    for ci, h in items:
        r0 = ci * c
        beta = beta_all[r0:r0 + c, A_HEADS + h:A_HEADS + h + 1]
        gcc = gc_all[r0:r0 + c, h:h + 1]
        gcr = gc_t[h:h + 1, r0:r0 + c]
        k = kn[h][r0:r0 + c]
        q_i.append(qn[h][r0:r0 + c])
        k_i.append(k)
        kb_i.append(k * beta)
        vb_i.append(vv[h][r0:r0 + c] * beta)
        dec_i.append(jnp.exp(jnp.where(tril, gcc - gcr, -1e30)))
        gcc_i.append(gcc)
        gl_i.append(gc_all[r0 + c - 1:r0 + c, h:h + 1])
    a_i = [jnp.where(strict, dot(kb, k, _NT) * dec, 0.0) for kb, k, dec in zip(kb_i, k_i, dec_i)]
    t_i = _neumann_inv_many([-a for a in a_i], eye)
    egc_i = [jnp.exp(g) for g in gcc_i]
    sol_i = [_dot3(t, jnp.concatenate([vb, kb * e], axis=1)) for t, vb, kb, e in zip(t_i, vb_i, kb_i, egc_i)]
    attn_i = [jnp.where(tril, dot(q, k, _NT) * dec, 0.0) for q, k, dec in zip(q_i, k_i, dec_i)]
    kq_i = [jnp.concatenate([sol[:, A_DV:], q * e], axis=0) for sol, q, e in zip(sol_i, q_i, egc_i)]
    kg_i = [k * jnp.exp(gl - g) for k, gl, g in zip(k_i, gl_i, gcc_i)]

    s = [s_ref[h] for h in range(A_HEADS)]
    out_rows = []
    for ci in range(nch):
        ids = [ci * A_HEADS + h for h in range(A_HEADS)]
        ks = [dot(kq_i[i], s[h]) for h, i in enumerate(ids)]
        u = [sol_i[i][:, :A_DV] - ks[h][:c] for h, i in enumerate(ids)]
        o = [ks[h][c:] + dot(attn_i[i], u[h]) for h, i in enumerate(ids)]
        s = [s[h] * jnp.exp(gl_i[i]) + dot(kg_i[i], u[h], _TN) for h, i in enumerate(ids)]
        out_rows.append(jnp.concatenate(o, axis=1))
    for h in range(A_HEADS):
        s_ref[h] = s[h]
    o_all = jnp.concatenate(out_rows, axis=0)
    ms = _head_sums(o_all * o_all, A_DV) * (1.0 / A_DV)
    oa_ref[...] = (o_all * lax.rsqrt(ms + EPS) * nw * _silu(z)).astype(oa_ref.dtype)

    @pl.when(step == n_steps - 1)
    def _():
        snew_ref[...] = s_ref[...]


def _row(v, width=LANES):
    v = v.reshape(1, -1).astype(F32)
    return jnp.pad(v, ((0, 0), (0, width - v.shape[1])))


def _mixer_geometry(row0, seq):
    c = min(CHUNK, seq)
    nch = _pick(seq // c, (MIXER_CHUNKS_PER_STEP, 2, 1))
    rows = c * nch
    assert row0 % rows == 0
    return c, nch, rows, seq // rows, row0 // rows


def _mixer_a(pa, row0, bsz, seq, precise, conv_prev, s0, conv_w, a_log, dt_bias, norm_w):
    c, nch, rows, n_steps, blk0 = _mixer_geometry(row0, seq)
    kern = functools.partial(_mixer_a_kernel, c=c, nch=nch, n_steps=n_steps, precise=precise)
    full = lambda shape: pl.BlockSpec(shape, lambda b, n: (0,) * len(shape))
    return pl.pallas_call(
        kern,
        grid=(bsz, n_steps),
        in_specs=[
            pl.BlockSpec((rows, A_SEG), lambda b, n: (blk0 + b * n_steps + n, 0)),
            pl.BlockSpec((None, A_CONV - 1, A_CONV_CH), lambda b, n: (b, 0, 0)),
            pl.BlockSpec((None, A_HEADS, A_DK, A_DV), lambda b, n: (b, 0, 0, 0)),
            full((A_CONV, A_CONV_CH)),
            full((1, LANES)),
            full((1, LANES)),
            full((1, A_V)),
        ],
        out_specs=[
            pl.BlockSpec((rows, A_V), lambda b, n: (b * n_steps + n, 0)),
            pl.BlockSpec((None, A_CONV - 1, A_CONV_CH), lambda b, n: (b, 0, 0)),
            pl.BlockSpec((None, A_HEADS, A_DK, A_DV), lambda b, n: (b, 0, 0, 0)),
        ],
        out_shape=[
            jax.ShapeDtypeStruct((bsz * seq, A_V), F32 if precise else BF16),
            jax.ShapeDtypeStruct((bsz, A_CONV - 1, A_CONV_CH), F32),
            jax.ShapeDtypeStruct((bsz, A_HEADS, A_DK, A_DV), F32),
        ],
        scratch_shapes=[pltpu.VMEM((SUBLANES, A_CONV_CH), F32), pltpu.VMEM((A_HEADS, A_DK, A_DV), F32)],
        compiler_params=_cparams(("arbitrary", "arbitrary")),
        name="mixer_a",
    )(pa, conv_prev, s0, conv_w, _row(a_log), _row(dt_bias), jnp.tile(norm_w, A_HEADS).reshape(1, A_V))


def _mixer_b_kernel(pb_ref, sprev_ref, s0_ref, mu_ref, w0_ref, w2_ref, a0_ref, a2_ref, g2_ref,
                    kk_ref, ka_ref, rk_ref, lnw_ref, lnb_ref,
                    ob_ref, shnew_ref, snew_ref, last_ref, s_ref, *, c, nch, n_steps, precise):
    dot = _mdot(precise)
    step = pl.program_id(1)
    rows = c * nch

    @pl.when(step == 0)
    def _():
        last_ref[...] = sprev_ref[...]
        s_ref[...] = s0_ref[...]

    p = pb_ref[...]
    rowi = lax.broadcasted_iota(I32, (rows, 1), 0)
    prev = jnp.where(rowi == 0, last_ref[...], pltpu.roll(p, 1, 0))
    last_ref[...] = p[rows - 1:rows]

    @pl.when(step == n_steps - 1)
    def _():
        shnew_ref[...] = p[rows - 1:rows]

    xs = p + (prev - p) * mu_ref[...]
    r_all = xs[:, 0:B_C]
    k_all = xs[:, B_C:2 * B_C]
    v_all = xs[:, 2 * B_C:3 * B_C]
    xwa = xs[:, 3 * B_C:3 * B_C + B_W_LORA + B_A_LORA]
    xg = xs[:, 3 * B_C + B_W_LORA + B_A_LORA:B_WIDTH]

    w_log = -_softplus(-(w0_ref[...] + dot(jnp.tanh(xwa), w2_ref[...]))) - 0.5
    lw = -jnp.exp(w_log)
    rate = jax.nn.sigmoid(a0_ref[...] + dot(xwa, a2_ref[...]))
    gate = dot(jax.nn.sigmoid(xg), g2_ref[...])
    kkr = k_all * kk_ref[...]
    k2_all = k_all * (1.0 + (rate - 1.0) * ka_ref[...])

    cl = _cumsum_rows(lw, _chunk_tril(rows, c))
    e_cl = jnp.exp(cl)
    e_neg = jnp.exp(-cl)
    at_all = jnp.exp(cl - lw)
    bt_all = rate * e_neg
    kt_all = k2_all * e_neg
    rt_all = r_all * e_cl
    tril, strict, eye = _tri_masks(c)
    rk = rk_ref[...]
    lnw = lnw_ref[...]
    lnb = lnb_ref[...]
    kk_all = kkr * lax.rsqrt(_head_sums(kkr * kkr, B_N) + 1e-6)
    kkn = [kk_all[:, h * B_N:(h + 1) * B_N] for h in range(B_HEADS)]

    items = [(ci, h) for ci in range(nch) for h in range(B_HEADS)]
    at_i, bt_i, kt_i, rt_i, v_i = [], [], [], [], []
    for ci, h in items:
        rs = slice(ci * c, (ci + 1) * c)
        ls = slice(h * B_N, (h + 1) * B_N)
        kk = kkn[h][rs]
        at_i.append(-kk * at_all[rs, ls])
        bt_i.append(kk * bt_all[rs, ls])
        kt_i.append(kt_all[rs, ls])
        rt_i.append(rt_all[rs, ls])
        v_i.append(v_all[rs, ls])
    bk_i = [jnp.concatenate([b, k], axis=0) for b, k in zip(bt_i, kt_i)]
    ar_i = [jnp.concatenate([a, r], axis=0) for a, r in zip(at_i, rt_i)]
    row2 = lax.broadcasted_iota(I32, (c, 2 * c), 0)
    col2 = jnp.bitwise_and(lax.broadcasted_iota(I32, (c, 2 * c), 1), c - 1)
    strict2 = row2 > col2
    tril2 = row2 >= col2
    g_a = [jnp.where(strict2, _dot3(a, bk, _NT), 0.0) for a, bk in zip(at_i, bk_i)]
    g_r = [jnp.where(tril2, dot(r, bk, _NT), 0.0) for r, bk in zip(rt_i, bk_i)]
    t_i = _neumann_inv_many([g[:, :c] for g in g_a], eye)
    mv_i = [dot(g[:, c:], v) for g, v in zip(g_a, v_i)]

    s = [s_ref[h] for h in range(B_HEADS)]
    out_rows = []
    for ci in range(nch):
        ids = [ci * B_HEADS + h for h in range(B_HEADS)]
        rs = slice(ci * c, (ci + 1) * c)
        ars = [dot(ar_i[i], s[h], _NT) for h, i in enumerate(ids)]
        u = [_dot3(t_i[i], ars[h][:c] + mv_i[i]) for h, i in enumerate(ids)]
        uv = [jnp.concatenate([u[h], v_i[i]], axis=0) for h, i in enumerate(ids)]
        o = [ars[h][c:] + dot(g_r[i], uv[h]) for h, i in enumerate(ids)]
        e_last = e_cl[(ci + 1) * c - 1:(ci + 1) * c]
        s = [(s[h] + dot(uv[h], bk_i[i], _TN)) * e_last[:, h * B_N:(h + 1) * B_N] for h, i in enumerate(ids)]
        out_rows.append(jnp.concatenate(o, axis=1))
    for h in range(B_HEADS):
        s_ref[h] = s[h]
    o_all = jnp.concatenate(out_rows, axis=0)
    dev = o_all - _head_sums(o_all, B_N) * (1.0 / B_N)
    var = _head_sums(dev * dev, B_N) * (1.0 / B_N)
    gn = dev * lax.rsqrt(var + B_GN_EPS) * lnw + lnb
    bonus = _head_sums(r_all * k2_all * rk, B_N) * v_all
    ob_ref[...] = ((gn + bonus) * gate).astype(ob_ref.dtype)

    @pl.when(step == n_steps - 1)
    def _():
        snew_ref[...] = s_ref[...]


def _mixer_b(pb, row0, bsz, seq, precise, shift_prev, s0, mu, w0, w2, a0, a2, g2, k_k, k_a, r_k, ln_w, ln_b):
    c, nch, rows, n_steps, blk0 = _mixer_geometry(row0, seq)
    kern = functools.partial(_mixer_b_kernel, c=c, nch=nch, n_steps=n_steps, precise=precise)
    full = lambda shape: pl.BlockSpec(shape, lambda b, n: (0,) * len(shape))
    lora = B_W_LORA + B_A_LORA
    w2p = jnp.zeros((lora, B_C), F32).at[:B_W_LORA].set(w2)
    a2p = jnp.zeros((lora, B_C), F32).at[B_W_LORA:].set(a2)
    vec = lambda v: v.reshape(1, -1).astype(F32)
    return pl.pallas_call(
        kern,
        grid=(bsz, n_steps),
        in_specs=[
            pl.BlockSpec((rows, B_WIDTH), lambda b, n: (blk0 + b * n_steps + n, 0)),
            pl.BlockSpec((None, 1, B_WIDTH), lambda b, n: (b, 0, 0)),
            pl.BlockSpec((None, B_HEADS, B_N, B_N), lambda b, n: (b, 0, 0, 0)),
            full((1, B_WIDTH)),
            full((1, B_C)), full((lora, B_C)),
            full((1, B_C)), full((lora, B_C)),
            full((B_G_LORA, B_C)),
            full((1, B_C)), full((1, B_C)), full((1, B_C)), full((1, B_C)), full((1, B_C)),
        ],
        out_specs=[
            pl.BlockSpec((rows, B_C), lambda b, n: (b * n_steps + n, 0)),
            pl.BlockSpec((None, 1, B_WIDTH), lambda b, n: (b, 0, 0)),
            pl.BlockSpec((None, B_HEADS, B_N, B_N), lambda b, n: (b, 0, 0, 0)),
        ],
        out_shape=[
            jax.ShapeDtypeStruct((bsz * seq, B_C), F32 if precise else BF16),
            jax.ShapeDtypeStruct((bsz, 1, B_WIDTH), F32),
            jax.ShapeDtypeStruct((bsz, B_HEADS, B_N, B_N), F32),
        ],
        scratch_shapes=[pltpu.VMEM((1, B_WIDTH), F32), pltpu.VMEM((B_HEADS, B_N, B_N), F32)],
        compiler_params=_cparams(("arbitrary", "arbitrary")),
        name="mixer_b",
    )(pb, shift_prev, s0, vec(mu), vec(w0), w2p, vec(a0), a2p, g2,
      vec(k_k), vec(k_a), vec(r_k), vec(ln_w), vec(ln_b))


def _kv_expand(ckvn, krope, wuk_ref, wuv_ref, gk_ref, k_ref, v_ref, dot):
    k0 = dot(ckvn, wuk_ref[...])
    gk = gk_ref[...]
    for h in range(C_HEADS):
        kh = k0[:, h * C_HPAD:(h + 1) * C_HPAD] + krope
        ss = jnp.sum(kh * kh, axis=-1, keepdims=True) * (1.0 / C_QK)
        k_ref[:, h * C_HPAD:(h + 1) * C_HPAD] = (kh * lax.rsqrt(ss + EPS) * gk).astype(k_ref.dtype)
    v_ref[...] = dot(ckvn, wuv_ref[...]).astype(v_ref.dtype)


def _c_prep_kernel(pc_ref, cos_ref, sin_ref, qn_ref, kvn_ref, wq_ref, wqr_ref, wuk_ref, wuv_ref, gq_ref, gk_ref,
                   q_ref, k_ref, v_ref, ckv_ref, kr_ref, *, precise):
    dot = _mdot(precise)
    pc = pc_ref[...]
    cq = pc[:, 0:C_Q_LORA]
    ckv_raw = pc[:, C_Q_LORA:C_Q_LORA + C_KV_LORA]
    krp = pc[:, C_Q_LORA + C_KV_LORA:C_Q_LORA + C_KV_LORA + LANES]
    krr = pc[:, C_Q_LORA + C_KV_LORA + LANES:C_SEG]
    cos = cos_ref[...]
    sin = sin_ref[...]
    lane = lax.broadcasted_iota(I32, cos.shape, 1)
    cosq = jnp.where(lane < C_NOPE, 1.0, cos)

    cqn = _rms(cq) * qn_ref[...]
    q0 = dot(cqn, wq_ref[...])
    q1 = dot(cqn, wqr_ref[...])
    gq = gq_ref[...] * (C_QK ** -0.5)
    for h in range(C_HEADS):
        sl = slice(h * C_HPAD, (h + 1) * C_HPAD)
        qh = q0[:, sl] * cosq + q1[:, sl] * sin
        ss = jnp.sum(qh * qh, axis=-1, keepdims=True) * (1.0 / C_QK)
        q_ref[:, sl] = (qh * lax.rsqrt(ss + EPS) * gq).astype(q_ref.dtype)

    ckvn = _rms(ckv_raw) * kvn_ref[...]
    ckv_ref[...] = ckvn
    krope = krp * cos + krr * sin
    kr_ref[...] = krope[:, C_NOPE:C_NOPE + C_ROPE]
    _kv_expand(ckvn, krope, wuk_ref, wuv_ref, gk_ref, k_ref, v_ref, dot)


def _kv_expand_kernel(ckv_ref, kr_ref, wuk_ref, wuv_ref, gk_ref, k_ref, v_ref, *, precise):
    _kv_expand(ckv_ref[...], kr_ref[...], wuk_ref, wuv_ref, gk_ref, k_ref, v_ref, _mdot(precise))


def _c_weights(q_norm, kv_norm, w_uq, w_ukv, q_gain, k_gain):
    half = C_ROPE // 2
    wq = jnp.zeros((C_Q_LORA, C_HEADS * C_HPAD), F32)
    wqr = jnp.zeros((C_Q_LORA, C_HEADS * C_HPAD), F32)
    wuk = jnp.zeros((C_KV_LORA, C_HEADS * C_HPAD), F32)
    wuv = jnp.zeros((C_KV_LORA, C_HEADS * C_VDIM), F32)
    for h in range(C_HEADS):
        wh = w_uq[:, h * C_QK:(h + 1) * C_QK]
        wq = wq.at[:, h * C_HPAD:h * C_HPAD + C_QK].set(wh)
        rot = jnp.concatenate([-wh[:, C_NOPE + half:], wh[:, C_NOPE:C_NOPE + half]], axis=1)
        wqr = wqr.at[:, h * C_HPAD + C_NOPE:h * C_HPAD + C_QK].set(rot)
        kvh = w_ukv[:, h * (C_NOPE + C_VDIM):(h + 1) * (C_NOPE + C_VDIM)]
        wuk = wuk.at[:, h * C_HPAD:h * C_HPAD + C_NOPE].set(kvh[:, :C_NOPE])
        wuv = wuv.at[:, h * C_VDIM:(h + 1) * C_VDIM].set(kvh[:, C_NOPE:])
    gain = lambda g: _row(jnp.concatenate([g[:C_NOPE], g[C_NOPE:], g[C_NOPE:]]))
    return (q_norm.reshape(1, -1), kv_norm.reshape(1, -1), wq, wqr, wuk, wuv, gain(q_gain), gain(k_gain))


def _rope_tables(pos):
    half = C_ROPE // 2
    inv = ROPE_THETA ** (-(jnp.arange(0, C_ROPE, 2, dtype=F32) / C_ROPE))
    ang = pos.astype(F32)[:, None] * inv[None, :]
    cos, sin = jnp.cos(ang), jnp.sin(ang)
    pad = lambda t: jnp.pad(jnp.concatenate([t, t], axis=1), ((0, 0), (C_NOPE, LANES - C_NOPE - 2 * half)))
    return pad(cos), pad(sin)


def _c_prep(pc, cos, sin, cw, precise):
    t = pc.shape[0]
    tm = _pick(t, (512, 256, 128, 64, 32, 16, 8))
    qn, kvn, wq, wqr, wuk, wuv, gq, gk = cw
    full = lambda a: pl.BlockSpec(a.shape, lambda i: (0,) * a.ndim)
    rows = lambda w: pl.BlockSpec((tm, w), lambda i: (i, 0))
    act = F32 if precise else BF16
    return pl.pallas_call(
        functools.partial(_c_prep_kernel, precise=precise),
        grid=(t // tm,),
        in_specs=[rows(C_SEG), rows(LANES), rows(LANES)] + [full(a) for a in (qn, kvn, wq, wqr, wuk, wuv, gq, gk)],
        out_specs=[rows(C_HEADS * C_HPAD), rows(C_HEADS * C_HPAD), rows(C_HEADS * C_VDIM), rows(C_KV_LORA),
                   rows(C_ROPE)],
        out_shape=[
            jax.ShapeDtypeStruct((t, C_HEADS * C_HPAD), act),
            jax.ShapeDtypeStruct((t, C_HEADS * C_HPAD), act),
            jax.ShapeDtypeStruct((t, C_HEADS * C_VDIM), act),
            jax.ShapeDtypeStruct((t, C_KV_LORA), F32),
            jax.ShapeDtypeStruct((t, C_ROPE), F32),
        ],
        compiler_params=_cparams(("arbitrary",)),
        name="c_prep",
    )(pc, cos, sin, qn, kvn, wq, wqr, wuk, wuv, gq, gk)


def _kv_expand_call(ckv, krp, cw, precise):
    t = ckv.shape[0]
    tm = _pick(t, (1024, 512, 256, 128, 64, 32, 16, 8))
    _, _, _, _, wuk, wuv, _, gk = cw
    full = lambda a: pl.BlockSpec(a.shape, lambda i: (0,) * a.ndim)
    rows = lambda w: pl.BlockSpec((tm, w), lambda i: (i, 0))
    act = F32 if precise else BF16
    return pl.pallas_call(
        functools.partial(_kv_expand_kernel, precise=precise),
        grid=(t // tm,),
        in_specs=[rows(C_KV_LORA), rows(LANES), full(wuk), full(wuv), full(gk)],
        out_specs=[rows(C_HEADS * C_HPAD), rows(C_HEADS * C_VDIM)],
        out_shape=[
            jax.ShapeDtypeStruct((t, C_HEADS * C_HPAD), act),
            jax.ShapeDtypeStruct((t, C_HEADS * C_VDIM), act),
        ],
        compiler_params=_cparams(("arbitrary",)),
        name="kv_expand",
    )(ckv, krp, wuk, wuv, gk)


def _attn_kernel(*refs, tq, nq, past, has_cache, precise):
    if has_cache:
        q_ref, kn_ref, vn_ref, kc_ref, vc_ref, o_ref = refs
    else:
        q_ref, kn_ref, vn_ref, o_ref = refs
    dot = _mdot(precise)
    qi = pl.program_id(1)
    shift = CHUNK.bit_length() - 1

    q = q_ref[...]
    q_chunk = jnp.right_shift(past + qi * tq + lax.broadcasted_iota(I32, (tq, 1), 0), shift)
    qh = [q[:, h * C_HPAD:(h + 1) * C_HPAD] for h in range(C_HEADS)]
    init = tuple((jnp.full((tq, 1), -1e30, F32), jnp.zeros((tq, 1), F32), jnp.zeros((tq, C_VDIM), F32))
                 for _ in range(C_HEADS))

    def step(state, kb, vb, p0, w):
        vis = jnp.right_shift(p0 + lax.broadcasted_iota(I32, (1, w), 1), shift) <= q_chunk
        new = []
        for h in range(C_HEADS):
            m, den, acc = state[h]
            s = jnp.where(vis, dot(qh[h], kb[:, h * C_HPAD:(h + 1) * C_HPAD], _NT), -1e30)
            m_new = jnp.maximum(m, jnp.max(s, axis=-1, keepdims=True))
            scale = jnp.exp(m - m_new)
            p = jnp.exp(s - m_new)
            new.append((m_new, den * scale + jnp.sum(p, axis=-1, keepdims=True),
                        acc * scale + dot(p, vb[:, h * C_VDIM:(h + 1) * C_VDIM])))
        return tuple(new)

    def finish(state):
        o_ref[...] = jnp.concatenate([acc / den for _, den, acc in state], axis=1).astype(o_ref.dtype)

    if precise:
        def new_step(j, st):
            r0 = pl.multiple_of(j * tq, tq)
            return step(st, kn_ref[pl.ds(r0, tq), :], vn_ref[pl.ds(r0, tq), :], past + r0, tq)

        state = lax.fori_loop(0, qi + 1, new_step, init)
        if has_cache:
            tc = _pick(past, (ATTN_KEY_BLOCK, 256, 128, 64, 32, 16, 8))

            def cache_step(j, st):
                r0 = pl.multiple_of(j * tc, tc)
                return step(st, kc_ref[pl.ds(r0, tc), :], vc_ref[pl.ds(r0, tc), :], r0, tc)

            state = lax.fori_loop(0, past // tc, cache_step, state)
        finish(state)
    else:
        def attend(n_new):
            blocks = [(kn_ref, vn_ref, r0, min(ATTN_KEY_BLOCK, n_new - r0), past + r0)
                      for r0 in range(0, n_new, ATTN_KEY_BLOCK)]
            if has_cache:
                blocks += [(kc_ref, vc_ref, r0, min(ATTN_KEY_BLOCK, past - r0), r0)
                           for r0 in range(0, past, ATTN_KEY_BLOCK)]
            state = init
            for k_ref, v_ref, r0, w, p0 in blocks:
                state = step(state, k_ref[r0:r0 + w, :], v_ref[r0:r0 + w, :], p0, w)
            finish(state)

        if nq == 1:
            attend(tq)
        else:
            for blk in range(nq):
                pl.when(qi == blk)(functools.partial(attend, (blk + 1) * tq))


def _attention(q, k, v, row0, bsz, seq, precise, k_cache=None, v_cache=None):
    has_cache = k_cache is not None
    past = k_cache.shape[0] // bsz if has_cache else 0
    tq = _pick(seq, (256, 128, 64, 32, 16, 8))
    assert tq % CHUNK == 0 or tq == seq
    nq = seq // tq
    kern = functools.partial(_attn_kernel, tq=tq, nq=nq, past=past, has_cache=has_cache, precise=precise)
    qw, vw = C_HEADS * C_HPAD, C_HEADS * C_VDIM
    in_specs = [
        pl.BlockSpec((tq, qw), lambda b, i: (row0 // tq + b * nq + i, 0)),
        pl.BlockSpec((seq, qw), lambda b, i: (row0 // seq + b, 0)),
        pl.BlockSpec((seq, vw), lambda b, i: (row0 // seq + b, 0)),
    ]
    args = [q, k, v]
    if has_cache:
        in_specs += [pl.BlockSpec((past, qw), lambda b, i: (b, 0)), pl.BlockSpec((past, vw), lambda b, i: (b, 0))]
        args += [k_cache, v_cache]
    return pl.pallas_call(
        kern,
        grid=(bsz, nq),
        in_specs=in_specs,
        out_specs=pl.BlockSpec((tq, vw), lambda b, i: (b * nq + i, 0)),
        out_shape=jax.ShapeDtypeStruct((bsz * seq, vw), F32 if precise else BF16),
        compiler_params=_cparams(("arbitrary", "arbitrary")),
        name="attention",
    )(*args)


def _out_proj_kernel(*refs, n_prompt_tiles, precise, split_x):
    dot = _mdot(precise)
    is_p = pl.program_id(0) < n_prompt_tiles
    if split_x:
        y = jnp.where(is_p, refs[0][...], refs[1][...])
        refs = refs[2:]
    else:
        y = refs[0][...]
        refs = refs[1:]
    (oap_ref, obp_ref, ocp_ref, oas_ref, obs_ref, ocs_ref, w_ref, g_ref, wr_ref, br_ref,
     y_ref, xt_ref, ri_ref, rw_ref) = refs
    oa = jnp.where(is_p, oap_ref[...], oas_ref[...])
    ob = jnp.where(is_p, obp_ref[...], obs_ref[...])
    oc = jnp.where(is_p, ocp_ref[...], ocs_ref[...])
    y = y + dot(oa, w_ref[0:A_V])
    y = y + dot(ob, w_ref[A_V:A_V + B_C])
    y = y + dot(oc, w_ref[A_V + B_C:D_MIX])
    y_ref[...] = y
    xn = _rms(y) * g_ref[...]
    _rows_to_tiles(xt_ref, xn)

    logits = _dot3(xn, wr_ref[...]) + br_ref[...]
    lane = lax.broadcasted_iota(I32, logits.shape, 1)
    lanef = lane.astype(F32)
    neg = -1e30
    lg = jnp.where(lane < N_GROUPS, logits, neg)
    mg = jnp.max(lg, axis=-1, keepdims=True)
    pg = 1.0 / jnp.sum(jnp.exp(lg - mg), axis=-1, keepdims=True)
    gidx = jnp.min(jnp.where(lg == mg, lanef, float(LANES)), axis=-1, keepdims=True)
    lo = N_GROUPS + EXPERTS_PER_GROUP * gidx
    in_grp = (lanef >= lo) & (lanef < lo + EXPERTS_PER_GROUP)
    el = jnp.where(in_grp, logits, neg)
    m1 = jnp.max(el, axis=-1, keepdims=True)
    i1 = jnp.min(jnp.where(el == m1, lanef, float(LANES)), axis=-1, keepdims=True)
    el2 = jnp.where(lanef == i1, neg, el)
    m2 = jnp.max(el2, axis=-1, keepdims=True)
    i2 = jnp.min(jnp.where(el2 == m2, lanef, float(LANES)), axis=-1, keepdims=True)
    den = jnp.sum(jnp.exp(el - m1), axis=-1, keepdims=True)
    p1 = 1.0 / den
    p2 = jnp.exp(m2 - m1) / den
    w1 = pg * p1 / (p1 + p2)
    w2 = pg * p2 / (p1 + p2)
    e1 = (i1 - N_GROUPS).astype(I32)
    e2 = (i2 - N_GROUPS).astype(I32)
    ri_ref[...] = jnp.where(lane == 0, e1, jnp.where(lane == 1, e2, 0))
    rw_ref[...] = jnp.where(lane == 0, w1, jnp.where(lane == 1, w2, 0.0))


def _out_proj(xs, mix_p, mix_s, w_out, g, w_router, b_router, precise):
    tp, ts = mix_p[0].shape[0], mix_s[0].shape[0]
    t = tp + ts
    tm = _pick(ts, (512, 256, 128, 64, 32, 16, 8))
    assert tp % tm == 0 and sum(x.shape[0] for x in xs) == t
    npt = tp // tm
    kern = functools.partial(_out_proj_kernel, n_prompt_tiles=npt, precise=precise, split_x=len(xs) == 2)
    pspec = lambda w: pl.BlockSpec((tm, w), lambda i: (jnp.minimum(i, npt - 1), 0))
    sspec = lambda w: pl.BlockSpec((tm, w), lambda i: (jnp.maximum(i - npt, 0), 0))
    full = lambda a: pl.BlockSpec(a.shape, lambda i: (0,) * a.ndim)
    rows = lambda w: pl.BlockSpec((tm, w), lambda i: (i, 0))
    cw = C_HEADS * C_VDIM
    g2 = g.reshape(1, D_MODEL)
    x_specs = [pspec(D_MODEL), sspec(D_MODEL)] if len(xs) == 2 else [rows(D_MODEL)]
    return pl.pallas_call(
        kern,
        grid=(t // tm,),
        in_specs=x_specs + [pspec(A_V), pspec(B_C), pspec(cw), sspec(A_V), sspec(B_C), sspec(cw),
                            full(w_out), full(g2), full(w_router), full(b_router)],
        out_specs=[rows(D_MODEL), pl.BlockSpec((tm * SUBLANES, LANES), lambda i: (i, 0)), rows(LANES), rows(LANES)],
        out_shape=[
            jax.ShapeDtypeStruct((t, D_MODEL), F32),
            jax.ShapeDtypeStruct((t * SUBLANES, LANES), F32),
            jax.ShapeDtypeStruct((t, LANES), I32),
            jax.ShapeDtypeStruct((t, LANES), F32),
        ],
        compiler_params=_cparams(("arbitrary",)),
        name="out_proj_router",
    )(*xs, *mix_p, *mix_s, w_out, g2, w_router, b_router)


def _rank_kernel(ri_ref, rank_ref, cnt_ref, base_ref):
    @pl.when(pl.program_id(0) == 0)
    def _():
        base_ref[...] = jnp.zeros_like(base_ref)

    ri = ri_ref[...]
    tm = ri.shape[0]
    lane = lax.broadcasted_iota(I32, ri.shape, 1)
    oh0 = lane == ri[:, 0:1]
    oh1 = lane == ri[:, 1:2]
    cnt = oh0.astype(F32) + oh1.astype(F32)
    row = lax.broadcasted_iota(I32, (tm, tm), 0)
    col = lax.broadcasted_iota(I32, (tm, tm), 1)
    before = jnp.dot((row > col).astype(BF16), cnt.astype(BF16), preferred_element_type=F32) + base_ref[...]
    r0 = jnp.sum(jnp.where(oh0, before, 0.0), axis=-1, keepdims=True)
    r1 = jnp.sum(jnp.where(oh1, before, 0.0), axis=-1, keepdims=True)
    rank_ref[...] = jnp.where(lane == 0, r0, jnp.where(lane == 1, r1, 0.0)).astype(I32)
    total = base_ref[...] + jnp.sum(cnt, axis=0, keepdims=True)
    base_ref[...] = total
    cnt_ref[...] = total.astype(I32)


def _rank(ri):
    t = ri.shape[0]
    tm = _pick(t, (256, 128, 64, 32, 16, 8))
    return pl.pallas_call(
        _rank_kernel,
        grid=(t // tm,),
        in_specs=[pl.BlockSpec((tm, LANES), lambda i: (i, 0))],
        out_specs=[pl.BlockSpec((tm, LANES), lambda i: (i, 0)), pl.BlockSpec((1, LANES), lambda i: (0, 0))],
        out_shape=[jax.ShapeDtypeStruct((t, LANES), I32), jax.ShapeDtypeStruct((1, LANES), I32)],
        scratch_shapes=[pltpu.VMEM((1, LANES), F32)],
        compiler_params=_cparams(("arbitrary",)),
        name="moe_rank",
    )(ri)


def _gather_tiles(idx_ref, n, src_hbm, buf, slot, sem):
    def body(j, carry):
        for queue in range(2):
            r = 2 * j + queue
            src = pl.multiple_of(idx_ref[0, r] * SUBLANES, SUBLANES)
            dst = pl.multiple_of((slot * n + r) * SUBLANES, SUBLANES)
            pltpu.make_async_copy(src_hbm.at[pl.ds(src, SUBLANES)], buf.at[pl.ds(dst, SUBLANES)],
                                  sem.at[slot]).start(priority=queue)
        return carry

    lax.fori_loop(0, n // 2, body, 0, unroll=4)


def _gather_wait(n, src_hbm, buf, slot, sem):
    base = pl.multiple_of(slot * n * SUBLANES, SUBLANES)
    pltpu.make_async_copy(src_hbm.at[pl.ds(0, n * SUBLANES)], buf.at[pl.ds(base, n * SUBLANES)], sem.at[slot]).wait()
    return base


def _expert_kernel(te_ref, nu_ref, idc_ref, idn_ref, xt_hbm, wg_ref, wu_ref, wd_ref, o_ref,
                   xbuf, wgb, wub, wdb, sem):
    i = pl.program_id(0)
    n_used = nu_ref[0]
    slot = lax.rem(i, 2)

    @pl.when((i == 0) & (n_used > 0))
    def _():
        _gather_tiles(idc_ref, MOE_TILE, xt_hbm, xbuf, 0, sem)

    @pl.when(i + 1 < n_used)
    def _():
        _gather_tiles(idn_ref, MOE_TILE, xt_hbm, xbuf, 1 - slot, sem)

    @pl.when((i == 0) | (te_ref[i] != te_ref[jnp.maximum(i - 1, 0)]))
    def _():
        wgb[...] = wg_ref[...].astype(BF16)
        wub[...] = wu_ref[...].astype(BF16)
        wdb[...] = wd_ref[...].astype(BF16)

    @pl.when(i < n_used)
    def _():
        base = _gather_wait(MOE_TILE, xt_hbm, xbuf, slot, sem)
        x = _tiles_to_rows(xbuf, base, MOE_TILE).astype(BF16)
        hg = jnp.dot(x, wgb[...], preferred_element_type=F32)
        hu = jnp.dot(x, wub[...], preferred_element_type=F32)
        hidden = (_silu(hg) * hu).astype(BF16)
        _rows_to_tiles(o_ref, jnp.dot(hidden, wdb[...], preferred_element_type=F32))

    @pl.when(i >= n_used)
    def _():
        o_ref[...] = jnp.zeros_like(o_ref)


def _expert_ffn(xt, token_of_pos, tile_expert, n_used, e_gate, e_up, e_down):
    nt = token_of_pos.shape[0] // MOE_TILE
    idx = token_of_pos.reshape(nt, 1, MOE_TILE)
    tile_rows = MOE_TILE * SUBLANES
    smem = lambda imap: pl.BlockSpec((None, 1, MOE_TILE), imap, memory_space=pltpu.SMEM)
    grid_spec = pltpu.PrefetchScalarGridSpec(
        num_scalar_prefetch=2,
        grid=(nt,),
        in_specs=[
            smem(lambda i, te, nu: (i, 0, 0)),
            smem(lambda i, te, nu: (jnp.minimum(i + 1, nt - 1), 0, 0)),
            pl.BlockSpec(memory_space=pl.ANY),
            pl.BlockSpec((None, D_MODEL, D_EXPERT), lambda i, te, nu: (te[i], 0, 0)),
            pl.BlockSpec((None, D_MODEL, D_EXPERT), lambda i, te, nu: (te[i], 0, 0)),
            pl.BlockSpec((None, D_EXPERT, D_MODEL), lambda i, te, nu: (te[i], 0, 0)),
        ],
        out_specs=pl.BlockSpec((tile_rows, LANES), lambda i, te, nu: (i, 0)),
        scratch_shapes=[pltpu.VMEM((2 * tile_rows, LANES), F32),
                        pltpu.VMEM((D_MODEL, D_EXPERT), BF16), pltpu.VMEM((D_MODEL, D_EXPERT), BF16),
                        pltpu.VMEM((D_EXPERT, D_MODEL), BF16), pltpu.SemaphoreType.DMA((2,))],
    )
    return pl.pallas_call(
        _expert_kernel,
        grid_spec=grid_spec,
        out_shape=jax.ShapeDtypeStruct((nt * tile_rows, LANES), F32),
        compiler_params=_cparams(("arbitrary",)),
        name="expert_ffn",
    )(tile_expert, n_used, idx, idx, xt, e_gate, e_up, e_down)


def _combine_kernel(pc_ref, pn_ref, x_ref, rw_ref, ot_hbm, *refs, tm, n_first_tiles):
    i = pl.program_id(0)
    nt = pl.num_programs(0)
    slot = lax.rem(i, 2)
    n = TOP_K * tm
    obuf, sem = refs[-2:]

    @pl.when(i == 0)
    def _():
        _gather_tiles(pc_ref, n, ot_hbm, obuf, 0, sem)

    @pl.when(i + 1 < nt)
    def _():
        _gather_tiles(pn_ref, n, ot_hbm, obuf, 1 - slot, sem)

    base = _gather_wait(n, ot_hbm, obuf, slot, sem)
    rw = rw_ref[...]
    o0 = _tiles_to_rows(obuf, base, tm)
    o1 = _tiles_to_rows(obuf, base + tm * SUBLANES, tm)
    y = x_ref[...] + rw[:, 0:1] * o0 + rw[:, 1:2] * o1
    if n_first_tiles is None:
        refs[0][...] = y
    else:
        @pl.when(i < n_first_tiles)
        def _():
            refs[0][...] = y

        @pl.when(i >= n_first_tiles)
        def _():
            refs[1][...] = y


def _combine(x, out_tiles, pos, rw, split_rows=None):
    t = x.shape[0]
    rows = lambda w: pl.BlockSpec((tm, w), lambda i: (i, 0))
    if split_rows is None:
        tm = _pick(t, (256, 128, 64, 32, 16, 8))
        nft = None
        out_specs = rows(D_MODEL)
        out_shape = jax.ShapeDtypeStruct((t, D_MODEL), F32)
    else:
        tm = _pick(t - split_rows, (256, 128, 64, 32, 16, 8))
        assert split_rows % tm == 0
        nft = split_rows // tm
        out_specs = [pl.BlockSpec((tm, D_MODEL), lambda i: (jnp.minimum(i, nft - 1), 0)),
                     pl.BlockSpec((tm, D_MODEL), lambda i: (jnp.maximum(i - nft, 0), 0))]
        out_shape = [jax.ShapeDtypeStruct((split_rows, D_MODEL), F32),
                     jax.ShapeDtypeStruct((t - split_rows, D_MODEL), F32)]
    nt = t // tm
    n = TOP_K * tm
    idx = pos.reshape(nt, tm, TOP_K).transpose(0, 2, 1).reshape(nt, 1, n)
    kern = functools.partial(_combine_kernel, tm=tm, n_first_tiles=nft)
    smem = lambda imap: pl.BlockSpec((None, 1, n), imap, memory_space=pltpu.SMEM)
    return pl.pallas_call(
        kern,
        grid=(nt,),
        in_specs=[smem(lambda i: (i, 0, 0)), smem(lambda i: (jnp.minimum(i + 1, nt - 1), 0, 0)),
                  rows(D_MODEL), rows(LANES), pl.BlockSpec(memory_space=pl.ANY)],
        out_specs=out_specs,
        out_shape=out_shape,
        scratch_shapes=[pltpu.VMEM((2 * n * SUBLANES, LANES), F32), pltpu.SemaphoreType.DMA((2,))],
        compiler_params=_cparams(("arbitrary",)),
        name="moe_combine",
    )(idx, idx, x, rw, out_tiles)


def _route_tables(eid, rank, counts):
    t = eid.shape[0]
    padded = ((counts + MOE_TILE - 1) // MOE_TILE) * MOE_TILE
    ends = jnp.cumsum(padded)
    starts = ends - padded
    experts = jnp.arange(N_EXPERTS, dtype=I32)
    pos = jnp.sum(jnp.where(eid[:, :, None] == experts, starts, 0), axis=-1) + rank
    np_rows = -(-(TOP_K * t + N_EXPERTS * MOE_TILE) // MOE_TILE) * MOE_TILE
    token_of_pos = jnp.zeros((np_rows,), I32).at[pos.reshape(-1)].set(jnp.arange(TOP_K * t, dtype=I32) // TOP_K)
    tile_start = jnp.arange(np_rows // MOE_TILE, dtype=I32) * MOE_TILE
    tile_expert = jnp.minimum(jnp.sum(tile_start[:, None] >= ends[None, :], axis=1), N_EXPERTS - 1).astype(I32)
    n_used = (ends[-1] // MOE_TILE).astype(I32).reshape(1)
    return pos.astype(I32), token_of_pos, tile_expert, n_used


def _moe(y, xt, ri, rw, e_gate, e_up, e_down, split_rows):
    rank, counts = _rank(ri)
    pos, token_of_pos, tile_expert, n_used = _route_tables(ri[:, :TOP_K], rank[:, :TOP_K], counts[0, :N_EXPERTS])
    out_tiles = _expert_ffn(xt, token_of_pos, tile_expert, n_used, e_gate, e_up, e_down)
    return _combine(y, out_tiles, pos, rw, split_rows)


def _in_weights(w_in):
    half = C_ROPE // 2
    zeros = lambda n: jnp.zeros((D_MODEL, n), F32)
    c0 = A_WIDTH + B_WIDTH
    kr = w_in[:, c0 + C_Q_LORA + C_KV_LORA:c0 + C_WIDTH]
    kr_rot = jnp.concatenate([-kr[:, half:], kr[:, :half]], axis=1)
    tail = LANES - C_NOPE - C_ROPE
    cols = [
        w_in[:, :A_CONV_CH + A_V], w_in[:, A_CONV_CH + A_V:A_WIDTH], zeros(LANES - 2 * A_HEADS),
        w_in[:, A_WIDTH:c0],
        w_in[:, c0:c0 + C_Q_LORA + C_KV_LORA],
        zeros(C_NOPE), kr, zeros(tail),
        zeros(C_NOPE), kr_rot, zeros(tail),
    ]
    return jnp.concatenate(cols, axis=1)


def _router_weights(rg, rgb, re, reb):
    w = jnp.zeros((D_MODEL, LANES), F32).at[:, :N_GROUPS].set(rg).at[:, N_GROUPS:N_GROUPS + N_EXPERTS].set(re)
    b = jnp.zeros((1, LANES), F32).at[0, :N_GROUPS].set(rgb).at[0, N_GROUPS:N_GROUPS + N_EXPERTS].set(reb)
    return w, b


def _layer(xs, geom, st_p, st_s, cos, sin, wts, precise, split_out):
    (norm_mix, w_in, a_conv_w, a_A_log, a_dt_bias, a_norm_w,
     b_mu, b_w0, b_w2, b_a0, b_a2, b_g2, b_k_k, b_k_a, b_r_k, b_ln_w, b_ln_b,
     c_q_norm, c_kv_norm, c_w_uq, c_w_ukv, c_q_gain, c_k_gain,
     w_out, norm_ffn, router_group, router_group_bias, router_expert, router_expert_bias,
     e_gate, e_up, e_down) = wts
    bp, lp, bs, ls, past = geom
    tp = bp * lp

    pa, pb, pc = _in_proj(xs, norm_mix, _in_weights(w_in), precise)
    cw = _c_weights(c_q_norm, c_kv_norm, c_w_uq, c_w_ukv, c_q_gain, c_k_gain)
    q, k, v, ckv, krope = _c_prep(pc, cos, sin, cw, precise)

    a_args = (a_conv_w, a_A_log, a_dt_bias, a_norm_w)
    b_args = (b_mu, b_w0, b_w2, b_a0, b_a2, b_g2, b_k_k, b_k_a, b_r_k.reshape(-1), b_ln_w, b_ln_b)
    conv_p, delta_p, shift_p, wkv_p = st_p
    conv_s, delta_s, shift_s, wkv_s, ckv_past, krope_past = st_s

    oa_p, conv_np, delta_np = _mixer_a(pa, 0, bp, lp, precise, conv_p, delta_p, *a_args)
    oa_s, conv_ns, delta_ns = _mixer_a(pa, tp, bs, ls, precise, conv_s, delta_s, *a_args)
    ob_p, shift_np, wkv_np = _mixer_b(pb, 0, bp, lp, precise, shift_p, wkv_p, *b_args)
    ob_s, shift_ns, wkv_ns = _mixer_b(pb, tp, bs, ls, precise, shift_s, wkv_s, *b_args)

    oc_p = _attention(q, k, v, 0, bp, lp, precise)
    krp_past = jnp.pad(krope_past.reshape(bs * past, C_ROPE), ((0, 0), (C_NOPE, LANES - C_NOPE - C_ROPE)))
    k_cache, v_cache = _kv_expand_call(ckv_past.reshape(bs * past, C_KV_LORA), krp_past, cw, precise)
    oc_s = _attention(q, k, v, tp, bs, ls, precise, k_cache, v_cache)

    w_router, b_router = _router_weights(router_group, router_group_bias, router_expert, router_expert_bias)
    y, xt, ri, rw = _out_proj(xs, (oa_p, ob_p, oc_p), (oa_s, ob_s, oc_s), w_out, norm_ffn,
                              w_router, b_router, precise)
    x_new = _moe(y, xt, ri, rw, e_gate, e_up, e_down, tp if split_out else None)

    new_p = (conv_np, delta_np, shift_np, wkv_np, ckv[:tp].reshape(bp, lp, C_KV_LORA),
             krope[:tp].reshape(bp, lp, C_ROPE))
    new_s = (conv_ns, delta_ns, shift_ns, wkv_ns, ckv[tp:].reshape(bs, ls, C_KV_LORA),
             krope[tp:].reshape(bs, ls, C_ROPE))
    return x_new, new_p, new_s


def _forward(x_prompt, x_sample, cache_c_kv, cache_k_rope, state_conv_a, state_delta_a, state_shift_b,
             state_wkv_b, weights):
    bp, lp, _ = x_prompt.shape
    bs, ls, _ = x_sample.shape
    depth = cache_c_kv.shape[0]
    past = cache_c_kv.shape[2]
    geom = (bp, lp, bs, ls, past)
    xs = [x_prompt.reshape(bp * lp, D_MODEL), x_sample.reshape(bs * ls, D_MODEL)]
    pos = jnp.concatenate([jnp.tile(jnp.arange(lp), bp), jnp.tile(past + jnp.arange(ls), bs)])
    cos, sin = _rope_tables(pos)
    zeros = lambda *s: jnp.zeros(s, F32)
    st_p = (zeros(bp, A_CONV - 1, A_CONV_CH), zeros(bp, A_HEADS, A_DK, A_DV), zeros(bp, 1, B_WIDTH),
            zeros(bp, B_HEADS, B_N, B_N))
    news_p, news_s = [], []
    for l in range(depth):
        st_s = (state_conv_a[l], state_delta_a[l], state_shift_b[l], state_wkv_b[l], cache_c_kv[l], cache_k_rope[l])
        last = l == depth - 1
        x, new_p, new_s = _layer(xs, geom, st_p, st_s, cos, sin, [w[l] for w in weights], precise=not last,
                                 split_out=last)
        xs = x if last else [x]
        news_p.append(new_p)
        news_s.append(new_s)
    stack = lambda news, i: jnp.stack([n[i] for n in news])
    y_prompt = xs[0].reshape(bp, lp, D_MODEL)
    y_sample = xs[1].reshape(bs, ls, D_MODEL)
    p_conv, p_delta, p_shift, p_wkv, p_ckv, p_krope = (stack(news_p, i) for i in range(6))
    s_conv, s_delta, s_shift, s_wkv, s_ckv, s_krope = (stack(news_s, i) for i in range(6))
    return (y_prompt, y_sample, p_ckv, p_krope, p_conv, p_delta, p_shift, p_wkv,
            s_ckv, s_krope, s_conv, s_delta, s_shift, s_wkv)


def kernel(x_prompt, x_sample, cache_c_kv, cache_k_rope, state_conv_a, state_delta_a, state_shift_b, state_wkv_b,
           norm_mix, w_in, a_conv_w, a_A_log, a_dt_bias, a_norm_w,
           b_mu, b_w0, b_w2, b_a0, b_a2, b_g2, b_k_k, b_k_a, b_r_k, b_ln_w, b_ln_b,
           c_q_norm, c_kv_norm, c_w_uq, c_w_ukv, c_q_gain, c_k_gain,
           w_out, norm_ffn, router_group, router_group_bias, router_expert, router_expert_bias,
           e_gate, e_up, e_down):
    weights = (norm_mix, w_in, a_conv_w, a_A_log, a_dt_bias, a_norm_w,
               b_mu, b_w0, b_w2, b_a0, b_a2, b_g2, b_k_k, b_k_a, b_r_k, b_ln_w, b_ln_b,
               c_q_norm, c_kv_norm, c_w_uq, c_w_ukv, c_q_gain, c_k_gain,
               w_out, norm_ffn, router_group, router_group_bias, router_expert, router_expert_bias,
               e_gate, e_up, e_down)
    return _forward(x_prompt, x_sample, cache_c_kv, cache_k_rope, state_conv_a, state_delta_a, state_shift_b,
                    state_wkv_b, weights)
```

```python
import functools

import jax
import jax.numpy as jnp
from jax import lax
from jax.experimental import pallas as pl
from jax.experimental.pallas import tpu as pltpu

F32 = jnp.float32
BF16 = jnp.bfloat16
I32 = jnp.int32

D_MODEL = 1024
DEPTH = 2
CHUNK = 64
EPS = 1e-6
LANES = 128
SUBLANES = 8
ROW_TILES = D_MODEL // LANES

A_HEADS, A_DK, A_DV, A_CONV = 6, 64, 64, 4
A_QK = A_HEADS * A_DK
A_V = A_HEADS * A_DV
A_CONV_CH = 2 * A_QK + A_V
A_WIDTH = A_CONV_CH + A_V + 2 * A_HEADS
A_SEG = A_CONV_CH + A_V + LANES

B_HEADS, B_N = 6, 64
B_C = B_HEADS * B_N
B_W_LORA, B_A_LORA, B_G_LORA = 64, 64, 128
B_WIDTH = 3 * B_C + B_W_LORA + B_A_LORA + B_G_LORA
B_GN_EPS = 64e-5

C_HEADS, C_NOPE, C_ROPE, C_VDIM = 4, 64, 32, 64
C_QK = C_NOPE + C_ROPE
C_Q_LORA, C_KV_LORA = 256, 128
C_WIDTH = C_Q_LORA + C_KV_LORA + C_ROPE
C_SEG = C_Q_LORA + C_KV_LORA + 2 * LANES
C_HPAD = LANES
ROPE_THETA = 10000.0

P_TOTAL = A_WIDTH + B_WIDTH + C_WIDTH
D_MIX = A_V + B_C + C_HEADS * C_VDIM

N_GROUPS, EXPERTS_PER_GROUP = 4, 8
N_EXPERTS = N_GROUPS * EXPERTS_PER_GROUP
TOP_K = 2
D_EXPERT = 256
MOE_TILE = 256
MIXER_CHUNKS_PER_STEP = 4
ATTN_KEY_BLOCK = 2048

VMEM_LIMIT = 48 * 1024 * 1024

_NN = (((1,), (0,)), ((), ()))
_NT = (((1,), (1,)), ((), ()))
_TN = (((0,), (0,)), ((), ()))


def _pick(n, prefs):
    for p in prefs:
        if n % p == 0:
            return p
    raise ValueError(f"no tile for {n} in {prefs}")


def _cparams(sem):
    return pltpu.CompilerParams(dimension_semantics=sem, vmem_limit_bytes=VMEM_LIMIT)


def _bdot(a, b, dims=_NN):
    return lax.dot_general(a.astype(BF16), b.astype(BF16), dims, preferred_element_type=F32)


def _split2(a):
    hi = a.astype(BF16)
    lo = (a - hi.astype(F32)).astype(BF16)
    return hi, lo


def _dot3(a, b, dims=_NN):
    ah, al = _split2(a)
    bh, bl = _split2(b)
    f = lambda x, y: lax.dot_general(x, y, dims, preferred_element_type=F32)
    return f(ah, bh) + (f(ah, bl) + f(al, bh))


def _mdot(precise):
    return _dot3 if precise else _bdot


def _cumsum_rows(x, ltri):
    h = x.astype(BF16)
    r = x - h.astype(F32)
    m = r.astype(BF16)
    l = (r - m.astype(F32)).astype(BF16)
    d = lambda y: jnp.dot(ltri, y, preferred_element_type=F32)
    return d(h) + (d(m) + d(l))


def _head_sums(x, width):
    n = x.shape[1]
    shift = width.bit_length() - 1
    row = jnp.right_shift(lax.broadcasted_iota(I32, (n, n), 0), shift)
    col = jnp.right_shift(lax.broadcasted_iota(I32, (n, n), 1), shift)
    ones_bd = (row == col).astype(BF16)
    h = x.astype(BF16)
    r = x - h.astype(F32)
    m = r.astype(BF16)
    l = (r - m.astype(F32)).astype(BF16)
    d = lambda y: jnp.dot(y, ones_bd, preferred_element_type=F32)
    return d(h) + (d(m) + d(l))


def _tri_masks(c):
    row = lax.broadcasted_iota(I32, (c, c), 0)
    col = lax.broadcasted_iota(I32, (c, c), 1)
    return row >= col, row > col, (row == col).astype(F32)


def _chunk_tril(rows, c):
    shift = c.bit_length() - 1
    row = lax.broadcasted_iota(I32, (rows, rows), 0)
    col = lax.broadcasted_iota(I32, (rows, rows), 1)
    same = jnp.right_shift(row, shift) == jnp.right_shift(col, shift)
    return (same & (row >= col)).astype(BF16)


def _neumann_inv_many(ns, eye):
    c = eye.shape[0]
    keep_t = lax.broadcasted_iota(I32, (c, 2 * c), 1) >= c
    pts = [jnp.concatenate([n, eye], axis=1) for n in ns]
    m = 1
    while m < c:
        rs = [_dot3(pt[:, :c], pt) for pt in pts]
        pts = [r + jnp.where(keep_t, pt, 0.0) for r, pt in zip(rs, pts)]
        m *= 2
    return [pt[:, c:] for pt in pts]


def _softplus(x):
    return jnp.maximum(x, 0.0) + jnp.log1p(jnp.exp(-jnp.abs(x)))


def _silu(x):
    return x * jax.nn.sigmoid(x)


def _rms(x, eps=EPS):
    return x * lax.rsqrt(jnp.mean(x * x, axis=-1, keepdims=True) + eps)


def _rows_to_tiles(ref, val):
    n = val.shape[0]
    for j in range(ROW_TILES):
        ref[pl.ds(j, n, stride=SUBLANES), :] = val[:, j * LANES:(j + 1) * LANES]


def _tiles_to_rows(ref, start, n):
    return jnp.concatenate([ref[pl.ds(start + j, n, stride=SUBLANES), :] for j in range(ROW_TILES)], axis=1)


def _in_proj_kernel(*refs, precise, n_first_tiles):
    if n_first_tiles is None:
        x = refs[0][...]
        g_ref, *refs = refs[1:]
    else:
        x = jnp.where(pl.program_id(0) < n_first_tiles, refs[0][...], refs[1][...])
        g_ref, *refs = refs[2:]
    if precise:
        wh_ref, wl_ref, pa_ref, pb_ref, pc_ref = refs
    else:
        wh_ref, pa_ref, pb_ref, pc_ref = refs
    xn = _rms(x) * g_ref[...]
    xh = xn.astype(BF16)
    if precise:
        xl = (xn - xh.astype(F32)).astype(BF16)
    lo = 0
    for out_ref, width in ((pa_ref, A_SEG), (pb_ref, B_WIDTH), (pc_ref, C_SEG)):
        acc = jnp.dot(xh, wh_ref[:, lo:lo + width], preferred_element_type=F32)
        if precise:
            acc = acc + (jnp.dot(xh, wl_ref[:, lo:lo + width], preferred_element_type=F32)
                         + jnp.dot(xl, wh_ref[:, lo:lo + width], preferred_element_type=F32))
        out_ref[...] = acc
        lo += width


def _split_kernel(w_ref, hi_ref, lo_ref):
    hi, lo = _split2(w_ref[...])
    hi_ref[...] = hi
    lo_ref[...] = lo


def _split_hi_lo(w):
    r, c = w.shape
    tc = _pick(c, (512, 256, 128))
    spec = pl.BlockSpec((r, tc), lambda j: (0, j))
    return pl.pallas_call(
        _split_kernel,
        grid=(c // tc,),
        in_specs=[spec],
        out_specs=[spec, spec],
        out_shape=[jax.ShapeDtypeStruct((r, c), BF16), jax.ShapeDtypeStruct((r, c), BF16)],
        compiler_params=_cparams(("arbitrary",)),
        name="split_hi_lo",
    )(w)


def _in_proj(xs, g, w, precise):
    prefs = (256, 128, 64, 32, 16, 8) if precise else (512, 256, 128, 64, 32, 16, 8)
    t = sum(x.shape[0] for x in xs)
    if len(xs) == 1:
        tm = _pick(t, prefs)
        nft = None
        x_specs = [pl.BlockSpec((tm, D_MODEL), lambda i: (i, 0))]
    else:
        tm = _pick(xs[1].shape[0], prefs)
        assert xs[0].shape[0] % tm == 0
        nft = xs[0].shape[0] // tm
        x_specs = [pl.BlockSpec((tm, D_MODEL), lambda i: (jnp.minimum(i, nft - 1), 0)),
                   pl.BlockSpec((tm, D_MODEL), lambda i: (jnp.maximum(i - nft, 0), 0))]
    wtot = A_SEG + B_WIDTH + C_SEG
    wspec = pl.BlockSpec((D_MODEL, wtot), lambda i: (0, 0))
    w_args = _split_hi_lo(w) if precise else [w.astype(BF16)]
    return pl.pallas_call(
        functools.partial(_in_proj_kernel, precise=precise, n_first_tiles=nft),
        grid=(t // tm,),
        in_specs=x_specs + [pl.BlockSpec((1, D_MODEL), lambda i: (0, 0))] + [wspec] * len(w_args),
        out_specs=[
            pl.BlockSpec((tm, A_SEG), lambda i: (i, 0)),
            pl.BlockSpec((tm, B_WIDTH), lambda i: (i, 0)),
            pl.BlockSpec((tm, C_SEG), lambda i: (i, 0)),
        ],
        out_shape=[
            jax.ShapeDtypeStruct((t, A_SEG), F32),
            jax.ShapeDtypeStruct((t, B_WIDTH), F32),
            jax.ShapeDtypeStruct((t, C_SEG), F32),
        ],
        compiler_params=_cparams(("arbitrary",)),
        name="in_proj",
    )(*xs, g.reshape(1, D_MODEL), *w_args)


def _mixer_a_kernel(pa_ref, cprev_ref, s0_ref, cw_ref, alog_ref, dtb_ref, nw_ref,
                    oa_ref, cnew_ref, snew_ref, tail_ref, s_ref, *, c, nch, n_steps, precise):
    dot = _mdot(precise)
    step = pl.program_id(1)
    rows = c * nch

    @pl.when(step == 0)
    def _():
        tail_ref[...] = jnp.zeros_like(tail_ref)
        tail_ref[SUBLANES - (A_CONV - 1):SUBLANES, :] = cprev_ref[...]
        s_ref[...] = s0_ref[...]

    x = pa_ref[...]
    qkv = x[:, :A_CONV_CH]
    z = x[:, A_CONV_CH:A_CONV_CH + A_V]
    ab = x[:, A_CONV_CH + A_V:A_SEG]

    xp = jnp.concatenate([tail_ref[...], qkv], axis=0)
    cw = cw_ref[...]
    y = qkv * cw[A_CONV - 1:A_CONV]
    for j in range(A_CONV - 1):
        o = SUBLANES - (A_CONV - 1) + j
        y = y + xp[o:o + rows] * cw[j:j + 1]
    tail_ref[...] = qkv[rows - SUBLANES:rows]

    @pl.when(step == n_steps - 1)
    def _():
        cnew_ref[...] = qkv[rows - (A_CONV - 1):rows]

    act = _silu(y)
    g_all = -jnp.exp(alog_ref[...]) * _softplus(ab + dtb_ref[...])
    beta_all = jax.nn.sigmoid(ab)
    gc_all = _cumsum_rows(g_all, _chunk_tril(rows, c))
    gc_t = gc_all.T
    tril, strict, eye = _tri_masks(c)
    nw = nw_ref[...]

    q_all = act[:, 0:A_QK]
    k_all = act[:, A_QK:2 * A_QK]
    q_all = q_all * lax.rsqrt(_head_sums(q_all * q_all, A_DK) + 1e-6) * (A_DK ** -0.5)
    k_all = k_all * lax.rsqrt(_head_sums(k_all * k_all, A_DK) + 1e-6)
    qn = [q_all[:, h * A_DK:(h + 1) * A_DK] for h in range(A_HEADS)]
    kn = [k_all[:, h * A_DK:(h + 1) * A_DK] for h in range(A_HEADS)]
    vv = [act[:, 2 * A_QK + h * A_DV:2 * A_QK + (h + 1) * A_DV] for h in range(A_HEADS)]

    items = [(ci, h) for ci in range(nch) for h in range(A_HEADS)]
    q_i, k_i, kb_i, vb_i, dec_i, gcc_i, gl_i = [], [], [], [], [], [], []
    for ci, h in items:
        r0 = ci * c
        beta = beta_all[r0:r0 + c, A_HEADS + h:A_HEADS + h + 1]
        gcc = gc_all[r0:r0 + c, h:h + 1]
        gcr = gc_t[h:h + 1, r0:r0 + c]
        k = kn[h][r0:r0 + c]
        q_i.append(qn[h][r0:r0 + c])
        k_i.append(k)
        kb_i.append(k * beta)
        vb_i.append(vv[h][r0:r0 + c] * beta)
        dec_i.append(jnp.exp(jnp.where(tril, gcc - gcr, -1e30)))
        gcc_i.append(gcc)
        gl_i.append(gc_all[r0 + c - 1:r0 + c, h:h + 1])
    a_i = [jnp.where(strict, dot(kb, k, _NT) * dec, 0.0) for kb, k, dec in zip(kb_i, k_i, dec_i)]
    t_i = _neumann_inv_many([-a for a in a_i], eye)
    egc_i = [jnp.exp(g) for g in gcc_i]
    sol_i = [_dot3(t, jnp.concatenate([vb, kb * e], axis=1)) for t, vb, kb, e in zip(t_i, vb_i, kb_i, egc_i)]
    attn_i = [jnp.where(tril, dot(q, k, _NT) * dec, 0.0) for q, k, dec in zip(q_i, k_i, dec_i)]
    kq_i = [jnp.concatenate([sol[:, A_DV:], q * e], axis=0) for sol, q, e in zip(sol_i, q_i, egc_i)]
    kg_i = [k * jnp.exp(gl - g) for k, gl, g in zip(k_i, gl_i, gcc_i)]

    s = [s_ref[h] for h in range(A_HEADS)]
    out_rows = []
    for ci in range(nch):
        ids = [ci * A_HEADS + h for h in range(A_HEADS)]
        ks = [dot(kq_i[i], s[h]) for h, i in enumerate(ids)]
        u = [sol_i[i][:, :A_DV] - ks[h][:c] for h, i in enumerate(ids)]
        o = [ks[h][c:] + dot(attn_i[i], u[h]) for h, i in enumerate(ids)]
        s = [s[h] * jnp.exp(gl_i[i]) + dot(kg_i[i], u[h], _TN) for h, i in enumerate(ids)]
        out_rows.append(jnp.concatenate(o, axis=1))
    for h in range(A_HEADS):
        s_ref[h] = s[h]
    o_all = jnp.concatenate(out_rows, axis=0)
    ms = _head_sums(o_all * o_all, A_DV) * (1.0 / A_DV)
    oa_ref[...] = (o_all * lax.rsqrt(ms + EPS) * nw * _silu(z)).astype(oa_ref.dtype)

    @pl.when(step == n_steps - 1)
    def _():
        snew_ref[...] = s_ref[...]


def _row(v, width=LANES):
    v = v.reshape(1, -1).astype(F32)
    return jnp.pad(v, ((0, 0), (0, width - v.shape[1])))


def _mixer_geometry(row0, seq):
    c = min(CHUNK, seq)
    nch = _pick(seq // c, (MIXER_CHUNKS_PER_STEP, 2, 1))
    rows = c * nch
    assert row0 % rows == 0
    return c, nch, rows, seq // rows, row0 // rows


def _mixer_a(pa, row0, bsz, seq, precise, conv_prev, s0, conv_w, a_log, dt_bias, norm_w):
    c, nch, rows, n_steps, blk0 = _mixer_geometry(row0, seq)
    kern = functools.partial(_mixer_a_kernel, c=c, nch=nch, n_steps=n_steps, precise=precise)
    full = lambda shape: pl.BlockSpec(shape, lambda b, n: (0,) * len(shape))
    return pl.pallas_call(
        kern,
        grid=(bsz, n_steps),
        in_specs=[
            pl.BlockSpec((rows, A_SEG), lambda b, n: (blk0 + b * n_steps + n, 0)),
            pl.BlockSpec((None, A_CONV - 1, A_CONV_CH), lambda b, n: (b, 0, 0)),
            pl.BlockSpec((None, A_HEADS, A_DK, A_DV), lambda b, n: (b, 0, 0, 0)),
            full((A_CONV, A_CONV_CH)),
            full((1, LANES)),
            full((1, LANES)),
            full((1, A_V)),
        ],
        out_specs=[
            pl.BlockSpec((rows, A_V), lambda b, n: (b * n_steps + n, 0)),
            pl.BlockSpec((None, A_CONV - 1, A_CONV_CH), lambda b, n: (b, 0, 0)),
            pl.BlockSpec((None, A_HEADS, A_DK, A_DV), lambda b, n: (b, 0, 0, 0)),
        ],
        out_shape=[
            jax.ShapeDtypeStruct((bsz * seq, A_V), F32 if precise else BF16),
            jax.ShapeDtypeStruct((bsz, A_CONV - 1, A_CONV_CH), F32),
            jax.ShapeDtypeStruct((bsz, A_HEADS, A_DK, A_DV), F32),
        ],
        scratch_shapes=[pltpu.VMEM((SUBLANES, A_CONV_CH), F32), pltpu.VMEM((A_HEADS, A_DK, A_DV), F32)],
        compiler_params=_cparams(("arbitrary", "arbitrary")),
        name="mixer_a",
    )(pa, conv_prev, s0, conv_w, _row(a_log), _row(dt_bias), jnp.tile(norm_w, A_HEADS).reshape(1, A_V))


def _mixer_b_kernel(pb_ref, sprev_ref, s0_ref, mu_ref, w0_ref, w2_ref, a0_ref, a2_ref, g2_ref,
                    kk_ref, ka_ref, rk_ref, lnw_ref, lnb_ref,
                    ob_ref, shnew_ref, snew_ref, last_ref, s_ref, *, c, nch, n_steps, precise):
    dot = _mdot(precise)
    step = pl.program_id(1)
    rows = c * nch

    @pl.when(step == 0)
    def _():
        last_ref[...] = sprev_ref[...]
        s_ref[...] = s0_ref[...]

    p = pb_ref[...]
    rowi = lax.broadcasted_iota(I32, (rows, 1), 0)
    prev = jnp.where(rowi == 0, last_ref[...], pltpu.roll(p, 1, 0))
    last_ref[...] = p[rows - 1:rows]

    @pl.when(step == n_steps - 1)
    def _():
        shnew_ref[...] = p[rows - 1:rows]

    xs = p + (prev - p) * mu_ref[...]
    r_all = xs[:, 0:B_C]
    k_all = xs[:, B_C:2 * B_C]
    v_all = xs[:, 2 * B_C:3 * B_C]
    xwa = xs[:, 3 * B_C:3 * B_C + B_W_LORA + B_A_LORA]
    xg = xs[:, 3 * B_C + B_W_LORA + B_A_LORA:B_WIDTH]

    w_log = -_softplus(-(w0_ref[...] + dot(jnp.tanh(xwa), w2_ref[...]))) - 0.5
    lw = -jnp.exp(w_log)
    rate = jax.nn.sigmoid(a0_ref[...] + dot(xwa, a2_ref[...]))
    gate = dot(jax.nn.sigmoid(xg), g2_ref[...])
    kkr = k_all * kk_ref[...]
    k2_all = k_all * (1.0 + (rate - 1.0) * ka_ref[...])

    cl = _cumsum_rows(lw, _chunk_tril(rows, c))
    e_cl = jnp.exp(cl)
    e_neg = jnp.exp(-cl)
    at_all = jnp.exp(cl - lw)
    bt_all = rate * e_neg
    kt_all = k2_all * e_neg
    rt_all = r_all * e_cl
    tril, strict, eye = _tri_masks(c)
    rk = rk_ref[...]
    lnw = lnw_ref[...]
    lnb = lnb_ref[...]
    kkr_h = [kkr[:, h * B_N:(h + 1) * B_N] for h in range(B_HEADS)]
    kkn = [x * lax.rsqrt(jnp.sum(x * x, axis=-1, keepdims=True) + 1e-6) for x in kkr_h]

    items = [(ci, h) for ci in range(nch) for h in range(B_HEADS)]
    at_i, bt_i, kt_i, rt_i, v_i = [], [], [], [], []
    for ci, h in items:
        rs = slice(ci * c, (ci + 1) * c)
        ls = slice(h * B_N, (h + 1) * B_N)
        kk = kkn[h][rs]
        at_i.append(-kk * at_all[rs, ls])
        bt_i.append(kk * bt_all[rs, ls])
        kt_i.append(kt_all[rs, ls])
        rt_i.append(rt_all[rs, ls])
        v_i.append(v_all[rs, ls])
    bk_i = [jnp.concatenate([b, k], axis=0) for b, k in zip(bt_i, kt_i)]
    ar_i = [jnp.concatenate([a, r], axis=0) for a, r in zip(at_i, rt_i)]
    row2 = lax.broadcasted_iota(I32, (c, 2 * c), 0)
    col2 = jnp.bitwise_and(lax.broadcasted_iota(I32, (c, 2 * c), 1), c - 1)
    strict2 = row2 > col2
    tril2 = row2 >= col2
    g_a = [jnp.where(strict2, _dot3(a, bk, _NT), 0.0) for a, bk in zip(at_i, bk_i)]
    g_r = [jnp.where(tril2, dot(r, bk, _NT), 0.0) for r, bk in zip(rt_i, bk_i)]
    t_i = _neumann_inv_many([g[:, :c] for g in g_a], eye)
    mv_i = [dot(g[:, c:], v) for g, v in zip(g_a, v_i)]

    s = [s_ref[h] for h in range(B_HEADS)]
    out_rows = []
    for ci in range(nch):
        ids = [ci * B_HEADS + h for h in range(B_HEADS)]
        rs = slice(ci * c, (ci + 1) * c)
        ars = [dot(ar_i[i], s[h], _NT) for h, i in enumerate(ids)]
        u = [_dot3(t_i[i], ars[h][:c] + mv_i[i]) for h, i in enumerate(ids)]
        uv = [jnp.concatenate([u[h], v_i[i]], axis=0) for h, i in enumerate(ids)]
        o = [ars[h][c:] + dot(g_r[i], uv[h]) for h, i in enumerate(ids)]
        e_last = e_cl[(ci + 1) * c - 1:(ci + 1) * c]
        s = [(s[h] + dot(uv[h], bk_i[i], _TN)) * e_last[:, h * B_N:(h + 1) * B_N] for h, i in enumerate(ids)]
        outs = []
        for h in range(B_HEADS):
            ls = slice(h * B_N, (h + 1) * B_N)
            mean = jnp.mean(o[h], axis=-1, keepdims=True)
            var = jnp.mean(jnp.square(o[h] - mean), axis=-1, keepdims=True)
            gn = (o[h] - mean) * lax.rsqrt(var + B_GN_EPS) * lnw[:, ls] + lnb[:, ls]
            bonus = jnp.sum(r_all[rs, ls] * k2_all[rs, ls] * rk[:, ls], axis=-1, keepdims=True) * v_all[rs, ls]
            outs.append((gn + bonus) * gate[rs, ls])
        out_rows.append(jnp.concatenate(outs, axis=1))
    for h in range(B_HEADS):
        s_ref[h] = s[h]
    ob_ref[...] = jnp.concatenate(out_rows, axis=0).astype(ob_ref.dtype)

    @pl.when(step == n_steps - 1)
    def _():
        snew_ref[...] = s_ref[...]


def _mixer_b(pb, row0, bsz, seq, precise, shift_prev, s0, mu, w0, w2, a0, a2, g2, k_k, k_a, r_k, ln_w, ln_b):
    c, nch, rows, n_steps, blk0 = _mixer_geometry(row0, seq)
    kern = functools.partial(_mixer_b_kernel, c=c, nch=nch, n_steps=n_steps, precise=precise)
    full = lambda shape: pl.BlockSpec(shape, lambda b, n: (0,) * len(shape))
    lora = B_W_LORA + B_A_LORA
    w2p = jnp.zeros((lora, B_C), F32).at[:B_W_LORA].set(w2)
    a2p = jnp.zeros((lora, B_C), F32).at[B_W_LORA:].set(a2)
    vec = lambda v: v.reshape(1, -1).astype(F32)
    return pl.pallas_call(
        kern,
        grid=(bsz, n_steps),
        in_specs=[
            pl.BlockSpec((rows, B_WIDTH), lambda b, n: (blk0 + b * n_steps + n, 0)),
            pl.BlockSpec((None, 1, B_WIDTH), lambda b, n: (b, 0, 0)),
            pl.BlockSpec((None, B_HEADS, B_N, B_N), lambda b, n: (b, 0, 0, 0)),
            full((1, B_WIDTH)),
            full((1, B_C)), full((lora, B_C)),
            full((1, B_C)), full((lora, B_C)),
            full((B_G_LORA, B_C)),
            full((1, B_C)), full((1, B_C)), full((1, B_C)), full((1, B_C)), full((1, B_C)),
        ],
        out_specs=[
            pl.BlockSpec((rows, B_C), lambda b, n: (b * n_steps + n, 0)),
            pl.BlockSpec((None, 1, B_WIDTH), lambda b, n: (b, 0, 0)),
            pl.BlockSpec((None, B_HEADS, B_N, B_N), lambda b, n: (b, 0, 0, 0)),
        ],
        out_shape=[
            jax.ShapeDtypeStruct((bsz * seq, B_C), F32 if precise else BF16),
            jax.ShapeDtypeStruct((bsz, 1, B_WIDTH), F32),
            jax.ShapeDtypeStruct((bsz, B_HEADS, B_N, B_N), F32),
        ],
        scratch_shapes=[pltpu.VMEM((1, B_WIDTH), F32), pltpu.VMEM((B_HEADS, B_N, B_N), F32)],
        compiler_params=_cparams(("arbitrary", "arbitrary")),
        name="mixer_b",
    )(pb, shift_prev, s0, vec(mu), vec(w0), w2p, vec(a0), a2p, g2,
      vec(k_k), vec(k_a), vec(r_k), vec(ln_w), vec(ln_b))


def _kv_expand(ckvn, krope, wuk_ref, wuv_ref, gk_ref, k_ref, v_ref, dot):
    k0 = dot(ckvn, wuk_ref[...])
    gk = gk_ref[...]
    for h in range(C_HEADS):
        kh = k0[:, h * C_HPAD:(h + 1) * C_HPAD] + krope
        ss = jnp.sum(kh * kh, axis=-1, keepdims=True) * (1.0 / C_QK)
        k_ref[:, h * C_HPAD:(h + 1) * C_HPAD] = (kh * lax.rsqrt(ss + EPS) * gk).astype(k_ref.dtype)
    v_ref[...] = dot(ckvn, wuv_ref[...]).astype(v_ref.dtype)


def _c_prep_kernel(pc_ref, cos_ref, sin_ref, qn_ref, kvn_ref, wq_ref, wqr_ref, wuk_ref, wuv_ref, gq_ref, gk_ref,
                   q_ref, k_ref, v_ref, ckv_ref, kr_ref, *, precise):
    dot = _mdot(precise)
    pc = pc_ref[...]
    cq = pc[:, 0:C_Q_LORA]
    ckv_raw = pc[:, C_Q_LORA:C_Q_LORA + C_KV_LORA]
    krp = pc[:, C_Q_LORA + C_KV_LORA:C_Q_LORA + C_KV_LORA + LANES]
    krr = pc[:, C_Q_LORA + C_KV_LORA + LANES:C_SEG]
    cos = cos_ref[...]
    sin = sin_ref[...]
    lane = lax.broadcasted_iota(I32, cos.shape, 1)
    cosq = jnp.where(lane < C_NOPE, 1.0, cos)

    cqn = _rms(cq) * qn_ref[...]
    q0 = dot(cqn, wq_ref[...])
    q1 = dot(cqn, wqr_ref[...])
    gq = gq_ref[...] * (C_QK ** -0.5)
    for h in range(C_HEADS):
        sl = slice(h * C_HPAD, (h + 1) * C_HPAD)
        qh = q0[:, sl] * cosq + q1[:, sl] * sin
        ss = jnp.sum(qh * qh, axis=-1, keepdims=True) * (1.0 / C_QK)
        q_ref[:, sl] = (qh * lax.rsqrt(ss + EPS) * gq).astype(q_ref.dtype)

    ckvn = _rms(ckv_raw) * kvn_ref[...]
    ckv_ref[...] = ckvn
    krope = krp * cos + krr * sin
    kr_ref[...] = krope[:, C_NOPE:C_NOPE + C_ROPE]
    _kv_expand(ckvn, krope, wuk_ref, wuv_ref, gk_ref, k_ref, v_ref, dot)


def _kv_expand_kernel(ckv_ref, kr_ref, wuk_ref, wuv_ref, gk_ref, k_ref, v_ref, *, precise):
    _kv_expand(ckv_ref[...], kr_ref[...], wuk_ref, wuv_ref, gk_ref, k_ref, v_ref, _mdot(precise))


def _c_weights(q_norm, kv_norm, w_uq, w_ukv, q_gain, k_gain):
    half = C_ROPE // 2
    wq = jnp.zeros((C_Q_LORA, C_HEADS * C_HPAD), F32)
    wqr = jnp.zeros((C_Q_LORA, C_HEADS * C_HPAD), F32)
    wuk = jnp.zeros((C_KV_LORA, C_HEADS * C_HPAD), F32)
    wuv = jnp.zeros((C_KV_LORA, C_HEADS * C_VDIM), F32)
    for h in range(C_HEADS):
        wh = w_uq[:, h * C_QK:(h + 1) * C_QK]
        wq = wq.at[:, h * C_HPAD:h * C_HPAD + C_QK].set(wh)
        rot = jnp.concatenate([-wh[:, C_NOPE + half:], wh[:, C_NOPE:C_NOPE + half]], axis=1)
        wqr = wqr.at[:, h * C_HPAD + C_NOPE:h * C_HPAD + C_QK].set(rot)
        kvh = w_ukv[:, h * (C_NOPE + C_VDIM):(h + 1) * (C_NOPE + C_VDIM)]
        wuk = wuk.at[:, h * C_HPAD:h * C_HPAD + C_NOPE].set(kvh[:, :C_NOPE])
        wuv = wuv.at[:, h * C_VDIM:(h + 1) * C_VDIM].set(kvh[:, C_NOPE:])
    gain = lambda g: _row(jnp.concatenate([g[:C_NOPE], g[C_NOPE:], g[C_NOPE:]]))
    return (q_norm.reshape(1, -1), kv_norm.reshape(1, -1), wq, wqr, wuk, wuv, gain(q_gain), gain(k_gain))


def _rope_tables(pos):
    half = C_ROPE // 2
    inv = ROPE_THETA ** (-(jnp.arange(0, C_ROPE, 2, dtype=F32) / C_ROPE))
    ang = pos.astype(F32)[:, None] * inv[None, :]
    cos, sin = jnp.cos(ang), jnp.sin(ang)
    pad = lambda t: jnp.pad(jnp.concatenate([t, t], axis=1), ((0, 0), (C_NOPE, LANES - C_NOPE - 2 * half)))
    return pad(cos), pad(sin)


def _c_prep(pc, cos, sin, cw, precise):
    t = pc.shape[0]
    tm = _pick(t, (512, 256, 128, 64, 32, 16, 8))
    qn, kvn, wq, wqr, wuk, wuv, gq, gk = cw
    full = lambda a: pl.BlockSpec(a.shape, lambda i: (0,) * a.ndim)
    rows = lambda w: pl.BlockSpec((tm, w), lambda i: (i, 0))
    act = F32 if precise else BF16
    return pl.pallas_call(
        functools.partial(_c_prep_kernel, precise=precise),
        grid=(t // tm,),
        in_specs=[rows(C_SEG), rows(LANES), rows(LANES)] + [full(a) for a in (qn, kvn, wq, wqr, wuk, wuv, gq, gk)],
        out_specs=[rows(C_HEADS * C_HPAD), rows(C_HEADS * C_HPAD), rows(C_HEADS * C_VDIM), rows(C_KV_LORA),
                   rows(C_ROPE)],
        out_shape=[
            jax.ShapeDtypeStruct((t, C_HEADS * C_HPAD), act),
            jax.ShapeDtypeStruct((t, C_HEADS * C_HPAD), act),
            jax.ShapeDtypeStruct((t, C_HEADS * C_VDIM), act),
            jax.ShapeDtypeStruct((t, C_KV_LORA), F32),
            jax.ShapeDtypeStruct((t, C_ROPE), F32),
        ],
        compiler_params=_cparams(("arbitrary",)),
        name="c_prep",
    )(pc, cos, sin, qn, kvn, wq, wqr, wuk, wuv, gq, gk)


def _kv_expand_call(ckv, krp, cw, precise):
    t = ckv.shape[0]
    tm = _pick(t, (1024, 512, 256, 128, 64, 32, 16, 8))
    _, _, _, _, wuk, wuv, _, gk = cw
    full = lambda a: pl.BlockSpec(a.shape, lambda i: (0,) * a.ndim)
    rows = lambda w: pl.BlockSpec((tm, w), lambda i: (i, 0))
    act = F32 if precise else BF16
    return pl.pallas_call(
        functools.partial(_kv_expand_kernel, precise=precise),
        grid=(t // tm,),
        in_specs=[rows(C_KV_LORA), rows(LANES), full(wuk), full(wuv), full(gk)],
        out_specs=[rows(C_HEADS * C_HPAD), rows(C_HEADS * C_VDIM)],
        out_shape=[
            jax.ShapeDtypeStruct((t, C_HEADS * C_HPAD), act),
            jax.ShapeDtypeStruct((t, C_HEADS * C_VDIM), act),
        ],
        compiler_params=_cparams(("arbitrary",)),
        name="kv_expand",
    )(ckv, krp, wuk, wuv, gk)


def _attn_kernel(*refs, tq, nq, past, has_cache, precise):
    if has_cache:
        q_ref, kn_ref, vn_ref, kc_ref, vc_ref, o_ref = refs
    else:
        q_ref, kn_ref, vn_ref, o_ref = refs
    dot = _mdot(precise)
    qi = pl.program_id(1)
    shift = CHUNK.bit_length() - 1

    q = q_ref[...]
    q_chunk = jnp.right_shift(past + qi * tq + lax.broadcasted_iota(I32, (tq, 1), 0), shift)
    qh = [q[:, h * C_HPAD:(h + 1) * C_HPAD] for h in range(C_HEADS)]
    init = tuple((jnp.full((tq, 1), -1e30, F32), jnp.zeros((tq, 1), F32), jnp.zeros((tq, C_VDIM), F32))
                 for _ in range(C_HEADS))

    def step(state, kb, vb, p0, w):
        vis = jnp.right_shift(p0 + lax.broadcasted_iota(I32, (1, w), 1), shift) <= q_chunk
        new = []
        for h in range(C_HEADS):
            m, den, acc = state[h]
            s = jnp.where(vis, dot(qh[h], kb[:, h * C_HPAD:(h + 1) * C_HPAD], _NT), -1e30)
            m_new = jnp.maximum(m, jnp.max(s, axis=-1, keepdims=True))
            scale = jnp.exp(m - m_new)
            p = jnp.exp(s - m_new)
            new.append((m_new, den * scale + jnp.sum(p, axis=-1, keepdims=True),
                        acc * scale + dot(p, vb[:, h * C_VDIM:(h + 1) * C_VDIM])))
        return tuple(new)

    def finish(state):
        o_ref[...] = jnp.concatenate([acc / den for _, den, acc in state], axis=1).astype(o_ref.dtype)

    if precise:
        def new_step(j, st):
            r0 = pl.multiple_of(j * tq, tq)
            return step(st, kn_ref[pl.ds(r0, tq), :], vn_ref[pl.ds(r0, tq), :], past + r0, tq)

        state = lax.fori_loop(0, qi + 1, new_step, init)
        if has_cache:
            tc = _pick(past, (ATTN_KEY_BLOCK, 256, 128, 64, 32, 16, 8))

            def cache_step(j, st):
                r0 = pl.multiple_of(j * tc, tc)
                return step(st, kc_ref[pl.ds(r0, tc), :], vc_ref[pl.ds(r0, tc), :], r0, tc)

            state = lax.fori_loop(0, past // tc, cache_step, state)
        finish(state)
    else:
        def attend(n_new):
            blocks = [(kn_ref, vn_ref, r0, min(ATTN_KEY_BLOCK, n_new - r0), past + r0)
                      for r0 in range(0, n_new, ATTN_KEY_BLOCK)]
            if has_cache:
                blocks += [(kc_ref, vc_ref, r0, min(ATTN_KEY_BLOCK, past - r0), r0)
                           for r0 in range(0, past, ATTN_KEY_BLOCK)]
            state = init
            for k_ref, v_ref, r0, w, p0 in blocks:
                state = step(state, k_ref[r0:r0 + w, :], v_ref[r0:r0 + w, :], p0, w)
            finish(state)

        if nq == 1:
            attend(tq)
        else:
            for blk in range(nq):
                pl.when(qi == blk)(functools.partial(attend, (blk + 1) * tq))


def _attention(q, k, v, row0, bsz, seq, precise, k_cache=None, v_cache=None):
    has_cache = k_cache is not None
    past = k_cache.shape[0] // bsz if has_cache else 0
    tq = _pick(seq, (256, 128, 64, 32, 16, 8))
    assert tq % CHUNK == 0 or tq == seq
    nq = seq // tq
    kern = functools.partial(_attn_kernel, tq=tq, nq=nq, past=past, has_cache=has_cache, precise=precise)
    qw, vw = C_HEADS * C_HPAD, C_HEADS * C_VDIM
    in_specs = [
        pl.BlockSpec((tq, qw), lambda b, i: (row0 // tq + b * nq + i, 0)),
        pl.BlockSpec((seq, qw), lambda b, i: (row0 // seq + b, 0)),
        pl.BlockSpec((seq, vw), lambda b, i: (row0 // seq + b, 0)),
    ]
    args = [q, k, v]
    if has_cache:
        in_specs += [pl.BlockSpec((past, qw), lambda b, i: (b, 0)), pl.BlockSpec((past, vw), lambda b, i: (b, 0))]
        args += [k_cache, v_cache]
    return pl.pallas_call(
        kern,
        grid=(bsz, nq),
        in_specs=in_specs,
        out_specs=pl.BlockSpec((tq, vw), lambda b, i: (b * nq + i, 0)),
        out_shape=jax.ShapeDtypeStruct((bsz * seq, vw), F32 if precise else BF16),
        compiler_params=_cparams(("arbitrary", "arbitrary")),
        name="attention",
    )(*args)


def _out_proj_kernel(*refs, n_prompt_tiles, precise, split_x):
    dot = _mdot(precise)
    is_p = pl.program_id(0) < n_prompt_tiles
    if split_x:
        y = jnp.where(is_p, refs[0][...], refs[1][...])
        refs = refs[2:]
    else:
        y = refs[0][...]
        refs = refs[1:]
    (oap_ref, obp_ref, ocp_ref, oas_ref, obs_ref, ocs_ref, w_ref, g_ref, wr_ref, br_ref,
     y_ref, xt_ref, ri_ref, rw_ref) = refs
    oa = jnp.where(is_p, oap_ref[...], oas_ref[...])
    ob = jnp.where(is_p, obp_ref[...], obs_ref[...])
    oc = jnp.where(is_p, ocp_ref[...], ocs_ref[...])
    y = y + dot(oa, w_ref[0:A_V])
    y = y + dot(ob, w_ref[A_V:A_V + B_C])
    y = y + dot(oc, w_ref[A_V + B_C:D_MIX])
    y_ref[...] = y
    xn = _rms(y) * g_ref[...]
    _rows_to_tiles(xt_ref, xn)

    logits = _dot3(xn, wr_ref[...]) + br_ref[...]
    lane = lax.broadcasted_iota(I32, logits.shape, 1)
    lanef = lane.astype(F32)
    neg = -1e30
    lg = jnp.where(lane < N_GROUPS, logits, neg)
    mg = jnp.max(lg, axis=-1, keepdims=True)
    pg = 1.0 / jnp.sum(jnp.exp(lg - mg), axis=-1, keepdims=True)
    gidx = jnp.min(jnp.where(lg == mg, lanef, float(LANES)), axis=-1, keepdims=True)
    lo = N_GROUPS + EXPERTS_PER_GROUP * gidx
    in_grp = (lanef >= lo) & (lanef < lo + EXPERTS_PER_GROUP)
    el = jnp.where(in_grp, logits, neg)
    m1 = jnp.max(el, axis=-1, keepdims=True)
    i1 = jnp.min(jnp.where(el == m1, lanef, float(LANES)), axis=-1, keepdims=True)
    el2 = jnp.where(lanef == i1, neg, el)
    m2 = jnp.max(el2, axis=-1, keepdims=True)
    i2 = jnp.min(jnp.where(el2 == m2, lanef, float(LANES)), axis=-1, keepdims=True)
    den = jnp.sum(jnp.exp(el - m1), axis=-1, keepdims=True)
    p1 = 1.0 / den
    p2 = jnp.exp(m2 - m1) / den
    w1 = pg * p1 / (p1 + p2)
    w2 = pg * p2 / (p1 + p2)
    e1 = (i1 - N_GROUPS).astype(I32)
    e2 = (i2 - N_GROUPS).astype(I32)
    ri_ref[...] = jnp.where(lane == 0, e1, jnp.where(lane == 1, e2, 0))
    rw_ref[...] = jnp.where(lane == 0, w1, jnp.where(lane == 1, w2, 0.0))


def _out_proj(xs, mix_p, mix_s, w_out, g, w_router, b_router, precise):
    tp, ts = mix_p[0].shape[0], mix_s[0].shape[0]
    t = tp + ts
    tm = _pick(ts, (512, 256, 128, 64, 32, 16, 8))
    assert tp % tm == 0 and sum(x.shape[0] for x in xs) == t
    npt = tp // tm
    kern = functools.partial(_out_proj_kernel, n_prompt_tiles=npt, precise=precise, split_x=len(xs) == 2)
    pspec = lambda w: pl.BlockSpec((tm, w), lambda i: (jnp.minimum(i, npt - 1), 0))
    sspec = lambda w: pl.BlockSpec((tm, w), lambda i: (jnp.maximum(i - npt, 0), 0))
    full = lambda a: pl.BlockSpec(a.shape, lambda i: (0,) * a.ndim)
    rows = lambda w: pl.BlockSpec((tm, w), lambda i: (i, 0))
    cw = C_HEADS * C_VDIM
    g2 = g.reshape(1, D_MODEL)
    x_specs = [pspec(D_MODEL), sspec(D_MODEL)] if len(xs) == 2 else [rows(D_MODEL)]
    return pl.pallas_call(
        kern,
        grid=(t // tm,),
        in_specs=x_specs + [pspec(A_V), pspec(B_C), pspec(cw), sspec(A_V), sspec(B_C), sspec(cw),
                            full(w_out), full(g2), full(w_router), full(b_router)],
        out_specs=[rows(D_MODEL), pl.BlockSpec((tm * SUBLANES, LANES), lambda i: (i, 0)), rows(LANES), rows(LANES)],
        out_shape=[
            jax.ShapeDtypeStruct((t, D_MODEL), F32),
            jax.ShapeDtypeStruct((t * SUBLANES, LANES), F32),
            jax.ShapeDtypeStruct((t, LANES), I32),
            jax.ShapeDtypeStruct((t, LANES), F32),
        ],
        compiler_params=_cparams(("arbitrary",)),
        name="out_proj_router",
    )(*xs, *mix_p, *mix_s, w_out, g2, w_router, b_router)


def _rank_kernel(ri_ref, rank_ref, cnt_ref, base_ref):
    @pl.when(pl.program_id(0) == 0)
    def _():
        base_ref[...] = jnp.zeros_like(base_ref)

    ri = ri_ref[...]
    tm = ri.shape[0]
    lane = lax.broadcasted_iota(I32, ri.shape, 1)
    oh0 = lane == ri[:, 0:1]
    oh1 = lane == ri[:, 1:2]
    cnt = oh0.astype(F32) + oh1.astype(F32)
    row = lax.broadcasted_iota(I32, (tm, tm), 0)
    col = lax.broadcasted_iota(I32, (tm, tm), 1)
    before = jnp.dot((row > col).astype(BF16), cnt.astype(BF16), preferred_element_type=F32) + base_ref[...]
    r0 = jnp.sum(jnp.where(oh0, before, 0.0), axis=-1, keepdims=True)
    r1 = jnp.sum(jnp.where(oh1, before, 0.0), axis=-1, keepdims=True)
    rank_ref[...] = jnp.where(lane == 0, r0, jnp.where(lane == 1, r1, 0.0)).astype(I32)
    total = base_ref[...] + jnp.sum(cnt, axis=0, keepdims=True)
    base_ref[...] = total
    cnt_ref[...] = total.astype(I32)


def _rank(ri):
    t = ri.shape[0]
    tm = _pick(t, (256, 128, 64, 32, 16, 8))
    return pl.pallas_call(
        _rank_kernel,
        grid=(t // tm,),
        in_specs=[pl.BlockSpec((tm, LANES), lambda i: (i, 0))],
        out_specs=[pl.BlockSpec((tm, LANES), lambda i: (i, 0)), pl.BlockSpec((1, LANES), lambda i: (0, 0))],
        out_shape=[jax.ShapeDtypeStruct((t, LANES), I32), jax.ShapeDtypeStruct((1, LANES), I32)],
        scratch_shapes=[pltpu.VMEM((1, LANES), F32)],
        compiler_params=_cparams(("arbitrary",)),
        name="moe_rank",
    )(ri)


def _gather_tiles(idx_ref, n, src_hbm, buf, slot, sem):
    def body(j, carry):
        for queue in range(2):
            r = 2 * j + queue
            src = pl.multiple_of(idx_ref[0, r] * SUBLANES, SUBLANES)
            dst = pl.multiple_of((slot * n + r) * SUBLANES, SUBLANES)
            pltpu.make_async_copy(src_hbm.at[pl.ds(src, SUBLANES)], buf.at[pl.ds(dst, SUBLANES)],
                                  sem.at[slot]).start(priority=queue)
        return carry

    lax.fori_loop(0, n // 2, body, 0, unroll=4)


def _gather_wait(n, src_hbm, buf, slot, sem):
    base = pl.multiple_of(slot * n * SUBLANES, SUBLANES)
    pltpu.make_async_copy(src_hbm.at[pl.ds(0, n * SUBLANES)], buf.at[pl.ds(base, n * SUBLANES)], sem.at[slot]).wait()
    return base


def _expert_kernel(te_ref, nu_ref, idc_ref, idn_ref, xt_hbm, wg_ref, wu_ref, wd_ref, o_ref,
                   xbuf, wgb, wub, wdb, sem):
    i = pl.program_id(0)
    n_used = nu_ref[0]
    slot = lax.rem(i, 2)

    @pl.when((i == 0) & (n_used > 0))
    def _():
        _gather_tiles(idc_ref, MOE_TILE, xt_hbm, xbuf, 0, sem)

    @pl.when(i + 1 < n_used)
    def _():
        _gather_tiles(idn_ref, MOE_TILE, xt_hbm, xbuf, 1 - slot, sem)

    @pl.when((i == 0) | (te_ref[i] != te_ref[jnp.maximum(i - 1, 0)]))
    def _():
        wgb[...] = wg_ref[...].astype(BF16)
        wub[...] = wu_ref[...].astype(BF16)
        wdb[...] = wd_ref[...].astype(BF16)

    @pl.when(i < n_used)
    def _():
        base = _gather_wait(MOE_TILE, xt_hbm, xbuf, slot, sem)
        x = _tiles_to_rows(xbuf, base, MOE_TILE).astype(BF16)
        hg = jnp.dot(x, wgb[...], preferred_element_type=F32)
        hu = jnp.dot(x, wub[...], preferred_element_type=F32)
        hidden = (_silu(hg) * hu).astype(BF16)
        _rows_to_tiles(o_ref, jnp.dot(hidden, wdb[...], preferred_element_type=F32))

    @pl.when(i >= n_used)
    def _():
        o_ref[...] = jnp.zeros_like(o_ref)


def _expert_ffn(xt, token_of_pos, tile_expert, n_used, e_gate, e_up, e_down):
    nt = token_of_pos.shape[0] // MOE_TILE
    idx = token_of_pos.reshape(nt, 1, MOE_TILE)
    tile_rows = MOE_TILE * SUBLANES
    smem = lambda imap: pl.BlockSpec((None, 1, MOE_TILE), imap, memory_space=pltpu.SMEM)
    grid_spec = pltpu.PrefetchScalarGridSpec(
        num_scalar_prefetch=2,
        grid=(nt,),
        in_specs=[
            smem(lambda i, te, nu: (i, 0, 0)),
            smem(lambda i, te, nu: (jnp.minimum(i + 1, nt - 1), 0, 0)),
            pl.BlockSpec(memory_space=pl.ANY),
            pl.BlockSpec((None, D_MODEL, D_EXPERT), lambda i, te, nu: (te[i], 0, 0)),
            pl.BlockSpec((None, D_MODEL, D_EXPERT), lambda i, te, nu: (te[i], 0, 0)),
            pl.BlockSpec((None, D_EXPERT, D_MODEL), lambda i, te, nu: (te[i], 0, 0)),
        ],
        out_specs=pl.BlockSpec((tile_rows, LANES), lambda i, te, nu: (i, 0)),
        scratch_shapes=[pltpu.VMEM((2 * tile_rows, LANES), F32),
                        pltpu.VMEM((D_MODEL, D_EXPERT), BF16), pltpu.VMEM((D_MODEL, D_EXPERT), BF16),
                        pltpu.VMEM((D_EXPERT, D_MODEL), BF16), pltpu.SemaphoreType.DMA((2,))],
    )
    return pl.pallas_call(
        _expert_kernel,
        grid_spec=grid_spec,
        out_shape=jax.ShapeDtypeStruct((nt * tile_rows, LANES), F32),
        compiler_params=_cparams(("arbitrary",)),
        name="expert_ffn",
    )(tile_expert, n_used, idx, idx, xt, e_gate, e_up, e_down)


def _combine_kernel(pc_ref, pn_ref, x_ref, rw_ref, ot_hbm, *refs, tm, n_first_tiles):
    i = pl.program_id(0)
    nt = pl.num_programs(0)
    slot = lax.rem(i, 2)
    n = TOP_K * tm
    obuf, sem = refs[-2:]

    @pl.when(i == 0)
    def _():
        _gather_tiles(pc_ref, n, ot_hbm, obuf, 0, sem)

    @pl.when(i + 1 < nt)
    def _():
        _gather_tiles(pn_ref, n, ot_hbm, obuf, 1 - slot, sem)

    base = _gather_wait(n, ot_hbm, obuf, slot, sem)
    rw = rw_ref[...]
    o0 = _tiles_to_rows(obuf, base, tm)
    o1 = _tiles_to_rows(obuf, base + tm * SUBLANES, tm)
    y = x_ref[...] + rw[:, 0:1] * o0 + rw[:, 1:2] * o1
    if n_first_tiles is None:
        refs[0][...] = y
    else:
        @pl.when(i < n_first_tiles)
        def _():
            refs[0][...] = y

        @pl.when(i >= n_first_tiles)
        def _():
            refs[1][...] = y


def _combine(x, out_tiles, pos, rw, split_rows=None):
    t = x.shape[0]
    rows = lambda w: pl.BlockSpec((tm, w), lambda i: (i, 0))
    if split_rows is None:
        tm = _pick(t, (256, 128, 64, 32, 16, 8))
        nft = None
        out_specs = rows(D_MODEL)
        out_shape = jax.ShapeDtypeStruct((t, D_MODEL), F32)
    else:
        tm = _pick(t - split_rows, (256, 128, 64, 32, 16, 8))
        assert split_rows % tm == 0
        nft = split_rows // tm
        out_specs = [pl.BlockSpec((tm, D_MODEL), lambda i: (jnp.minimum(i, nft - 1), 0)),
                     pl.BlockSpec((tm, D_MODEL), lambda i: (jnp.maximum(i - nft, 0), 0))]
        out_shape = [jax.ShapeDtypeStruct((split_rows, D_MODEL), F32),
                     jax.ShapeDtypeStruct((t - split_rows, D_MODEL), F32)]
    nt = t // tm
    n = TOP_K * tm
    idx = pos.reshape(nt, tm, TOP_K).transpose(0, 2, 1).reshape(nt, 1, n)
    kern = functools.partial(_combine_kernel, tm=tm, n_first_tiles=nft)
    smem = lambda imap: pl.BlockSpec((None, 1, n), imap, memory_space=pltpu.SMEM)
    return pl.pallas_call(
        kern,
        grid=(nt,),
        in_specs=[smem(lambda i: (i, 0, 0)), smem(lambda i: (jnp.minimum(i + 1, nt - 1), 0, 0)),
                  rows(D_MODEL), rows(LANES), pl.BlockSpec(memory_space=pl.ANY)],
        out_specs=out_specs,
        out_shape=out_shape,
        scratch_shapes=[pltpu.VMEM((2 * n * SUBLANES, LANES), F32), pltpu.SemaphoreType.DMA((2,))],
        compiler_params=_cparams(("arbitrary",)),
        name="moe_combine",
    )(idx, idx, x, rw, out_tiles)


def _route_tables(eid, rank, counts):
    t = eid.shape[0]
    padded = ((counts + MOE_TILE - 1) // MOE_TILE) * MOE_TILE
    ends = jnp.cumsum(padded)
    starts = ends - padded
    experts = jnp.arange(N_EXPERTS, dtype=I32)
    pos = jnp.sum(jnp.where(eid[:, :, None] == experts, starts, 0), axis=-1) + rank
    np_rows = -(-(TOP_K * t + N_EXPERTS * MOE_TILE) // MOE_TILE) * MOE_TILE
    token_of_pos = jnp.zeros((np_rows,), I32).at[pos.reshape(-1)].set(jnp.arange(TOP_K * t, dtype=I32) // TOP_K)
    tile_start = jnp.arange(np_rows // MOE_TILE, dtype=I32) * MOE_TILE
    tile_expert = jnp.minimum(jnp.sum(tile_start[:, None] >= ends[None, :], axis=1), N_EXPERTS - 1).astype(I32)
    n_used = (ends[-1] // MOE_TILE).astype(I32).reshape(1)
    return pos.astype(I32), token_of_pos, tile_expert, n_used


def _moe(y, xt, ri, rw, e_gate, e_up, e_down, split_rows):
    rank, counts = _rank(ri)
    pos, token_of_pos, tile_expert, n_used = _route_tables(ri[:, :TOP_K], rank[:, :TOP_K], counts[0, :N_EXPERTS])
    out_tiles = _expert_ffn(xt, token_of_pos, tile_expert, n_used, e_gate, e_up, e_down)
    return _combine(y, out_tiles, pos, rw, split_rows)


def _in_weights(w_in):
    half = C_ROPE // 2
    zeros = lambda n: jnp.zeros((D_MODEL, n), F32)
    c0 = A_WIDTH + B_WIDTH
    kr = w_in[:, c0 + C_Q_LORA + C_KV_LORA:c0 + C_WIDTH]
    kr_rot = jnp.concatenate([-kr[:, half:], kr[:, :half]], axis=1)
    tail = LANES - C_NOPE - C_ROPE
    cols = [
        w_in[:, :A_CONV_CH + A_V], w_in[:, A_CONV_CH + A_V:A_WIDTH], zeros(LANES - 2 * A_HEADS),
        w_in[:, A_WIDTH:c0],
        w_in[:, c0:c0 + C_Q_LORA + C_KV_LORA],
        zeros(C_NOPE), kr, zeros(tail),
        zeros(C_NOPE), kr_rot, zeros(tail),
    ]
    return jnp.concatenate(cols, axis=1)


def _router_weights(rg, rgb, re, reb):
    w = jnp.zeros((D_MODEL, LANES), F32).at[:, :N_GROUPS].set(rg).at[:, N_GROUPS:N_GROUPS + N_EXPERTS].set(re)
    b = jnp.zeros((1, LANES), F32).at[0, :N_GROUPS].set(rgb).at[0, N_GROUPS:N_GROUPS + N_EXPERTS].set(reb)
    return w, b


def _layer(xs, geom, st_p, st_s, cos, sin, wts, precise, split_out):
    (norm_mix, w_in, a_conv_w, a_A_log, a_dt_bias, a_norm_w,
     b_mu, b_w0, b_w2, b_a0, b_a2, b_g2, b_k_k, b_k_a, b_r_k, b_ln_w, b_ln_b,
     c_q_norm, c_kv_norm, c_w_uq, c_w_ukv, c_q_gain, c_k_gain,
     w_out, norm_ffn, router_group, router_group_bias, router_expert, router_expert_bias,
     e_gate, e_up, e_down) = wts
    bp, lp, bs, ls, past = geom
    tp = bp * lp

    pa, pb, pc = _in_proj(xs, norm_mix, _in_weights(w_in), precise)
    cw = _c_weights(c_q_norm, c_kv_norm, c_w_uq, c_w_ukv, c_q_gain, c_k_gain)
    q, k, v, ckv, krope = _c_prep(pc, cos, sin, cw, precise)

    a_args = (a_conv_w, a_A_log, a_dt_bias, a_norm_w)
    b_args = (b_mu, b_w0, b_w2, b_a0, b_a2, b_g2, b_k_k, b_k_a, b_r_k.reshape(-1), b_ln_w, b_ln_b)
    conv_p, delta_p, shift_p, wkv_p = st_p
    conv_s, delta_s, shift_s, wkv_s, ckv_past, krope_past = st_s

    oa_p, conv_np, delta_np = _mixer_a(pa, 0, bp, lp, precise, conv_p, delta_p, *a_args)
    oa_s, conv_ns, delta_ns = _mixer_a(pa, tp, bs, ls, precise, conv_s, delta_s, *a_args)
    ob_p, shift_np, wkv_np = _mixer_b(pb, 0, bp, lp, precise, shift_p, wkv_p, *b_args)
    ob_s, shift_ns, wkv_ns = _mixer_b(pb, tp, bs, ls, precise, shift_s, wkv_s, *b_args)

    oc_p = _attention(q, k, v, 0, bp, lp, precise)
    krp_past = jnp.pad(krope_past.reshape(bs * past, C_ROPE), ((0, 0), (C_NOPE, LANES - C_NOPE - C_ROPE)))
    k_cache, v_cache = _kv_expand_call(ckv_past.reshape(bs * past, C_KV_LORA), krp_past, cw, precise)
    oc_s = _attention(q, k, v, tp, bs, ls, precise, k_cache, v_cache)

    w_router, b_router = _router_weights(router_group, router_group_bias, router_expert, router_expert_bias)
    y, xt, ri, rw = _out_proj(xs, (oa_p, ob_p, oc_p), (oa_s, ob_s, oc_s), w_out, norm_ffn,
                              w_router, b_router, precise)
    x_new = _moe(y, xt, ri, rw, e_gate, e_up, e_down, tp if split_out else None)

    new_p = (conv_np, delta_np, shift_np, wkv_np, ckv[:tp].reshape(bp, lp, C_KV_LORA),
             krope[:tp].reshape(bp, lp, C_ROPE))
    new_s = (conv_ns, delta_ns, shift_ns, wkv_ns, ckv[tp:].reshape(bs, ls, C_KV_LORA),
             krope[tp:].reshape(bs, ls, C_ROPE))
    return x_new, new_p, new_s


def _forward(x_prompt, x_sample, cache_c_kv, cache_k_rope, state_conv_a, state_delta_a, state_shift_b,
             state_wkv_b, weights):
    bp, lp, _ = x_prompt.shape
    bs, ls, _ = x_sample.shape
    depth = cache_c_kv.shape[0]
    past = cache_c_kv.shape[2]
    geom = (bp, lp, bs, ls, past)
    xs = [x_prompt.reshape(bp * lp, D_MODEL), x_sample.reshape(bs * ls, D_MODEL)]
    pos = jnp.concatenate([jnp.tile(jnp.arange(lp), bp), jnp.tile(past + jnp.arange(ls), bs)])
    cos, sin = _rope_tables(pos)
    zeros = lambda *s: jnp.zeros(s, F32)
    st_p = (zeros(bp, A_CONV - 1, A_CONV_CH), zeros(bp, A_HEADS, A_DK, A_DV), zeros(bp, 1, B_WIDTH),
            zeros(bp, B_HEADS, B_N, B_N))
    news_p, news_s = [], []
    for l in range(depth):
        st_s = (state_conv_a[l], state_delta_a[l], state_shift_b[l], state_wkv_b[l], cache_c_kv[l], cache_k_rope[l])
        last = l == depth - 1
        x, new_p, new_s = _layer(xs, geom, st_p, st_s, cos, sin, [w[l] for w in weights], precise=not last,
                                 split_out=last)
        xs = x if last else [x]
        news_p.append(new_p)
        news_s.append(new_s)
    stack = lambda news, i: jnp.stack([n[i] for n in news])
    y_prompt = xs[0].reshape(bp, lp, D_MODEL)
    y_sample = xs[1].reshape(bs, ls, D_MODEL)
    p_conv, p_delta, p_shift, p_wkv, p_ckv, p_krope = (stack(news_p, i) for i in range(6))
    s_conv, s_delta, s_shift, s_wkv, s_ckv, s_krope = (stack(news_s, i) for i in range(6))
    return (y_prompt, y_sample, p_ckv, p_krope, p_conv, p_delta, p_shift, p_wkv,
            s_ckv, s_krope, s_conv, s_delta, s_shift, s_wkv)


def kernel(x_prompt, x_sample, cache_c_kv, cache_k_rope, state_conv_a, state_delta_a, state_shift_b, state_wkv_b,
           norm_mix, w_in, a_conv_w, a_A_log, a_dt_bias, a_norm_w,
           b_mu, b_w0, b_w2, b_a0, b_a2, b_g2, b_k_k, b_k_a, b_r_k, b_ln_w, b_ln_b,
           c_q_norm, c_kv_norm, c_w_uq, c_w_ukv, c_q_gain, c_k_gain,
           w_out, norm_ffn, router_group, router_group_bias, router_expert, router_expert_bias,
           e_gate, e_up, e_down):
    weights = (norm_mix, w_in, a_conv_w, a_A_log, a_dt_bias, a_norm_w,
               b_mu, b_w0, b_w2, b_a0, b_a2, b_g2, b_k_k, b_k_a, b_r_k, b_ln_w, b_ln_b,
               c_q_norm, c_kv_norm, c_w_uq, c_w_ukv, c_q_gain, c_k_gain,
               w_out, norm_ffn, router_group, router_group_bias, router_expert, router_expert_bias,
               e_gate, e_up, e_down)
    return _forward(x_prompt, x_sample, cache_c_kv, cache_k_rope, state_conv_a, state_delta_a, state_shift_b,
                    state_wkv_b, weights)
```

```python
import functools

import jax
import jax.numpy as jnp
from jax import lax
from jax.experimental import pallas as pl
from jax.experimental.pallas import tpu as pltpu

F32 = jnp.float32
BF16 = jnp.bfloat16
I32 = jnp.int32

D_MODEL = 1024
DEPTH = 2
CHUNK = 64
EPS = 1e-6
LANES = 128
SUBLANES = 8
ROW_TILES = D_MODEL // LANES

A_HEADS, A_DK, A_DV, A_CONV = 6, 64, 64, 4
A_QK = A_HEADS * A_DK
A_V = A_HEADS * A_DV
A_CONV_CH = 2 * A_QK + A_V
A_WIDTH = A_CONV_CH + A_V + 2 * A_HEADS
A_SEG = A_CONV_CH + A_V + LANES

B_HEADS, B_N = 6, 64
B_C = B_HEADS * B_N
B_W_LORA, B_A_LORA, B_G_LORA = 64, 64, 128
B_WIDTH = 3 * B_C + B_W_LORA + B_A_LORA + B_G_LORA
B_GN_EPS = 64e-5

C_HEADS, C_NOPE, C_ROPE, C_VDIM = 4, 64, 32, 64
C_QK = C_NOPE + C_ROPE
C_Q_LORA, C_KV_LORA = 256, 128
C_WIDTH = C_Q_LORA + C_KV_LORA + C_ROPE
C_SEG = C_Q_LORA + C_KV_LORA + 2 * LANES
C_HPAD = LANES
ROPE_THETA = 10000.0

P_TOTAL = A_WIDTH + B_WIDTH + C_WIDTH
D_MIX = A_V + B_C + C_HEADS * C_VDIM

N_GROUPS, EXPERTS_PER_GROUP = 4, 8
N_EXPERTS = N_GROUPS * EXPERTS_PER_GROUP
TOP_K = 2
D_EXPERT = 256
MOE_TILE = 256
EXPERT_SLOTS = 3
MIXER_CHUNKS_PER_STEP = 4
ATTN_KEY_BLOCK = 2048

VMEM_LIMIT = 48 * 1024 * 1024

_NN = (((1,), (0,)), ((), ()))
_NT = (((1,), (1,)), ((), ()))
_TN = (((0,), (0,)), ((), ()))


def _pick(n, prefs):
    for p in prefs:
        if n % p == 0:
            return p
    raise ValueError(f"no tile for {n} in {prefs}")


def _cparams(sem):
    return pltpu.CompilerParams(dimension_semantics=sem, vmem_limit_bytes=VMEM_LIMIT)


def _bdot(a, b, dims=_NN):
    return lax.dot_general(a.astype(BF16), b.astype(BF16), dims, preferred_element_type=F32)


def _split2(a):
    hi = a.astype(BF16)
    lo = (a - hi.astype(F32)).astype(BF16)
    return hi, lo


def _dot3(a, b, dims=_NN):
    ah, al = _split2(a)
    bh, bl = _split2(b)
    f = lambda x, y: lax.dot_general(x, y, dims, preferred_element_type=F32)
    return f(ah, bh) + (f(ah, bl) + f(al, bh))


def _mdot(precise):
    return _dot3 if precise else _bdot


def _cumsum_rows(x, ltri):
    h = x.astype(BF16)
    r = x - h.astype(F32)
    m = r.astype(BF16)
    l = (r - m.astype(F32)).astype(BF16)
    d = lambda y: jnp.dot(ltri, y, preferred_element_type=F32)
    return d(h) + (d(m) + d(l))


def _head_sums(x, width):
    n = x.shape[1]
    shift = width.bit_length() - 1
    row = jnp.right_shift(lax.broadcasted_iota(I32, (n, n), 0), shift)
    col = jnp.right_shift(lax.broadcasted_iota(I32, (n, n), 1), shift)
    ones_bd = (row == col).astype(BF16)
    h = x.astype(BF16)
    r = x - h.astype(F32)
    m = r.astype(BF16)
    l = (r - m.astype(F32)).astype(BF16)
    d = lambda y: jnp.dot(y, ones_bd, preferred_element_type=F32)
    return d(h) + (d(m) + d(l))


def _tri_masks(c):
    row = lax.broadcasted_iota(I32, (c, c), 0)
    col = lax.broadcasted_iota(I32, (c, c), 1)
    return row >= col, row > col, (row == col).astype(F32)


def _chunk_tril(rows, c):
    shift = c.bit_length() - 1
    row = lax.broadcasted_iota(I32, (rows, rows), 0)
    col = lax.broadcasted_iota(I32, (rows, rows), 1)
    same = jnp.right_shift(row, shift) == jnp.right_shift(col, shift)
    return (same & (row >= col)).astype(BF16)


def _neumann_inv_many(ns, eye):
    c = eye.shape[0]
    keep_t = lax.broadcasted_iota(I32, (c, 2 * c), 1) >= c
    pts = [jnp.concatenate([n, eye], axis=1) for n in ns]
    m = 1
    while m < c:
        rs = [_dot3(pt[:, :c], pt) for pt in pts]
        pts = [r + jnp.where(keep_t, pt, 0.0) for r, pt in zip(rs, pts)]
        m *= 2
    return [pt[:, c:] for pt in pts]


def _softplus(x):
    return jnp.maximum(x, 0.0) + jnp.log1p(jnp.exp(-jnp.abs(x)))


def _silu(x):
    return x * jax.nn.sigmoid(x)


def _rms(x, eps=EPS):
    return x * lax.rsqrt(jnp.mean(x * x, axis=-1, keepdims=True) + eps)


def _rows_to_tiles(ref, val):
    n = val.shape[0]
    for j in range(ROW_TILES):
        ref[pl.ds(j, n, stride=SUBLANES), :] = val[:, j * LANES:(j + 1) * LANES]


def _tiles_to_rows(ref, start, n):
    return jnp.concatenate([ref[pl.ds(start + j, n, stride=SUBLANES), :] for j in range(ROW_TILES)], axis=1)


def _in_proj_kernel(*refs, precise, n_first_tiles):
    if n_first_tiles is None:
        x = refs[0][...]
        g_ref, *refs = refs[1:]
    else:
        x = jnp.where(pl.program_id(0) < n_first_tiles, refs[0][...], refs[1][...])
        g_ref, *refs = refs[2:]
    if precise:
        wh_ref, wl_ref, pa_ref, pb_ref, pc_ref = refs
    else:
        wh_ref, pa_ref, pb_ref, pc_ref = refs
    xn = _rms(x) * g_ref[...]
    xh = xn.astype(BF16)
    if precise:
        xl = (xn - xh.astype(F32)).astype(BF16)
    lo = 0
    for out_ref, width in ((pa_ref, A_SEG), (pb_ref, B_WIDTH), (pc_ref, C_SEG)):
        acc = jnp.dot(xh, wh_ref[:, lo:lo + width], preferred_element_type=F32)
        if precise:
            acc = acc + (jnp.dot(xh, wl_ref[:, lo:lo + width], preferred_element_type=F32)
                         + jnp.dot(xl, wh_ref[:, lo:lo + width], preferred_element_type=F32))
        out_ref[...] = acc
        lo += width


def _split_kernel(w_ref, hi_ref, lo_ref):
    hi, lo = _split2(w_ref[...])
    hi_ref[...] = hi
    lo_ref[...] = lo


def _split_hi_lo(w):
    r, c = w.shape
    tc = _pick(c, (512, 256, 128))
    spec = pl.BlockSpec((r, tc), lambda j: (0, j))
    return pl.pallas_call(
        _split_kernel,
        grid=(c // tc,),
        in_specs=[spec],
        out_specs=[spec, spec],
        out_shape=[jax.ShapeDtypeStruct((r, c), BF16), jax.ShapeDtypeStruct((r, c), BF16)],
        compiler_params=_cparams(("arbitrary",)),
        name="split_hi_lo",
    )(w)


def _in_proj(xs, g, w, precise):
    prefs = (256, 128, 64, 32, 16, 8) if precise else (512, 256, 128, 64, 32, 16, 8)
    t = sum(x.shape[0] for x in xs)
    if len(xs) == 1:
        tm = _pick(t, prefs)
        nft = None
        x_specs = [pl.BlockSpec((tm, D_MODEL), lambda i: (i, 0))]
    else:
        tm = _pick(xs[1].shape[0], prefs)
        assert xs[0].shape[0] % tm == 0
        nft = xs[0].shape[0] // tm
        x_specs = [pl.BlockSpec((tm, D_MODEL), lambda i: (jnp.minimum(i, nft - 1), 0)),
                   pl.BlockSpec((tm, D_MODEL), lambda i: (jnp.maximum(i - nft, 0), 0))]
    wtot = A_SEG + B_WIDTH + C_SEG
    wspec = pl.BlockSpec((D_MODEL, wtot), lambda i: (0, 0))
    w_args = _split_hi_lo(w) if precise else [w.astype(BF16)]
    return pl.pallas_call(
        functools.partial(_in_proj_kernel, precise=precise, n_first_tiles=nft),
        grid=(t // tm,),
        in_specs=x_specs + [pl.BlockSpec((1, D_MODEL), lambda i: (0, 0))] + [wspec] * len(w_args),
        out_specs=[
            pl.BlockSpec((tm, A_SEG), lambda i: (i, 0)),
            pl.BlockSpec((tm, B_WIDTH), lambda i: (i, 0)),
            pl.BlockSpec((tm, C_SEG), lambda i: (i, 0)),
        ],
        out_shape=[
            jax.ShapeDtypeStruct((t, A_SEG), F32),
            jax.ShapeDtypeStruct((t, B_WIDTH), F32),
            jax.ShapeDtypeStruct((t, C_SEG), F32),
        ],
        compiler_params=_cparams(("arbitrary",)),
        name="in_proj",
    )(*xs, g.reshape(1, D_MODEL), *w_args)


def _mixer_a_kernel(pa_ref, cprev_ref, s0_ref, cw_ref, alog_ref, dtb_ref, nw_ref,
                    oa_ref, cnew_ref, snew_ref, tail_ref, s_ref, *, c, nch, n_steps, precise):
    dot = _mdot(precise)
    step = pl.program_id(1)
    rows = c * nch

    @pl.when(step == 0)
    def _():
        tail_ref[...] = jnp.zeros_like(tail_ref)
        tail_ref[SUBLANES - (A_CONV - 1):SUBLANES, :] = cprev_ref[...]
        s_ref[...] = s0_ref[...]

    x = pa_ref[...]
    qkv = x[:, :A_CONV_CH]
    z = x[:, A_CONV_CH:A_CONV_CH + A_V]
    ab = x[:, A_CONV_CH + A_V:A_SEG]

    xp = jnp.concatenate([tail_ref[...], qkv], axis=0)
    cw = cw_ref[...]
    y = qkv * cw[A_CONV - 1:A_CONV]
    for j in range(A_CONV - 1):
        o = SUBLANES - (A_CONV - 1) + j
        y = y + xp[o:o + rows] * cw[j:j + 1]
    tail_ref[...] = qkv[rows - SUBLANES:rows]

    @pl.when(step == n_steps - 1)
    def _():
        cnew_ref[...] = qkv[rows - (A_CONV - 1):rows]

    act = _silu(y)
    g_all = -jnp.exp(alog_ref[...]) * _softplus(ab + dtb_ref[...])
    beta_all = jax.nn.sigmoid(ab)
    gc_all = _cumsum_rows(g_all, _chunk_tril(rows, c))
    gc_t = gc_all.T
    tril, strict, eye = _tri_masks(c)
    nw = nw_ref[...]

    q_all = act[:, 0:A_QK]
    k_all = act[:, A_QK:2 * A_QK]
    q_all = q_all * lax.rsqrt(_head_sums(q_all * q_all, A_DK) + 1e-6) * (A_DK ** -0.5)
    k_all = k_all * lax.rsqrt(_head_sums(k_all * k_all, A_DK) + 1e-6)
    qn = [q_all[:, h * A_DK:(h + 1) * A_DK] for h in range(A_HEADS)]
    kn = [k_all[:, h * A_DK:(h + 1) * A_DK] for h in range(A_HEADS)]
    vv = [act[:, 2 * A_QK + h * A_DV:2 * A_QK + (h + 1) * A_DV] for h in range(A_HEADS)]

    items = [(ci, h) for ci in range(nch) for h in range(A_HEADS)]
    q_i, k_i, kb_i, vb_i, dec_i, gcc_i, gl_i = [], [], [], [], [], [], []
    for ci, h in items:
        r0 = ci * c
        beta = beta_all[r0:r0 + c, A_HEADS + h:A_HEADS + h + 1]
        gcc = gc_all[r0:r0 + c, h:h + 1]
        gcr = gc_t[h:h + 1, r0:r0 + c]
        k = kn[h][r0:r0 + c]
        q_i.append(qn[h][r0:r0 + c])
        k_i.append(k)
        kb_i.append(k * beta)
        vb_i.append(vv[h][r0:r0 + c] * beta)
        dec_i.append(jnp.exp(jnp.where(tril, gcc - gcr, -1e30)))
        gcc_i.append(gcc)
        gl_i.append(gc_all[r0 + c - 1:r0 + c, h:h + 1])
    a_i = [jnp.where(strict, dot(kb, k, _NT) * dec, 0.0) for kb, k, dec in zip(kb_i, k_i, dec_i)]
    t_i = _neumann_inv_many([-a for a in a_i], eye)
    egc_i = [jnp.exp(g) for g in gcc_i]
    sol_i = [_dot3(t, jnp.concatenate([vb, kb * e], axis=1)) for t, vb, kb, e in zip(t_i, vb_i, kb_i, egc_i)]
    attn_i = [jnp.where(tril, dot(q, k, _NT) * dec, 0.0) for q, k, dec in zip(q_i, k_i, dec_i)]
    kq_i = [jnp.concatenate([sol[:, A_DV:], q * e], axis=0) for sol, q, e in zip(sol_i, q_i, egc_i)]
    kg_i = [k * jnp.exp(gl - g) for k, gl, g in zip(k_i, gl_i, gcc_i)]

    s = [s_ref[h] for h in range(A_HEADS)]
    out_rows = []
    for ci in range(nch):
        ids = [ci * A_HEADS + h for h in range(A_HEADS)]
        ks = [dot(kq_i[i], s[h]) for h, i in enumerate(ids)]
        u = [sol_i[i][:, :A_DV] - ks[h][:c] for h, i in enumerate(ids)]
        o = [ks[h][c:] + dot(attn_i[i], u[h]) for h, i in enumerate(ids)]
        s = [s[h] * jnp.exp(gl_i[i]) + dot(kg_i[i], u[h], _TN) for h, i in enumerate(ids)]
        out_rows.append(jnp.concatenate(o, axis=1))
    for h in range(A_HEADS):
        s_ref[h] = s[h]
    o_all = jnp.concatenate(out_rows, axis=0)
    ms = _head_sums(o_all * o_all, A_DV) * (1.0 / A_DV)
    oa_ref[...] = (o_all * lax.rsqrt(ms + EPS) * nw * _silu(z)).astype(oa_ref.dtype)

    @pl.when(step == n_steps - 1)
    def _():
        snew_ref[...] = s_ref[...]


def _row(v, width=LANES):
    v = v.reshape(1, -1).astype(F32)
    return jnp.pad(v, ((0, 0), (0, width - v.shape[1])))


def _mixer_geometry(row0, seq):
    c = min(CHUNK, seq)
    nch = _pick(seq // c, (MIXER_CHUNKS_PER_STEP, 2, 1))
    rows = c * nch
    assert row0 % rows == 0
    return c, nch, rows, seq // rows, row0 // rows


def _mixer_a(pa, row0, bsz, seq, precise, conv_prev, s0, conv_w, a_log, dt_bias, norm_w):
    c, nch, rows, n_steps, blk0 = _mixer_geometry(row0, seq)
    kern = functools.partial(_mixer_a_kernel, c=c, nch=nch, n_steps=n_steps, precise=precise)
    full = lambda shape: pl.BlockSpec(shape, lambda b, n: (0,) * len(shape))
    return pl.pallas_call(
        kern,
        grid=(bsz, n_steps),
        in_specs=[
            pl.BlockSpec((rows, A_SEG), lambda b, n: (blk0 + b * n_steps + n, 0)),
            pl.BlockSpec((None, A_CONV - 1, A_CONV_CH), lambda b, n: (b, 0, 0)),
            pl.BlockSpec((None, A_HEADS, A_DK, A_DV), lambda b, n: (b, 0, 0, 0)),
            full((A_CONV, A_CONV_CH)),
            full((1, LANES)),
            full((1, LANES)),
            full((1, A_V)),
        ],
        out_specs=[
            pl.BlockSpec((rows, A_V), lambda b, n: (b * n_steps + n, 0)),
            pl.BlockSpec((None, A_CONV - 1, A_CONV_CH), lambda b, n: (b, 0, 0)),
            pl.BlockSpec((None, A_HEADS, A_DK, A_DV), lambda b, n: (b, 0, 0, 0)),
        ],
        out_shape=[
            jax.ShapeDtypeStruct((bsz * seq, A_V), F32 if precise else BF16),
            jax.ShapeDtypeStruct((bsz, A_CONV - 1, A_CONV_CH), F32),
            jax.ShapeDtypeStruct((bsz, A_HEADS, A_DK, A_DV), F32),
        ],
        scratch_shapes=[pltpu.VMEM((SUBLANES, A_CONV_CH), F32), pltpu.VMEM((A_HEADS, A_DK, A_DV), F32)],
        compiler_params=_cparams(("arbitrary", "arbitrary")),
        name="mixer_a",
    )(pa, conv_prev, s0, conv_w, _row(a_log), _row(dt_bias), jnp.tile(norm_w, A_HEADS).reshape(1, A_V))


def _mixer_b_kernel(pb_ref, sprev_ref, s0_ref, mu_ref, w0_ref, w2_ref, a0_ref, a2_ref, g2_ref,
                    kk_ref, ka_ref, rk_ref, lnw_ref, lnb_ref,
                    ob_ref, shnew_ref, snew_ref, last_ref, s_ref, *, c, nch, n_steps, precise):
    dot = _mdot(precise)
    step = pl.program_id(1)
    rows = c * nch

    @pl.when(step == 0)
    def _():
        last_ref[...] = sprev_ref[...]
        s_ref[...] = s0_ref[...]

    p = pb_ref[...]
    rowi = lax.broadcasted_iota(I32, (rows, 1), 0)
    prev = jnp.where(rowi == 0, last_ref[...], pltpu.roll(p, 1, 0))
    last_ref[...] = p[rows - 1:rows]

    @pl.when(step == n_steps - 1)
    def _():
        shnew_ref[...] = p[rows - 1:rows]

    xs = p + (prev - p) * mu_ref[...]
    r_all = xs[:, 0:B_C]
    k_all = xs[:, B_C:2 * B_C]
    v_all = xs[:, 2 * B_C:3 * B_C]
    xwa = xs[:, 3 * B_C:3 * B_C + B_W_LORA + B_A_LORA]
    xg = xs[:, 3 * B_C + B_W_LORA + B_A_LORA:B_WIDTH]

    w_log = -_softplus(-(w0_ref[...] + dot(jnp.tanh(xwa), w2_ref[...]))) - 0.5
    lw = -jnp.exp(w_log)
    rate = jax.nn.sigmoid(a0_ref[...] + dot(xwa, a2_ref[...]))
    gate = dot(jax.nn.sigmoid(xg), g2_ref[...])
    kkr = k_all * kk_ref[...]
    k2_all = k_all * (1.0 + (rate - 1.0) * ka_ref[...])

    cl = _cumsum_rows(lw, _chunk_tril(rows, c))
    e_cl = jnp.exp(cl)
    e_neg = jnp.exp(-cl)
    at_all = jnp.exp(cl - lw)
    bt_all = rate * e_neg
    kt_all = k2_all * e_neg
    rt_all = r_all * e_cl
    tril, strict, eye = _tri_masks(c)
    rk = rk_ref[...]
    lnw = lnw_ref[...]
    lnb = lnb_ref[...]
    kkr_h = [kkr[:, h * B_N:(h + 1) * B_N] for h in range(B_HEADS)]
    kkn = [x * lax.rsqrt(jnp.sum(x * x, axis=-1, keepdims=True) + 1e-6) for x in kkr_h]

    items = [(ci, h) for ci in range(nch) for h in range(B_HEADS)]
    at_i, bt_i, kt_i, rt_i, v_i = [], [], [], [], []
    for ci, h in items:
        rs = slice(ci * c, (ci + 1) * c)
        ls = slice(h * B_N, (h + 1) * B_N)
        kk = kkn[h][rs]
        at_i.append(-kk * at_all[rs, ls])
        bt_i.append(kk * bt_all[rs, ls])
        kt_i.append(kt_all[rs, ls])
        rt_i.append(rt_all[rs, ls])
        v_i.append(v_all[rs, ls])
    bk_i = [jnp.concatenate([b, k], axis=0) for b, k in zip(bt_i, kt_i)]
    ar_i = [jnp.concatenate([a, r], axis=0) for a, r in zip(at_i, rt_i)]
    row2 = lax.broadcasted_iota(I32, (c, 2 * c), 0)
    col2 = jnp.bitwise_and(lax.broadcasted_iota(I32, (c, 2 * c), 1), c - 1)
    strict2 = row2 > col2
    tril2 = row2 >= col2
    g_a = [jnp.where(strict2, _dot3(a, bk, _NT), 0.0) for a, bk in zip(at_i, bk_i)]
    g_r = [jnp.where(tril2, dot(r, bk, _NT), 0.0) for r, bk in zip(rt_i, bk_i)]
    t_i = _neumann_inv_many([g[:, :c] for g in g_a], eye)
    mv_i = [dot(g[:, c:], v) for g, v in zip(g_a, v_i)]

    s = [s_ref[h] for h in range(B_HEADS)]
    out_rows = []
    for ci in range(nch):
        ids = [ci * B_HEADS + h for h in range(B_HEADS)]
        rs = slice(ci * c, (ci + 1) * c)
        ars = [dot(ar_i[i], s[h], _NT) for h, i in enumerate(ids)]
        u = [_dot3(t_i[i], ars[h][:c] + mv_i[i]) for h, i in enumerate(ids)]
        uv = [jnp.concatenate([u[h], v_i[i]], axis=0) for h, i in enumerate(ids)]
        o = [ars[h][c:] + dot(g_r[i], uv[h]) for h, i in enumerate(ids)]
        e_last = e_cl[(ci + 1) * c - 1:(ci + 1) * c]
        s = [(s[h] + dot(uv[h], bk_i[i], _TN)) * e_last[:, h * B_N:(h + 1) * B_N] for h, i in enumerate(ids)]
        outs = []
        for h in range(B_HEADS):
            ls = slice(h * B_N, (h + 1) * B_N)
            mean = jnp.mean(o[h], axis=-1, keepdims=True)
            var = jnp.mean(jnp.square(o[h] - mean), axis=-1, keepdims=True)
            gn = (o[h] - mean) * lax.rsqrt(var + B_GN_EPS) * lnw[:, ls] + lnb[:, ls]
            bonus = jnp.sum(r_all[rs, ls] * k2_all[rs, ls] * rk[:, ls], axis=-1, keepdims=True) * v_all[rs, ls]
            outs.append((gn + bonus) * gate[rs, ls])
        out_rows.append(jnp.concatenate(outs, axis=1))
    for h in range(B_HEADS):
        s_ref[h] = s[h]
    ob_ref[...] = jnp.concatenate(out_rows, axis=0).astype(ob_ref.dtype)

    @pl.when(step == n_steps - 1)
    def _():
        snew_ref[...] = s_ref[...]


def _mixer_b(pb, row0, bsz, seq, precise, shift_prev, s0, mu, w0, w2, a0, a2, g2, k_k, k_a, r_k, ln_w, ln_b):
    c, nch, rows, n_steps, blk0 = _mixer_geometry(row0, seq)
    kern = functools.partial(_mixer_b_kernel, c=c, nch=nch, n_steps=n_steps, precise=precise)
    full = lambda shape: pl.BlockSpec(shape, lambda b, n: (0,) * len(shape))
    lora = B_W_LORA + B_A_LORA
    w2p = jnp.zeros((lora, B_C), F32).at[:B_W_LORA].set(w2)
    a2p = jnp.zeros((lora, B_C), F32).at[B_W_LORA:].set(a2)
    vec = lambda v: v.reshape(1, -1).astype(F32)
    return pl.pallas_call(
        kern,
        grid=(bsz, n_steps),
        in_specs=[
            pl.BlockSpec((rows, B_WIDTH), lambda b, n: (blk0 + b * n_steps + n, 0)),
            pl.BlockSpec((None, 1, B_WIDTH), lambda b, n: (b, 0, 0)),
            pl.BlockSpec((None, B_HEADS, B_N, B_N), lambda b, n: (b, 0, 0, 0)),
            full((1, B_WIDTH)),
            full((1, B_C)), full((lora, B_C)),
            full((1, B_C)), full((lora, B_C)),
            full((B_G_LORA, B_C)),
            full((1, B_C)), full((1, B_C)), full((1, B_C)), full((1, B_C)), full((1, B_C)),
        ],
        out_specs=[
            pl.BlockSpec((rows, B_C), lambda b, n: (b * n_steps + n, 0)),
            pl.BlockSpec((None, 1, B_WIDTH), lambda b, n: (b, 0, 0)),
            pl.BlockSpec((None, B_HEADS, B_N, B_N), lambda b, n: (b, 0, 0, 0)),
        ],
        out_shape=[
            jax.ShapeDtypeStruct((bsz * seq, B_C), F32 if precise else BF16),
            jax.ShapeDtypeStruct((bsz, 1, B_WIDTH), F32),
            jax.ShapeDtypeStruct((bsz, B_HEADS, B_N, B_N), F32),
        ],
        scratch_shapes=[pltpu.VMEM((1, B_WIDTH), F32), pltpu.VMEM((B_HEADS, B_N, B_N), F32)],
        compiler_params=_cparams(("arbitrary", "arbitrary")),
        name="mixer_b",
    )(pb, shift_prev, s0, vec(mu), vec(w0), w2p, vec(a0), a2p, g2,
      vec(k_k), vec(k_a), vec(r_k), vec(ln_w), vec(ln_b))


def _kv_expand(ckvn, krope, wuk_ref, wuv_ref, gk_ref, k_ref, v_ref, dot):
    k0 = dot(ckvn, wuk_ref[...])
    gk = gk_ref[...]
    for h in range(C_HEADS):
        kh = k0[:, h * C_HPAD:(h + 1) * C_HPAD] + krope
        ss = jnp.sum(kh * kh, axis=-1, keepdims=True) * (1.0 / C_QK)
        k_ref[:, h * C_HPAD:(h + 1) * C_HPAD] = (kh * lax.rsqrt(ss + EPS) * gk).astype(k_ref.dtype)
    v_ref[...] = dot(ckvn, wuv_ref[...]).astype(v_ref.dtype)


def _c_prep_kernel(pc_ref, cos_ref, sin_ref, qn_ref, kvn_ref, wq_ref, wqr_ref, wuk_ref, wuv_ref, gq_ref, gk_ref,
                   q_ref, k_ref, v_ref, ckv_ref, kr_ref, *, precise):
    dot = _mdot(precise)
    pc = pc_ref[...]
    cq = pc[:, 0:C_Q_LORA]
    ckv_raw = pc[:, C_Q_LORA:C_Q_LORA + C_KV_LORA]
    krp = pc[:, C_Q_LORA + C_KV_LORA:C_Q_LORA + C_KV_LORA + LANES]
    krr = pc[:, C_Q_LORA + C_KV_LORA + LANES:C_SEG]
    cos = cos_ref[...]
    sin = sin_ref[...]
    lane = lax.broadcasted_iota(I32, cos.shape, 1)
    cosq = jnp.where(lane < C_NOPE, 1.0, cos)

    cqn = _rms(cq) * qn_ref[...]
    q0 = dot(cqn, wq_ref[...])
    q1 = dot(cqn, wqr_ref[...])
    gq = gq_ref[...] * (C_QK ** -0.5)
    for h in range(C_HEADS):
        sl = slice(h * C_HPAD, (h + 1) * C_HPAD)
        qh = q0[:, sl] * cosq + q1[:, sl] * sin
        ss = jnp.sum(qh * qh, axis=-1, keepdims=True) * (1.0 / C_QK)
        q_ref[:, sl] = (qh * lax.rsqrt(ss + EPS) * gq).astype(q_ref.dtype)

    ckvn = _rms(ckv_raw) * kvn_ref[...]
    ckv_ref[...] = ckvn
    krope = krp * cos + krr * sin
    kr_ref[...] = krope[:, C_NOPE:C_NOPE + C_ROPE]
    _kv_expand(ckvn, krope, wuk_ref, wuv_ref, gk_ref, k_ref, v_ref, dot)


def _kv_expand_kernel(ckv_ref, kr_ref, wuk_ref, wuv_ref, gk_ref, k_ref, v_ref, *, precise):
    _kv_expand(ckv_ref[...], kr_ref[...], wuk_ref, wuv_ref, gk_ref, k_ref, v_ref, _mdot(precise))


def _c_weights(q_norm, kv_norm, w_uq, w_ukv, q_gain, k_gain):
    half = C_ROPE // 2
    wq = jnp.zeros((C_Q_LORA, C_HEADS * C_HPAD), F32)
    wqr = jnp.zeros((C_Q_LORA, C_HEADS * C_HPAD), F32)
    wuk = jnp.zeros((C_KV_LORA, C_HEADS * C_HPAD), F32)
    wuv = jnp.zeros((C_KV_LORA, C_HEADS * C_VDIM), F32)
    for h in range(C_HEADS):
        wh = w_uq[:, h * C_QK:(h + 1) * C_QK]
        wq = wq.at[:, h * C_HPAD:h * C_HPAD + C_QK].set(wh)
        rot = jnp.concatenate([-wh[:, C_NOPE + half:], wh[:, C_NOPE:C_NOPE + half]], axis=1)
        wqr = wqr.at[:, h * C_HPAD + C_NOPE:h * C_HPAD + C_QK].set(rot)
        kvh = w_ukv[:, h * (C_NOPE + C_VDIM):(h + 1) * (C_NOPE + C_VDIM)]
        wuk = wuk.at[:, h * C_HPAD:h * C_HPAD + C_NOPE].set(kvh[:, :C_NOPE])
        wuv = wuv.at[:, h * C_VDIM:(h + 1) * C_VDIM].set(kvh[:, C_NOPE:])
    gain = lambda g: _row(jnp.concatenate([g[:C_NOPE], g[C_NOPE:], g[C_NOPE:]]))
    return (q_norm.reshape(1, -1), kv_norm.reshape(1, -1), wq, wqr, wuk, wuv, gain(q_gain), gain(k_gain))


def _rope_tables(pos):
    half = C_ROPE // 2
    inv = ROPE_THETA ** (-(jnp.arange(0, C_ROPE, 2, dtype=F32) / C_ROPE))
    ang = pos.astype(F32)[:, None] * inv[None, :]
    cos, sin = jnp.cos(ang), jnp.sin(ang)
    pad = lambda t: jnp.pad(jnp.concatenate([t, t], axis=1), ((0, 0), (C_NOPE, LANES - C_NOPE - 2 * half)))
    return pad(cos), pad(sin)


def _c_prep(pc, cos, sin, cw, precise):
    t = pc.shape[0]
    tm = _pick(t, (512, 256, 128, 64, 32, 16, 8))
    qn, kvn, wq, wqr, wuk, wuv, gq, gk = cw
    full = lambda a: pl.BlockSpec(a.shape, lambda i: (0,) * a.ndim)
    rows = lambda w: pl.BlockSpec((tm, w), lambda i: (i, 0))
    act = F32 if precise else BF16
    return pl.pallas_call(
        functools.partial(_c_prep_kernel, precise=precise),
        grid=(t // tm,),
        in_specs=[rows(C_SEG), rows(LANES), rows(LANES)] + [full(a) for a in (qn, kvn, wq, wqr, wuk, wuv, gq, gk)],
        out_specs=[rows(C_HEADS * C_HPAD), rows(C_HEADS * C_HPAD), rows(C_HEADS * C_VDIM), rows(C_KV_LORA),
                   rows(C_ROPE)],
        out_shape=[
            jax.ShapeDtypeStruct((t, C_HEADS * C_HPAD), act),
            jax.ShapeDtypeStruct((t, C_HEADS * C_HPAD), act),
            jax.ShapeDtypeStruct((t, C_HEADS * C_VDIM), act),
            jax.ShapeDtypeStruct((t, C_KV_LORA), F32),
            jax.ShapeDtypeStruct((t, C_ROPE), F32),
        ],
        compiler_params=_cparams(("arbitrary",)),
        name="c_prep",
    )(pc, cos, sin, qn, kvn, wq, wqr, wuk, wuv, gq, gk)


def _kv_expand_call(ckv, krp, cw, precise):
    t = ckv.shape[0]
    tm = _pick(t, (1024, 512, 256, 128, 64, 32, 16, 8))
    _, _, _, _, wuk, wuv, _, gk = cw
    full = lambda a: pl.BlockSpec(a.shape, lambda i: (0,) * a.ndim)
    rows = lambda w: pl.BlockSpec((tm, w), lambda i: (i, 0))
    act = F32 if precise else BF16
    return pl.pallas_call(
        functools.partial(_kv_expand_kernel, precise=precise),
        grid=(t // tm,),
        in_specs=[rows(C_KV_LORA), rows(LANES), full(wuk), full(wuv), full(gk)],
        out_specs=[rows(C_HEADS * C_HPAD), rows(C_HEADS * C_VDIM)],
        out_shape=[
            jax.ShapeDtypeStruct((t, C_HEADS * C_HPAD), act),
            jax.ShapeDtypeStruct((t, C_HEADS * C_VDIM), act),
        ],
        compiler_params=_cparams(("arbitrary",)),
        name="kv_expand",
    )(ckv, krp, wuk, wuv, gk)


def _attn_kernel(*refs, tq, nq, past, has_cache, precise):
    if has_cache:
        q_ref, kn_ref, vn_ref, kc_ref, vc_ref, o_ref = refs
    else:
        q_ref, kn_ref, vn_ref, o_ref = refs
    dot = _mdot(precise)
    qi = pl.program_id(1)
    shift = CHUNK.bit_length() - 1

    q = q_ref[...]
    q_chunk = jnp.right_shift(past + qi * tq + lax.broadcasted_iota(I32, (tq, 1), 0), shift)
    qh = [q[:, h * C_HPAD:(h + 1) * C_HPAD] for h in range(C_HEADS)]
    init = tuple((jnp.full((tq, 1), -1e30, F32), jnp.zeros((tq, 1), F32), jnp.zeros((tq, C_VDIM), F32))
                 for _ in range(C_HEADS))

    def step(state, kb, vb, p0, w):
        vis = jnp.right_shift(p0 + lax.broadcasted_iota(I32, (1, w), 1), shift) <= q_chunk
        new = []
        for h in range(C_HEADS):
            m, den, acc = state[h]
            s = jnp.where(vis, dot(qh[h], kb[:, h * C_HPAD:(h + 1) * C_HPAD], _NT), -1e30)
            m_new = jnp.maximum(m, jnp.max(s, axis=-1, keepdims=True))
            scale = jnp.exp(m - m_new)
            p = jnp.exp(s - m_new)
            new.append((m_new, den * scale + jnp.sum(p, axis=-1, keepdims=True),
                        acc * scale + dot(p, vb[:, h * C_VDIM:(h + 1) * C_VDIM])))
        return tuple(new)

    def finish(state):
        o_ref[...] = jnp.concatenate([acc / den for _, den, acc in state], axis=1).astype(o_ref.dtype)

    if precise:
        def new_step(j, st):
            r0 = pl.multiple_of(j * tq, tq)
            return step(st, kn_ref[pl.ds(r0, tq), :], vn_ref[pl.ds(r0, tq), :], past + r0, tq)

        state = lax.fori_loop(0, qi + 1, new_step, init)
        if has_cache:
            tc = _pick(past, (ATTN_KEY_BLOCK, 256, 128, 64, 32, 16, 8))

            def cache_step(j, st):
                r0 = pl.multiple_of(j * tc, tc)
                return step(st, kc_ref[pl.ds(r0, tc), :], vc_ref[pl.ds(r0, tc), :], r0, tc)

            state = lax.fori_loop(0, past // tc, cache_step, state)
        finish(state)
    else:
        def attend(n_new):
            blocks = [(kn_ref, vn_ref, r0, min(ATTN_KEY_BLOCK, n_new - r0), past + r0)
                      for r0 in range(0, n_new, ATTN_KEY_BLOCK)]
            if has_cache:
                blocks += [(kc_ref, vc_ref, r0, min(ATTN_KEY_BLOCK, past - r0), r0)
                           for r0 in range(0, past, ATTN_KEY_BLOCK)]
            state = init
            for k_ref, v_ref, r0, w, p0 in blocks:
                state = step(state, k_ref[r0:r0 + w, :], v_ref[r0:r0 + w, :], p0, w)
            finish(state)

        if nq == 1:
            attend(tq)
        else:
            for blk in range(nq):
                pl.when(qi == blk)(functools.partial(attend, (blk + 1) * tq))


def _attention(q, k, v, row0, bsz, seq, precise, k_cache=None, v_cache=None):
    has_cache = k_cache is not None
    past = k_cache.shape[0] // bsz if has_cache else 0
    tq = _pick(seq, (256, 128, 64, 32, 16, 8))
    assert tq % CHUNK == 0 or tq == seq
    nq = seq // tq
    kern = functools.partial(_attn_kernel, tq=tq, nq=nq, past=past, has_cache=has_cache, precise=precise)
    qw, vw = C_HEADS * C_HPAD, C_HEADS * C_VDIM
    in_specs = [
        pl.BlockSpec((tq, qw), lambda b, i: (row0 // tq + b * nq + i, 0)),
        pl.BlockSpec((seq, qw), lambda b, i: (row0 // seq + b, 0)),
        pl.BlockSpec((seq, vw), lambda b, i: (row0 // seq + b, 0)),
    ]
    args = [q, k, v]
    if has_cache:
        in_specs += [pl.BlockSpec((past, qw), lambda b, i: (b, 0)), pl.BlockSpec((past, vw), lambda b, i: (b, 0))]
        args += [k_cache, v_cache]
    return pl.pallas_call(
        kern,
        grid=(bsz, nq),
        in_specs=in_specs,
        out_specs=pl.BlockSpec((tq, vw), lambda b, i: (b * nq + i, 0)),
        out_shape=jax.ShapeDtypeStruct((bsz * seq, vw), F32 if precise else BF16),
        compiler_params=_cparams(("arbitrary", "arbitrary")),
        name="attention",
    )(*args)


def _out_proj_kernel(*refs, n_prompt_tiles, precise, split_x):
    dot = _mdot(precise)
    is_p = pl.program_id(0) < n_prompt_tiles
    if split_x:
        y = jnp.where(is_p, refs[0][...], refs[1][...])
        refs = refs[2:]
    else:
        y = refs[0][...]
        refs = refs[1:]
    (oap_ref, obp_ref, ocp_ref, oas_ref, obs_ref, ocs_ref, w_ref, g_ref, wr_ref, br_ref,
     y_ref, xt_ref, ri_ref, rw_ref) = refs
    oa = jnp.where(is_p, oap_ref[...], oas_ref[...])
    ob = jnp.where(is_p, obp_ref[...], obs_ref[...])
    oc = jnp.where(is_p, ocp_ref[...], ocs_ref[...])
    y = y + dot(oa, w_ref[0:A_V])
    y = y + dot(ob, w_ref[A_V:A_V + B_C])
    y = y + dot(oc, w_ref[A_V + B_C:D_MIX])
    y_ref[...] = y
    xn = _rms(y) * g_ref[...]
    _rows_to_tiles(xt_ref, xn)

    logits = _dot3(xn, wr_ref[...]) + br_ref[...]
    lane = lax.broadcasted_iota(I32, logits.shape, 1)
    lanef = lane.astype(F32)
    neg = -1e30
    lg = jnp.where(lane < N_GROUPS, logits, neg)
    mg = jnp.max(lg, axis=-1, keepdims=True)
    pg = 1.0 / jnp.sum(jnp.exp(lg - mg), axis=-1, keepdims=True)
    gidx = jnp.min(jnp.where(lg == mg, lanef, float(LANES)), axis=-1, keepdims=True)
    lo = N_GROUPS + EXPERTS_PER_GROUP * gidx
    in_grp = (lanef >= lo) & (lanef < lo + EXPERTS_PER_GROUP)
    el = jnp.where(in_grp, logits, neg)
    m1 = jnp.max(el, axis=-1, keepdims=True)
    i1 = jnp.min(jnp.where(el == m1, lanef, float(LANES)), axis=-1, keepdims=True)
    el2 = jnp.where(lanef == i1, neg, el)
    m2 = jnp.max(el2, axis=-1, keepdims=True)
    i2 = jnp.min(jnp.where(el2 == m2, lanef, float(LANES)), axis=-1, keepdims=True)
    den = jnp.sum(jnp.exp(el - m1), axis=-1, keepdims=True)
    p1 = 1.0 / den
    p2 = jnp.exp(m2 - m1) / den
    w1 = pg * p1 / (p1 + p2)
    w2 = pg * p2 / (p1 + p2)
    e1 = (i1 - N_GROUPS).astype(I32)
    e2 = (i2 - N_GROUPS).astype(I32)
    ri_ref[...] = jnp.where(lane == 0, e1, jnp.where(lane == 1, e2, 0))
    rw_ref[...] = jnp.where(lane == 0, w1, jnp.where(lane == 1, w2, 0.0))


def _out_proj(xs, mix_p, mix_s, w_out, g, w_router, b_router, precise):
    tp, ts = mix_p[0].shape[0], mix_s[0].shape[0]
    t = tp + ts
    tm = _pick(ts, (512, 256, 128, 64, 32, 16, 8))
    assert tp % tm == 0 and sum(x.shape[0] for x in xs) == t
    npt = tp // tm
    kern = functools.partial(_out_proj_kernel, n_prompt_tiles=npt, precise=precise, split_x=len(xs) == 2)
    pspec = lambda w: pl.BlockSpec((tm, w), lambda i: (jnp.minimum(i, npt - 1), 0))
    sspec = lambda w: pl.BlockSpec((tm, w), lambda i: (jnp.maximum(i - npt, 0), 0))
    full = lambda a: pl.BlockSpec(a.shape, lambda i: (0,) * a.ndim)
    rows = lambda w: pl.BlockSpec((tm, w), lambda i: (i, 0))
    cw = C_HEADS * C_VDIM
    g2 = g.reshape(1, D_MODEL)
    x_specs = [pspec(D_MODEL), sspec(D_MODEL)] if len(xs) == 2 else [rows(D_MODEL)]
    return pl.pallas_call(
        kern,
        grid=(t // tm,),
        in_specs=x_specs + [pspec(A_V), pspec(B_C), pspec(cw), sspec(A_V), sspec(B_C), sspec(cw),
                            full(w_out), full(g2), full(w_router), full(b_router)],
        out_specs=[rows(D_MODEL), pl.BlockSpec((tm * SUBLANES, LANES), lambda i: (i, 0)), rows(LANES), rows(LANES)],
        out_shape=[
            jax.ShapeDtypeStruct((t, D_MODEL), F32),
            jax.ShapeDtypeStruct((t * SUBLANES, LANES), F32),
            jax.ShapeDtypeStruct((t, LANES), I32),
            jax.ShapeDtypeStruct((t, LANES), F32),
        ],
        compiler_params=_cparams(("arbitrary",)),
        name="out_proj_router",
    )(*xs, *mix_p, *mix_s, w_out, g2, w_router, b_router)


def _rank_kernel(ri_ref, rank_ref, cnt_ref, base_ref):
    @pl.when(pl.program_id(0) == 0)
    def _():
        base_ref[...] = jnp.zeros_like(base_ref)

    ri = ri_ref[...]
    tm = ri.shape[0]
    lane = lax.broadcasted_iota(I32, ri.shape, 1)
    oh0 = lane == ri[:, 0:1]
    oh1 = lane == ri[:, 1:2]
    cnt = oh0.astype(F32) + oh1.astype(F32)
    row = lax.broadcasted_iota(I32, (tm, tm), 0)
    col = lax.broadcasted_iota(I32, (tm, tm), 1)
    before = jnp.dot((row > col).astype(BF16), cnt.astype(BF16), preferred_element_type=F32) + base_ref[...]
    r0 = jnp.sum(jnp.where(oh0, before, 0.0), axis=-1, keepdims=True)
    r1 = jnp.sum(jnp.where(oh1, before, 0.0), axis=-1, keepdims=True)
    rank_ref[...] = jnp.where(lane == 0, r0, jnp.where(lane == 1, r1, 0.0)).astype(I32)
    total = base_ref[...] + jnp.sum(cnt, axis=0, keepdims=True)
    base_ref[...] = total
    cnt_ref[...] = total.astype(I32)


def _rank(ri):
    t = ri.shape[0]
    tm = _pick(t, (256, 128, 64, 32, 16, 8))
    return pl.pallas_call(
        _rank_kernel,
        grid=(t // tm,),
        in_specs=[pl.BlockSpec((tm, LANES), lambda i: (i, 0))],
        out_specs=[pl.BlockSpec((tm, LANES), lambda i: (i, 0)), pl.BlockSpec((1, LANES), lambda i: (0, 0))],
        out_shape=[jax.ShapeDtypeStruct((t, LANES), I32), jax.ShapeDtypeStruct((1, LANES), I32)],
        scratch_shapes=[pltpu.VMEM((1, LANES), F32)],
        compiler_params=_cparams(("arbitrary",)),
        name="moe_rank",
    )(ri)


def _gather_tiles(idx_ref, n, src_hbm, buf, slot, sem):
    def body(j, carry):
        for queue in range(2):
            r = 2 * j + queue
            src = pl.multiple_of(idx_ref[0, r] * SUBLANES, SUBLANES)
            dst = pl.multiple_of((slot * n + r) * SUBLANES, SUBLANES)
            pltpu.make_async_copy(src_hbm.at[pl.ds(src, SUBLANES)], buf.at[pl.ds(dst, SUBLANES)],
                                  sem.at[slot]).start(priority=queue)
        return carry

    lax.fori_loop(0, n // 2, body, 0, unroll=4)


def _gather_wait(n, src_hbm, buf, slot, sem):
    base = pl.multiple_of(slot * n * SUBLANES, SUBLANES)
    pltpu.make_async_copy(src_hbm.at[pl.ds(0, n * SUBLANES)], buf.at[pl.ds(base, n * SUBLANES)], sem.at[slot]).wait()
    return base


def _expert_kernel(te_ref, nu_ref, idc_ref, idn_ref, idn2_ref, xt_hbm, wg_ref, wu_ref, wd_ref, o_ref,
                   xbuf, wgb, wub, wdb, sem):
    i = pl.program_id(0)
    n_used = nu_ref[0]
    slot = lax.rem(i, EXPERT_SLOTS)

    @pl.when((i == 0) & (n_used > 0))
    def _():
        _gather_tiles(idc_ref, MOE_TILE, xt_hbm, xbuf, 0, sem)

    @pl.when((i == 0) & (n_used > 1))
    def _():
        _gather_tiles(idn_ref, MOE_TILE, xt_hbm, xbuf, 1, sem)

    @pl.when(i + 2 < n_used)
    def _():
        _gather_tiles(idn2_ref, MOE_TILE, xt_hbm, xbuf, lax.rem(i + 2, EXPERT_SLOTS), sem)

    @pl.when((i == 0) | (te_ref[i] != te_ref[jnp.maximum(i - 1, 0)]))
    def _():
        wgb[...] = wg_ref[...].astype(BF16)
        wub[...] = wu_ref[...].astype(BF16)
        wdb[...] = wd_ref[...].astype(BF16)

    @pl.when(i < n_used)
    def _():
        base = _gather_wait(MOE_TILE, xt_hbm, xbuf, slot, sem)
        x = _tiles_to_rows(xbuf, base, MOE_TILE).astype(BF16)
        hg = jnp.dot(x, wgb[...], preferred_element_type=F32)
        hu = jnp.dot(x, wub[...], preferred_element_type=F32)
        hidden = (_silu(hg) * hu).astype(BF16)
        _rows_to_tiles(o_ref, jnp.dot(hidden, wdb[...], preferred_element_type=F32))

    @pl.when(i >= n_used)
    def _():
        o_ref[...] = jnp.zeros_like(o_ref)


def _expert_ffn(xt, token_of_pos, tile_expert, n_used, e_gate, e_up, e_down):
    nt = token_of_pos.shape[0] // MOE_TILE
    idx = token_of_pos.reshape(nt, 1, MOE_TILE)
    tile_rows = MOE_TILE * SUBLANES
    smem = lambda imap: pl.BlockSpec((None, 1, MOE_TILE), imap, memory_space=pltpu.SMEM)
    grid_spec = pltpu.PrefetchScalarGridSpec(
        num_scalar_prefetch=2,
        grid=(nt,),
        in_specs=[
            smem(lambda i, te, nu: (i, 0, 0)),
            smem(lambda i, te, nu: (jnp.minimum(i + 1, nt - 1), 0, 0)),
            smem(lambda i, te, nu: (jnp.minimum(i + 2, nt - 1), 0, 0)),
            pl.BlockSpec(memory_space=pl.ANY),
            pl.BlockSpec((None, D_MODEL, D_EXPERT), lambda i, te, nu: (te[i], 0, 0)),
            pl.BlockSpec((None, D_MODEL, D_EXPERT), lambda i, te, nu: (te[i], 0, 0)),
            pl.BlockSpec((None, D_EXPERT, D_MODEL), lambda i, te, nu: (te[i], 0, 0)),
        ],
        out_specs=pl.BlockSpec((tile_rows, LANES), lambda i, te, nu: (i, 0)),
        scratch_shapes=[pltpu.VMEM((EXPERT_SLOTS * tile_rows, LANES), F32),
                        pltpu.VMEM((D_MODEL, D_EXPERT), BF16), pltpu.VMEM((D_MODEL, D_EXPERT), BF16),
                        pltpu.VMEM((D_EXPERT, D_MODEL), BF16), pltpu.SemaphoreType.DMA((EXPERT_SLOTS,))],
    )
    return pl.pallas_call(
        _expert_kernel,
        grid_spec=grid_spec,
        out_shape=jax.ShapeDtypeStruct((nt * tile_rows, LANES), F32),
        compiler_params=_cparams(("arbitrary",)),
        name="expert_ffn",
    )(tile_expert, n_used, idx, idx, idx, xt, e_gate, e_up, e_down)


def _combine_kernel(pc_ref, pn_ref, x_ref, rw_ref, ot_hbm, *refs, tm, n_first_tiles):
    i = pl.program_id(0)
    nt = pl.num_programs(0)
    slot = lax.rem(i, 2)
    n = TOP_K * tm
    obuf, sem = refs[-2:]

    @pl.when(i == 0)
    def _():
        _gather_tiles(pc_ref, n, ot_hbm, obuf, 0, sem)

    @pl.when(i + 1 < nt)
    def _():
        _gather_tiles(pn_ref, n, ot_hbm, obuf, 1 - slot, sem)

    base = _gather_wait(n, ot_hbm, obuf, slot, sem)
    rw = rw_ref[...]
    o0 = _tiles_to_rows(obuf, base, tm)
    o1 = _tiles_to_rows(obuf, base + tm * SUBLANES, tm)
    y = x_ref[...] + rw[:, 0:1] * o0 + rw[:, 1:2] * o1
    if n_first_tiles is None:
        refs[0][...] = y
    else:
        @pl.when(i < n_first_tiles)
        def _():
            refs[0][...] = y

        @pl.when(i >= n_first_tiles)
        def _():
            refs[1][...] = y


def _combine(x, out_tiles, pos, rw, split_rows=None):
    t = x.shape[0]
    rows = lambda w: pl.BlockSpec((tm, w), lambda i: (i, 0))
    if split_rows is None:
        tm = _pick(t, (256, 128, 64, 32, 16, 8))
        nft = None
        out_specs = rows(D_MODEL)
        out_shape = jax.ShapeDtypeStruct((t, D_MODEL), F32)
    else:
        tm = _pick(t - split_rows, (256, 128, 64, 32, 16, 8))
        assert split_rows % tm == 0
        nft = split_rows // tm
        out_specs = [pl.BlockSpec((tm, D_MODEL), lambda i: (jnp.minimum(i, nft - 1), 0)),
                     pl.BlockSpec((tm, D_MODEL), lambda i: (jnp.maximum(i - nft, 0), 0))]
        out_shape = [jax.ShapeDtypeStruct((split_rows, D_MODEL), F32),
                     jax.ShapeDtypeStruct((t - split_rows, D_MODEL), F32)]
    nt = t // tm
    n = TOP_K * tm
    idx = pos.reshape(nt, tm, TOP_K).transpose(0, 2, 1).reshape(nt, 1, n)
    kern = functools.partial(_combine_kernel, tm=tm, n_first_tiles=nft)
    smem = lambda imap: pl.BlockSpec((None, 1, n), imap, memory_space=pltpu.SMEM)
    return pl.pallas_call(
        kern,
        grid=(nt,),
        in_specs=[smem(lambda i: (i, 0, 0)), smem(lambda i: (jnp.minimum(i + 1, nt - 1), 0, 0)),
                  rows(D_MODEL), rows(LANES), pl.BlockSpec(memory_space=pl.ANY)],
        out_specs=out_specs,
        out_shape=out_shape,
        scratch_shapes=[pltpu.VMEM((2 * n * SUBLANES, LANES), F32), pltpu.SemaphoreType.DMA((2,))],
        compiler_params=_cparams(("arbitrary",)),
        name="moe_combine",
    )(idx, idx, x, rw, out_tiles)


def _route_tables(eid, rank, counts):
    t = eid.shape[0]
    padded = ((counts + MOE_TILE - 1) // MOE_TILE) * MOE_TILE
    ends = jnp.cumsum(padded)
    starts = ends - padded
    experts = jnp.arange(N_EXPERTS, dtype=I32)
    pos = jnp.sum(jnp.where(eid[:, :, None] == experts, starts, 0), axis=-1) + rank
    np_rows = -(-(TOP_K * t + N_EXPERTS * MOE_TILE) // MOE_TILE) * MOE_TILE
    token_of_pos = jnp.zeros((np_rows,), I32).at[pos.reshape(-1)].set(jnp.arange(TOP_K * t, dtype=I32) // TOP_K)
    tile_start = jnp.arange(np_rows // MOE_TILE, dtype=I32) * MOE_TILE
    tile_expert = jnp.minimum(jnp.sum(tile_start[:, None] >= ends[None, :], axis=1), N_EXPERTS - 1).astype(I32)
    n_used = (ends[-1] // MOE_TILE).astype(I32).reshape(1)
    return pos.astype(I32), token_of_pos, tile_expert, n_used


def _moe(y, xt, ri, rw, e_gate, e_up, e_down, split_rows):
    rank, counts = _rank(ri)
    pos, token_of_pos, tile_expert, n_used = _route_tables(ri[:, :TOP_K], rank[:, :TOP_K], counts[0, :N_EXPERTS])
    out_tiles = _expert_ffn(xt, token_of_pos, tile_expert, n_used, e_gate, e_up, e_down)
    return _combine(y, out_tiles, pos, rw, split_rows)


def _in_weights(w_in):
    half = C_ROPE // 2
    zeros = lambda n: jnp.zeros((D_MODEL, n), F32)
    c0 = A_WIDTH + B_WIDTH
    kr = w_in[:, c0 + C_Q_LORA + C_KV_LORA:c0 + C_WIDTH]
    kr_rot = jnp.concatenate([-kr[:, half:], kr[:, :half]], axis=1)
    tail = LANES - C_NOPE - C_ROPE
    cols = [
        w_in[:, :A_CONV_CH + A_V], w_in[:, A_CONV_CH + A_V:A_WIDTH], zeros(LANES - 2 * A_HEADS),
        w_in[:, A_WIDTH:c0],
        w_in[:, c0:c0 + C_Q_LORA + C_KV_LORA],
        zeros(C_NOPE), kr, zeros(tail),
        zeros(C_NOPE), kr_rot, zeros(tail),
    ]
    return jnp.concatenate(cols, axis=1)


def _router_weights(rg, rgb, re, reb):
    w = jnp.zeros((D_MODEL, LANES), F32).at[:, :N_GROUPS].set(rg).at[:, N_GROUPS:N_GROUPS + N_EXPERTS].set(re)
    b = jnp.zeros((1, LANES), F32).at[0, :N_GROUPS].set(rgb).at[0, N_GROUPS:N_GROUPS + N_EXPERTS].set(reb)
    return w, b


def _layer(xs, geom, st_p, st_s, cos, sin, wts, precise, split_out):
    (norm_mix, w_in, a_conv_w, a_A_log, a_dt_bias, a_norm_w,
     b_mu, b_w0, b_w2, b_a0, b_a2, b_g2, b_k_k, b_k_a, b_r_k, b_ln_w, b_ln_b,
     c_q_norm, c_kv_norm, c_w_uq, c_w_ukv, c_q_gain, c_k_gain,
     w_out, norm_ffn, router_group, router_group_bias, router_expert, router_expert_bias,
     e_gate, e_up, e_down) = wts
    bp, lp, bs, ls, past = geom
    tp = bp * lp

    pa, pb, pc = _in_proj(xs, norm_mix, _in_weights(w_in), precise)
    cw = _c_weights(c_q_norm, c_kv_norm, c_w_uq, c_w_ukv, c_q_gain, c_k_gain)
    q, k, v, ckv, krope = _c_prep(pc, cos, sin, cw, precise)

    a_args = (a_conv_w, a_A_log, a_dt_bias, a_norm_w)
    b_args = (b_mu, b_w0, b_w2, b_a0, b_a2, b_g2, b_k_k, b_k_a, b_r_k.reshape(-1), b_ln_w, b_ln_b)
    conv_p, delta_p, shift_p, wkv_p = st_p
    conv_s, delta_s, shift_s, wkv_s, ckv_past, krope_past = st_s

    oa_p, conv_np, delta_np = _mixer_a(pa, 0, bp, lp, precise, conv_p, delta_p, *a_args)
    oa_s, conv_ns, delta_ns = _mixer_a(pa, tp, bs, ls, precise, conv_s, delta_s, *a_args)
    ob_p, shift_np, wkv_np = _mixer_b(pb, 0, bp, lp, precise, shift_p, wkv_p, *b_args)
    ob_s, shift_ns, wkv_ns = _mixer_b(pb, tp, bs, ls, precise, shift_s, wkv_s, *b_args)

    oc_p = _attention(q, k, v, 0, bp, lp, precise)
    krp_past = jnp.pad(krope_past.reshape(bs * past, C_ROPE), ((0, 0), (C_NOPE, LANES - C_NOPE - C_ROPE)))
    k_cache, v_cache = _kv_expand_call(ckv_past.reshape(bs * past, C_KV_LORA), krp_past, cw, precise)
    oc_s = _attention(q, k, v, tp, bs, ls, precise, k_cache, v_cache)

    w_router, b_router = _router_weights(router_group, router_group_bias, router_expert, router_expert_bias)
    y, xt, ri, rw = _out_proj(xs, (oa_p, ob_p, oc_p), (oa_s, ob_s, oc_s), w_out, norm_ffn,
                              w_router, b_router, precise)
    x_new = _moe(y, xt, ri, rw, e_gate, e_up, e_down, tp if split_out else None)

    new_p = (conv_np, delta_np, shift_np, wkv_np, ckv[:tp].reshape(bp, lp, C_KV_LORA),
             krope[:tp].reshape(bp, lp, C_ROPE))
    new_s = (conv_ns, delta_ns, shift_ns, wkv_ns, ckv[tp:].reshape(bs, ls, C_KV_LORA),
             krope[tp:].reshape(bs, ls, C_ROPE))
    return x_new, new_p, new_s


def _forward(x_prompt, x_sample, cache_c_kv, cache_k_rope, state_conv_a, state_delta_a, state_shift_b,
             state_wkv_b, weights):
    bp, lp, _ = x_prompt.shape
    bs, ls, _ = x_sample.shape
    depth = cache_c_kv.shape[0]
    past = cache_c_kv.shape[2]
    geom = (bp, lp, bs, ls, past)
    xs = [x_prompt.reshape(bp * lp, D_MODEL), x_sample.reshape(bs * ls, D_MODEL)]
    pos = jnp.concatenate([jnp.tile(jnp.arange(lp), bp), jnp.tile(past + jnp.arange(ls), bs)])
    cos, sin = _rope_tables(pos)
    zeros = lambda *s: jnp.zeros(s, F32)
    st_p = (zeros(bp, A_CONV - 1, A_CONV_CH), zeros(bp, A_HEADS, A_DK, A_DV), zeros(bp, 1, B_WIDTH),
            zeros(bp, B_HEADS, B_N, B_N))
    news_p, news_s = [], []
    for l in range(depth):
        st_s = (state_conv_a[l], state_delta_a[l], state_shift_b[l], state_wkv_b[l], cache_c_kv[l], cache_k_rope[l])
        last = l == depth - 1
        x, new_p, new_s = _layer(xs, geom, st_p, st_s, cos, sin, [w[l] for w in weights], precise=not last,
                                 split_out=last)
        xs = x if last else [x]
        news_p.append(new_p)
        news_s.append(new_s)
    stack = lambda news, i: jnp.stack([n[i] for n in news])
    y_prompt = xs[0].reshape(bp, lp, D_MODEL)
    y_sample = xs[1].reshape(bs, ls, D_MODEL)
    p_conv, p_delta, p_shift, p_wkv, p_ckv, p_krope = (stack(news_p, i) for i in range(6))
    s_conv, s_delta, s_shift, s_wkv, s_ckv, s_krope = (stack(news_s, i) for i in range(6))
    return (y_prompt, y_sample, p_ckv, p_krope, p_conv, p_delta, p_shift, p_wkv,
            s_ckv, s_krope, s_conv, s_delta, s_shift, s_wkv)


def kernel(x_prompt, x_sample, cache_c_kv, cache_k_rope, state_conv_a, state_delta_a, state_shift_b, state_wkv_b,
           norm_mix, w_in, a_conv_w, a_A_log, a_dt_bias, a_norm_w,
           b_mu, b_w0, b_w2, b_a0, b_a2, b_g2, b_k_k, b_k_a, b_r_k, b_ln_w, b_ln_b,
           c_q_norm, c_kv_norm, c_w_uq, c_w_ukv, c_q_gain, c_k_gain,
           w_out, norm_ffn, router_group, router_group_bias, router_expert, router_expert_bias,
           e_gate, e_up, e_down):
    weights = (norm_mix, w_in, a_conv_w, a_A_log, a_dt_bias, a_norm_w,
               b_mu, b_w0, b_w2, b_a0, b_a2, b_g2, b_k_k, b_k_a, b_r_k, b_ln_w, b_ln_b,
               c_q_norm, c_kv_norm, c_w_uq, c_w_ukv, c_q_gain, c_k_gain,
               w_out, norm_ffn, router_group, router_group_bias, router_expert, router_expert_bias,
               e_gate, e_up, e_down)
    return _forward(x_prompt, x_sample, cache_c_kv, cache_k_rope, state_conv_a, state_delta_a, state_shift_b,
                    state_wkv_b, weights)
```
